```python
import jax, jax.numpy as jnp
from jax import lax
import numpy as np

D_MODEL = 1024
BATCH = 4
SEQ = 4096
DEPTH = 2

EPS = 1e-6
GRID_W = 64
NEG = -1e30

POOL_WINDOWS = (2, 4, 8, 16)
POOL_GROUP = D_MODEL // 16
POOL_WIDTH = POOL_GROUP * len(POOL_WINDOWS)

RET_HEADS = 4
RET_DK = D_MODEL // 16
RET_DV = 2 * RET_DK
RET_QK = RET_HEADS * RET_DK
RET_V = RET_HEADS * RET_DV
RET_CHUNK = 128
ROPE_BASE = 10000.0

NA_HEADS = 4
NA_DH = D_MODEL // 16
NA_WIDTH = NA_HEADS * NA_DH
NA_ROWS_MAX = 8
NA_COLS = 16

N_BRANCH = 3
IN_SPLITS = (POOL_WIDTH, RET_QK, RET_QK, RET_V, RET_V, NA_WIDTH, NA_WIDTH, NA_WIDTH, N_BRANCH * D_MODEL)
IN_WIDTH = sum(IN_SPLITS)

PEER_KEYS = 128
PEER_EXPERTS = PEER_KEYS * PEER_KEYS
PEER_HEADS = 8
PEER_DK = 128
PEER_TOPK = 16
PEER_BLOCK = 128

kernel_name = "hybrid_pool_retention_natten_peer_encoder"


def rmsnorm(x, g):
    xf = x.astype(jnp.float32)
    y = xf * lax.rsqrt(jnp.mean(xf * xf, axis=-1, keepdims=True) + EPS)
    return (y * g.astype(jnp.float32)).astype(x.dtype)


def rms_heads(y):
    yf = y.astype(jnp.float32)
    return yf * lax.rsqrt(jnp.mean(yf * yf, axis=-1, keepdims=True) + EPS)


def rotary(x):
    S, dk = x.shape[2], x.shape[3]
    half = dk // 2
    inv = 1.0 / (ROPE_BASE ** jnp.linspace(0.0, 1.0, half, dtype=jnp.float32))
    ang = jnp.arange(S, dtype=jnp.float32)[:, None] * inv[None, :]
    cos, sin = jnp.cos(ang), jnp.sin(ang)
    x1, x2 = x[..., :half], x[..., half:]
    return jnp.concatenate([x1 * cos - x2 * sin, x1 * sin + x2 * cos], axis=-1)


def pool_mixer(xp, w_group, scale):
    B, S, _ = xp.shape
    xf = xp.astype(jnp.float32)
    cs = jnp.concatenate([jnp.zeros((B, 1, POOL_WIDTH), jnp.float32), jnp.cumsum(xf, axis=1)], axis=1)
    t = jnp.arange(S)
    outs = []
    for gi, w in enumerate(POOL_WINDOWS):
        lo, hi = w // 2, w - w // 2
        a = jnp.clip(t - lo, 0, S)
        b = jnp.clip(t + hi, 0, S)
        sl = slice(gi * POOL_GROUP, (gi + 1) * POOL_GROUP)
        csg = cs[..., sl]
        cnt = (b - a).astype(jnp.float32)[None, :, None]
        mean = (jnp.take(csg, b, axis=1) - jnp.take(csg, a, axis=1)) / cnt
        outs.append(jnp.einsum('bsc,ce->bse', mean - xf[..., sl], w_group[gi].astype(jnp.float32)))
    return (jnp.concatenate(outs, axis=-1) * scale).astype(xp.dtype)


def retention_direction(q, k, v, log_gamma, strict):
    B, H, S, dk = q.shape
    dv = v.shape[-1]
    C = RET_CHUNK
    n = S // C
    qc = q.reshape(B, H, n, C, dk)
    kc = k.reshape(B, H, n, C, dk)
    vc = v.reshape(B, H, n, C, dv)
    pos = jnp.arange(C, dtype=jnp.float32)
    lg = log_gamma.astype(jnp.float32)
    diff = pos[:, None] - pos[None, :]
    mask = (diff > 0) if strict else (diff >= 0)
    d_intra = jnp.where(mask, jnp.exp(lg[:, None, None] * jnp.where(mask, diff, 0.0)), 0.0)
    scores = jnp.einsum('bhncd,bhnmd->bhncm', qc, kc) * d_intra[None, :, None]
    intra = jnp.einsum('bhncm,bhnme->bhnce', scores, vc)
    lgv = lg[None, :, None, None, None]
    k_dec = kc * jnp.exp(lgv * (C - 1 - pos)[:, None])
    chunk_kv = jnp.einsum('bhncd,bhnce->nbhde', k_dec, vc)
    chunk_decay = jnp.exp(lg * C)[None, :, None, None]

    def step(state, kv):
        return state * chunk_decay + kv, state

    _, prev = lax.scan(step, jnp.zeros((B, H, dk, dv), jnp.float32), chunk_kv.astype(jnp.float32))
    q_dec = qc * jnp.exp(lgv * (pos + 1.0)[:, None])
    cross = jnp.einsum('bhncd,nbhde->bhnce', q_dec, prev)
    return (intra + cross).reshape(B, H, S, dv)


def retention_mixer(q, k, v, g, decay_logit):
    B, S, _ = q.shape
    heads = lambda t, d: t.reshape(B, S, RET_HEADS, d).transpose(0, 2, 1, 3)
    qh = rotary(heads(q, RET_DK))
    kh = rotary(heads(k, RET_DK)) * (RET_DK ** -0.5)
    vh = heads(v, RET_DV)
    lg = jax.nn.log_sigmoid(decay_logit.astype(jnp.float32))
    fwd = retention_direction(qh, kh, vh, lg[0], strict=False)
    bwd = retention_direction(jnp.flip(qh, 2), jnp.flip(kh, 2), jnp.flip(vh, 2), lg[1], strict=True)
    y = rms_heads(fwd + jnp.flip(bwd, 2))
    y = y.transpose(0, 2, 1, 3).reshape(B, S, RET_V)
    return (jax.nn.silu(g.astype(jnp.float32)) * y).astype(q.dtype)


def neighbourhood_attention(q, k, v, rpb):
    B, S, _ = q.shape
    rows = S // GRID_W
    kr = min(NA_ROWS_MAX, rows)
    grid = lambda t: t.reshape(B, rows, GRID_W, NA_HEADS, NA_DH).transpose(0, 3, 1, 2, 4)
    r = jnp.arange(rows)
    row_idx = jnp.clip(r - kr // 2, 0, rows - kr)[:, None] + jnp.arange(kr)[None, :]
    c = jnp.arange(GRID_W)
    col_start = jnp.clip(c - NA_COLS // 2, 0, GRID_W - NA_COLS)
    col_in = (c[None, :] >= col_start[:, None]) & (c[None, :] < col_start[:, None] + NA_COLS)
    di = row_idx - r[:, None] + (NA_ROWS_MAX - 1)
    dj = jnp.clip(c[None, :] - c[:, None], -(NA_COLS - 1), NA_COLS - 1) + (NA_COLS - 1)
    bias = rpb[:, di[:, None, :, None], dj[None, :, None, :]].astype(jnp.float32)
    bias = jnp.where(col_in[None, None, :, None, :], bias, NEG)
    qg = grid(q) * (NA_DH ** -0.5)
    kg = grid(k)[:, :, row_idx]
    vg = grid(v)[:, :, row_idx]
    s = jnp.einsum('bhrcd,bhrivd->bhrciv', qg, kg).astype(jnp.float32) + bias[None]
    p = jax.nn.softmax(s.reshape(B, NA_HEADS, rows, GRID_W, kr * GRID_W), axis=-1)
    p = p.reshape(s.shape).astype(v.dtype)
    o = jnp.einsum('bhrciv,bhrivd->bhrcd', p, vg)
    return o.transpose(0, 2, 3, 1, 4).reshape(B, S, NA_WIDTH)


def mixer_block(xn, w_in, pool_w, pool_scale, ret_decay, na_rpb, w_br_pool, w_br_ret, w_br_na, w_out):
    B, S, D = xn.shape
    proj = jnp.einsum('bsd,de->bse', xn, w_in)
    splits = [int(i) for i in np.cumsum(IN_SPLITS)[:-1]]
    p_pool, r_q, r_k, r_v, r_g, n_q, n_k, n_v, gate_logits = jnp.split(proj, splits, axis=-1)
    y_pool = pool_mixer(p_pool, pool_w, pool_scale)
    y_ret = retention_mixer(r_q, r_k, r_v, r_g, ret_decay)
    y_na = neighbourhood_attention(n_q, n_k, n_v, na_rpb)
    gates = jax.nn.sigmoid(gate_logits.astype(jnp.float32)).reshape(B, S, N_BRANCH, D).astype(xn.dtype)
    merged = (gates[:, :, 0] * jnp.einsum('bsc,cd->bsd', y_pool, w_br_pool)
              + gates[:, :, 1] * jnp.einsum('bsc,cd->bsd', y_ret, w_br_ret)
              + gates[:, :, 2] * jnp.einsum('bsc,cd->bsd', y_na, w_br_na))
    return jnp.einsum('bsd,de->bse', merged, w_out).astype(xn.dtype)


def peer_ffn(xn, w_query, sub_keys, w_u, w_v):
    B, S, D = xn.shape
    T = B * S
    xt = xn.reshape(T, D)
    q = jnp.einsum('td,de->te', xt, w_query).reshape(T, PEER_HEADS, 2, PEER_DK)
    s = jnp.einsum('thpd,pnd->thpn', q, sub_keys).astype(jnp.float32)
    s1, i1 = lax.top_k(s[:, :, 0], PEER_TOPK)
    s2, i2 = lax.top_k(s[:, :, 1], PEER_TOPK)
    cand_s = (s1[..., :, None] + s2[..., None, :]).reshape(T, PEER_HEADS, PEER_TOPK * PEER_TOPK)
    cand_i = (i1[..., :, None] * PEER_KEYS + i2[..., None, :]).reshape(T, PEER_HEADS, PEER_TOPK * PEER_TOPK)
    top_s, top_pos = lax.top_k(cand_s, PEER_TOPK)
    idx = jnp.take_along_axis(cand_i, top_pos, axis=-1).reshape(T, PEER_HEADS * PEER_TOPK)
    gate = jax.nn.softmax(top_s, axis=-1).reshape(T, PEER_HEADS * PEER_TOPK)
    nb = T // PEER_BLOCK

    def expert_block(args):
        xb, ib, gb = args
        a = jnp.einsum('td,tkd->tk', xb, w_u[ib]).astype(jnp.float32)
        h = (jax.nn.gelu(a, approximate=False) * gb).astype(xb.dtype)
        return jnp.einsum('tk,tkd->td', h, w_v[ib]).astype(xb.dtype)

    out = lax.map(expert_block, (xt.reshape(nb, PEER_BLOCK, D),
                                 idx.reshape(nb, PEER_BLOCK, -1),
                                 gate.reshape(nb, PEER_BLOCK, -1)))
    return out.reshape(B, S, D)


def setup_inputs(seed: int = 0) -> dict:
    key = jax.random.key(seed)
    ks = jax.random.split(key, 20)
    f32 = jnp.float32
    nrm = lambda k, shape, sc: jax.random.normal(k, shape, f32) * sc
    gam = 1.0 - 2.0 ** (-5.0 - np.arange(RET_HEADS, dtype=np.float64))
    logit0 = jnp.asarray(np.log(gam / (1.0 - gam)), dtype=f32)
    return {
        "x": nrm(ks[0], (BATCH, SEQ, D_MODEL), 1.0),
        "norm_mix": 1.0 + nrm(ks[1], (DEPTH, D_MODEL), 0.02),
        "w_in": nrm(ks[2], (DEPTH, D_MODEL, IN_WIDTH), D_MODEL ** -0.5),
        "pool_w": nrm(ks[3], (DEPTH, len(POOL_WINDOWS), POOL_GROUP, POOL_GROUP), POOL_GROUP ** -0.5),
        "pool_scale": 1.0 + nrm(ks[4], (DEPTH, POOL_WIDTH), 0.02),
        "ret_decay": logit0[None, None, :] + nrm(ks[5], (DEPTH, 2, RET_HEADS), 0.1),
        "na_rpb": nrm(ks[6], (DEPTH, NA_HEADS, 2 * NA_ROWS_MAX - 1, 2 * NA_COLS - 1), 0.1),
        "w_br_pool": nrm(ks[7], (DEPTH, POOL_WIDTH, D_MODEL), POOL_WIDTH ** -0.5),
        "w_br_ret": nrm(ks[8], (DEPTH, RET_V, D_MODEL), RET_V ** -0.5),
        "w_br_na": nrm(ks[9], (DEPTH, NA_WIDTH, D_MODEL), NA_WIDTH ** -0.5),
        "w_out": nrm(ks[10], (DEPTH, D_MODEL, D_MODEL), D_MODEL ** -0.5),
        "norm_ffn": 1.0 + nrm(ks[11], (DEPTH, D_MODEL), 0.02),
        "peer_w_query": nrm(ks[12], (DEPTH, D_MODEL, PEER_HEADS * 2 * PEER_DK), D_MODEL ** -0.5),
        "peer_sub_keys": nrm(ks[13], (DEPTH, 2, PEER_KEYS, PEER_DK), PEER_DK ** -0.5),
        "peer_u": nrm(ks[14], (DEPTH, PEER_EXPERTS, D_MODEL), D_MODEL ** -0.5),
        "peer_v": nrm(ks[15], (DEPTH, PEER_EXPERTS, D_MODEL), 0.25),
        "norm_final": 1.0 + nrm(ks[16], (D_MODEL,), 0.02),
    }


def reference(x, norm_mix, w_in, pool_w, pool_scale, ret_decay, na_rpb, w_br_pool, w_br_ret, w_br_na,
              w_out, norm_ffn, peer_w_query, peer_sub_keys, peer_u, peer_v, norm_final):
    for l in range(DEPTH):
        xn = rmsnorm(x, norm_mix[l])
        x = x + mixer_block(xn, w_in[l], pool_w[l], pool_scale[l], ret_decay[l], na_rpb[l],
                            w_br_pool[l], w_br_ret[l], w_br_na[l], w_out[l])
        hn = rmsnorm(x, norm_ffn[l])
        x = x + peer_ffn(hn, peer_w_query[l], peer_sub_keys[l], peer_u[l], peer_v[l])
    return rmsnorm(x, norm_final)
```

```python
import functools

import numpy as np
import jax
import jax.numpy as jnp
from jax import lax
from jax.experimental import pallas as pl
from jax.experimental.pallas import tpu as pltpu

F32 = jnp.float32
BF16 = jnp.bfloat16

EPS = 1e-6
GRID_W = 64
NEG = -1e30

POOL_WINDOWS = (2, 4, 8, 16)
POOL_GROUP = 64
POOL_WIDTH = POOL_GROUP * len(POOL_WINDOWS)

RET_HEADS = 4
RET_DK = 64
RET_DV = 128
RET_QK = RET_HEADS * RET_DK
RET_V = RET_HEADS * RET_DV
ROPE_BASE = 10000.0

NA_HEADS = 4
NA_DH = 64
NA_WIDTH = NA_HEADS * NA_DH
NA_ROWS_MAX = 8
NA_COLS = 16

N_BRANCH = 3

PEER_KEYS = 128
PEER_HEADS = 8
PEER_DK = 128
PEER_TOPK = 16

V7X_VMEM_LIMIT_BYTES = 56 * 1024 * 1024

INPROJ_TM = 256
RET_CHUNK = 256
NA_ROWS_PER_STEP = 8
MERGE_TM = 256
TOPK_TM = 512
TOPK_SUB = 128
PEER_TM = 512
PEER_TE = 1024


def _params(*sem):
    return pltpu.CompilerParams(dimension_semantics=sem, vmem_limit_bytes=V7X_VMEM_LIMIT_BYTES)


def _const_spec(shape):
    nd = len(shape)
    return pl.BlockSpec(shape, lambda *_: (0,) * nd)


def _inproj_kernel(x_ref, g_ref, w_ref, pool_ref, rq_ref, rk_ref, rv_ref, rg_ref,
                   nq_ref, nk_ref, nv_ref, gate_ref):
    x = x_ref[...]
    xn = x * lax.rsqrt(jnp.mean(x * x, axis=-1, keepdims=True) + EPS) * g_ref[...]
    xb = xn.astype(BF16)
    off = 0
    for ref, scale in ((pool_ref, None), (rq_ref, None), (rk_ref, None), (rv_ref, None), (rg_ref, None),
                       (nq_ref, NA_DH ** -0.5), (nk_ref, None), (nv_ref, None)):
        width = ref.shape[-1]
        y = jnp.dot(xb, w_ref[:, off:off + width], preferred_element_type=F32)
        if scale is not None:
            y = y * scale
        ref[...] = y.astype(ref.dtype)
        off += width
    d = x.shape[-1]
    for b in range(N_BRANCH):
        gate_ref[:, b * d:(b + 1) * d] = jnp.dot(
            xb, w_ref[:, off + b * d:off + (b + 1) * d], preferred_element_type=F32)


def _inproj(x2, g, w_in_bf):
    t, d = x2.shape
    tm = INPROJ_TM
    widths = (POOL_WIDTH, RET_QK, RET_QK, RET_V, RET_V, NA_WIDTH, NA_WIDTH, NA_WIDTH, N_BRANCH * d)
    dtypes = (F32, F32, F32, BF16, F32, BF16, BF16, BF16, F32)
    tok = lambda w: pl.BlockSpec((tm, w), lambda i: (i, 0))
    return pl.pallas_call(
        _inproj_kernel,
        grid=(t // tm,),
        in_specs=[tok(d), _const_spec((1, d)), _const_spec(w_in_bf.shape)],
        out_specs=[tok(w) for w in widths],
        out_shape=[jax.ShapeDtypeStruct((t, w), dt) for w, dt in zip(widths, dtypes)],
        compiler_params=_params("parallel"),
        name="inproj",
    )(x2, g.reshape(1, d), w_in_bf)


def _pool_kernel(p_ref, w_ref, scale_ref, o_ref, *, halves):
    x = p_ref[...]
    s = x.shape[0]
    t = lax.broadcasted_iota(jnp.int32, x.shape, 0)
    lane = lax.broadcasted_iota(jnp.int32, x.shape, 1)
    half = pl.program_id(1)

    def down(v, k):
        return jnp.where(t >= k, pltpu.roll(v, k, 0), 0.0)

    def up(v, k):
        return jnp.where(t < s - k, pltpu.roll(v, s - k, 0), 0.0)

    trail = {1: x}
    lead = {1: x}
    for m in (2, 4, 8):
        trail[m] = trail[m // 2] + down(trail[m // 2], m // 2)
        lead[m] = lead[m // 2] + up(lead[m // 2], m // 2)

    def window(hw):
        return down(trail[hw], 1) + lead[hw]

    def pick(lo_first, hi_first, lo_second, hi_second):
        first = jnp.where(lane < POOL_GROUP, lo_first, hi_first)
        second = jnp.where(lane < POOL_GROUP, lo_second, hi_second)
        return jnp.where(half == 0, first, second)

    hw = [w // 2 for w in POOL_WINDOWS]
    wsum = pick(window(hw[0]), window(hw[1]), window(hw[2]), window(hw[3]))
    hwv = pick(hw[0], hw[1], hw[2], hw[3])
    cnt = (jnp.minimum(t + hwv, s) - jnp.maximum(t - hwv, 0)).astype(F32)
    dlt = wsum / cnt - x
    y = jnp.dot(dlt.astype(BF16), w_ref[0], preferred_element_type=F32)
    o_ref[...] = y * scale_ref[0]


def _pool(p_pool, w_halves, scale_halves, batch, seq):
    t = p_pool.shape[0]
    halves = POOL_WIDTH // 128
    return pl.pallas_call(
        functools.partial(_pool_kernel, halves=halves),
        grid=(batch, halves),
        in_specs=[pl.BlockSpec((seq, 128), lambda b, h: (b, h)),
                  pl.BlockSpec((1, 128, 128), lambda b, h: (h, 0, 0)),
                  pl.BlockSpec((1, 1, 128), lambda b, h: (h, 0, 0))],
        out_specs=pl.BlockSpec((seq, 128), lambda b, h: (b, h)),
        out_shape=jax.ShapeDtypeStruct((t, POOL_WIDTH), F32),
        compiler_params=_params("parallel", "parallel"),
        name="pool",
    )(p_pool, w_halves, scale_halves)


def _rotary(x, cos, sin_signed, first_half):
    width = x.shape[-1]
    half = RET_DK // 2
    swapped = jnp.where(first_half, pltpu.roll(x, width - half, 1), pltpu.roll(x, half, 1))
    return x * cos + swapped * sin_signed


def _retention_kernel(lg_ref, q_ref, k_ref, v_ref, cos_ref, sin_ref, *rest, reverse):
    if reverse:
        fwd_ref, g_ref, o_ref, state_ref = rest
    else:
        o_ref, state_ref = rest
    c = q_ref.shape[0]

    @pl.when(pl.program_id(1) == 0)
    def _():
        state_ref[...] = jnp.zeros_like(state_ref)

    lane = lax.broadcasted_iota(jnp.int32, (c, RET_QK), 1)
    first_half = (lane % RET_DK) < (RET_DK // 2)
    cos = cos_ref[...]
    sin = sin_ref[...]
    q = _rotary(q_ref[...], cos, sin, first_half)
    k = _rotary(k_ref[...], cos, sin, first_half) * (RET_DK ** -0.5)
    v = v_ref[...]

    row = lax.broadcasted_iota(jnp.int32, (c, c), 0)
    col = lax.broadcasted_iota(jnp.int32, (c, c), 1)
    pos = lax.broadcasted_iota(jnp.int32, (c, 1), 0).astype(F32)
    if reverse:
        diff = col - row
        mask = diff > 0
        q_pow = c - pos
        k_pow = pos
    else:
        diff = row - col
        mask = diff >= 0
        q_pow = pos + 1.0
        k_pow = (c - 1) - pos
    dist = jnp.where(mask, diff, 0).astype(F32)
    direction = 1 if reverse else 0

    for h in range(RET_HEADS):
        lg = lg_ref[direction, h]
        qh = q[:, h * RET_DK:(h + 1) * RET_DK]
        kh = k[:, h * RET_DK:(h + 1) * RET_DK]
        vh = v[:, h * RET_DV:(h + 1) * RET_DV]
        decay = jnp.where(mask, jnp.exp(lg * dist), 0.0)
        scores = lax.dot_general(qh.astype(BF16), kh.astype(BF16), (((1,), (1,)), ((), ())),
                                 preferred_element_type=F32) * decay
        intra = jnp.dot(scores.astype(BF16), vh, preferred_element_type=F32)
        q_dec = qh * jnp.exp(lg * q_pow)
        k_dec = kh * jnp.exp(lg * k_pow)
        state = state_ref[h]
        cross = jnp.dot(q_dec.astype(BF16), state.astype(BF16), preferred_element_type=F32)
        chunk_kv = lax.dot_general(k_dec.astype(BF16), vh, (((0,), (0,)), ((), ())),
                                   preferred_element_type=F32)
        chunk_decay = jnp.exp(lg * jnp.full((1, RET_DV), float(c), F32))
        state_ref[h] = state * chunk_decay + chunk_kv
        out = intra + cross
        sl = slice(h * RET_DV, (h + 1) * RET_DV)
        if reverse:
            y = fwd_ref[:, sl] + out
            y = y * lax.rsqrt(jnp.mean(y * y, axis=-1, keepdims=True) + EPS)
            g = g_ref[:, sl]
            o_ref[:, sl] = (g / (1.0 + jnp.exp(-g))) * y
        else:
            o_ref[:, sl] = out


def _retention(lg, r_q, r_k, r_v, r_g, cos_t, sin_t, batch, seq):
    t = r_q.shape[0]
    c = RET_CHUNK
    n = seq // c
    scratch = [pltpu.VMEM((RET_HEADS, RET_DK, RET_DV), F32)]
    smem = pl.BlockSpec(memory_space=pltpu.SMEM)

    def specs(chunk_of):
        tok = lambda w: pl.BlockSpec((c, w), lambda b, i: (b * n + chunk_of(i), 0))
        tab = pl.BlockSpec((c, RET_QK), lambda b, i: (chunk_of(i), 0))
        return tok, tab

    tok, tab = specs(lambda i: i)
    fwd = pl.pallas_call(
        functools.partial(_retention_kernel, reverse=False),
        grid=(batch, n),
        in_specs=[smem, tok(RET_QK), tok(RET_QK), tok(RET_V), tab, tab],
        out_specs=tok(RET_V),
        out_shape=jax.ShapeDtypeStruct((t, RET_V), F32),
        scratch_shapes=scratch,
        compiler_params=_params("parallel", "arbitrary"),
        name="retention_fwd",
    )(lg, r_q, r_k, r_v, cos_t, sin_t)
    tok, tab = specs(lambda i: n - 1 - i)
    return pl.pallas_call(
        functools.partial(_retention_kernel, reverse=True),
        grid=(batch, n),
        in_specs=[smem, tok(RET_QK), tok(RET_QK), tok(RET_V), tab, tab, tok(RET_V), tok(RET_V)],
        out_specs=tok(RET_V),
        out_shape=jax.ShapeDtypeStruct((t, RET_V), F32),
        scratch_shapes=scratch,
        compiler_params=_params("parallel", "arbitrary"),
        name="retention_bwd",
    )(lg, r_q, r_k, r_v, cos_t, sin_t, fwd, r_g)


def _na_kernel(q_ref, k_ref, v_ref, bias_ref, o_ref, *, rows, kr):
    step = pl.program_id(1)
    rps = q_ref.shape[0] // GRID_W
    nkeys = kr * GRID_W
    lo = kr // 2
    hi_rows = rows - kr - lo

    def body(i, carry):
        r = step * rps + i
        r0 = jnp.clip(r - lo, 0, rows - kr)
        ver = jnp.where(r < lo, r, jnp.where(r > rows - kr + lo, r - (rows - kr), lo))
        kk = k_ref[pl.ds(pl.multiple_of(r0 * GRID_W, GRID_W), nkeys), :]
        vv = v_ref[pl.ds(pl.multiple_of(r0 * GRID_W, GRID_W), nkeys), :]
        qq = q_ref[pl.ds(pl.multiple_of(i * GRID_W, GRID_W), GRID_W), :]
        outs = []
        for h in range(NA_HEADS):
            sl = slice(h * NA_DH, (h + 1) * NA_DH)
            s = lax.dot_general(qq[:, sl], kk[:, sl], (((1,), (1,)), ((), ())),
                                preferred_element_type=F32) + bias_ref[ver, h]
            m = jnp.max(s, axis=-1, keepdims=True)
            p = jnp.exp(s - m)
            p = p / jnp.sum(p, axis=-1, keepdims=True)
            outs.append(jnp.dot(p.astype(BF16), vv[:, sl], preferred_element_type=F32))
        o_ref[pl.ds(pl.multiple_of(i * GRID_W, GRID_W), GRID_W), :] = jnp.concatenate(outs, axis=-1)
        return carry

    del hi_rows
    lax.fori_loop(0, rps, body, 0)


def _na_bias_table(rpb, rows, kr):
    lo = kr // 2
    reps = list(range(lo)) + [lo] + list(range(rows - kr + lo + 1, rows))
    r = jnp.asarray(reps, jnp.int32)
    row_idx = jnp.clip(r - lo, 0, rows - kr)[:, None] + jnp.arange(kr)[None, :]
    c = jnp.arange(GRID_W)
    col_start = jnp.clip(c - NA_COLS // 2, 0, GRID_W - NA_COLS)
    col_in = (c[None, :] >= col_start[:, None]) & (c[None, :] < col_start[:, None] + NA_COLS)
    di = row_idx - r[:, None] + (NA_ROWS_MAX - 1)
    dj = jnp.clip(c[None, :] - c[:, None], -(NA_COLS - 1), NA_COLS - 1) + (NA_COLS - 1)
    bias = rpb[:, di[:, None, :, None], dj[None, :, None, :]].astype(F32)
    bias = jnp.where(col_in[None, None, :, None, :], bias, NEG)
    nver = len(reps)
    return bias.transpose(1, 0, 2, 3, 4).reshape(nver, NA_HEADS, GRID_W, kr * GRID_W)


def _na(n_q, n_k, n_v, bias, batch, seq):
    t = n_q.shape[0]
    rows = seq // GRID_W
    kr = min(NA_ROWS_MAX, rows)
    rps = min(NA_ROWS_PER_STEP, rows)
    blk = rps * GRID_W
    steps = rows // rps
    return pl.pallas_call(
        functools.partial(_na_kernel, rows=rows, kr=kr),
        grid=(batch, steps),
        in_specs=[pl.BlockSpec((blk, NA_WIDTH), lambda b, i: (b * steps + i, 0)),
                  pl.BlockSpec((seq, NA_WIDTH), lambda b, i: (b, 0)),
                  pl.BlockSpec((seq, NA_WIDTH), lambda b, i: (b, 0)),
                  _const_spec(bias.shape)],
        out_specs=pl.BlockSpec((blk, NA_WIDTH), lambda b, i: (b * steps + i, 0)),
        out_shape=jax.ShapeDtypeStruct((t, NA_WIDTH), F32),
        compiler_params=_params("parallel", "arbitrary"),
        name="natten",
    )(n_q, n_k, n_v, bias)


def _merge_kernel(x_ref, yp_ref, yr_ref, yn_ref, gate_ref, wp_ref, wr_ref, wn_ref, wo_ref,
                  gf_ref, wq_ref, keys_ref, h_ref, hnt_ref, st_ref):
    d = x_ref.shape[-1]
    merged = None
    for b, (y_ref, w_ref) in enumerate(((yp_ref, wp_ref), (yr_ref, wr_ref), (yn_ref, wn_ref))):
        gate = 1.0 / (1.0 + jnp.exp(-gate_ref[:, b * d:(b + 1) * d]))
        term = gate * jnp.dot(y_ref[...].astype(BF16), w_ref[...], preferred_element_type=F32)
        merged = term if merged is None else merged + term
    h = x_ref[...] + jnp.dot(merged.astype(BF16), wo_ref[...], preferred_element_type=F32)
    h_ref[...] = h
    hn = h * lax.rsqrt(jnp.mean(h * h, axis=-1, keepdims=True) + EPS) * gf_ref[...]
    hnb = hn.astype(BF16)
    hnt_ref[...] = hn.T.astype(BF16)
    q = jnp.dot(hnb, wq_ref[...], preferred_element_type=F32).astype(BF16)
    for hh in range(PEER_HEADS):
        for p in range(2):
            col = (hh * 2 + p) * PEER_DK
            st_ref[hh, p] = lax.dot_general(keys_ref[p], q[:, col:col + PEER_DK],
                                            (((1,), (1,)), ((), ())), preferred_element_type=F32)


def _merge(x2, y_pool, y_ret, y_na, gates, wp, wr, wn, wo, g_ffn, wq, keys):
    t, d = x2.shape
    tm = MERGE_TM
    tok = lambda w: pl.BlockSpec((tm, w), lambda i: (i, 0))
    return pl.pallas_call(
        _merge_kernel,
        grid=(t // tm,),
        in_specs=[tok(d), tok(POOL_WIDTH), tok(RET_V), tok(NA_WIDTH), tok(N_BRANCH * d),
                  _const_spec(wp.shape), _const_spec(wr.shape), _const_spec(wn.shape), _const_spec(wo.shape),
                  _const_spec((1, d)), _const_spec(wq.shape), _const_spec(keys.shape)],
        out_specs=[tok(d),
                   pl.BlockSpec((d, tm), lambda i: (0, i)),
                   pl.BlockSpec((PEER_HEADS, 2, PEER_KEYS, tm), lambda i: (0, 0, 0, i))],
        out_shape=[jax.ShapeDtypeStruct((t, d), F32),
                   jax.ShapeDtypeStruct((d, t), BF16),
                   jax.ShapeDtypeStruct((PEER_HEADS, 2, PEER_KEYS, t), F32)],
        compiler_params=_params("parallel"),
        name="merge",
    )(x2, y_pool, y_ret, y_na, gates, wp, wr, wn, wo, g_ffn.reshape(1, d), wq, keys)


def _topk_kernel(s_ref, r2_ref, e2_ref, cc_ref, e1_ref):
    nk = s_ref.shape[2]
    k_top = PEER_TOPK
    w = TOPK_SUB
    key = lax.broadcasted_iota(jnp.int32, (nk, w), 0)
    arow = lax.broadcasted_iota(jnp.int32, (k_top, w), 0)
    neg_inf = -jnp.inf

    def top_ranks(s):
        def body(a, carry):
            s, rank, vals = carry
            m = jnp.max(s, axis=0, keepdims=True)
            idx = jnp.min(jnp.where(s == m, key, nk), axis=0, keepdims=True)
            hit = key == idx
            return (jnp.where(hit, neg_inf, s), jnp.where(hit, a, rank), jnp.where(arow == a, m, vals))
        init = (s, jnp.full((nk, w), k_top, jnp.int32), jnp.zeros((k_top, w), F32))
        _, rank, vals = lax.fori_loop(0, k_top, body, init)
        return rank, vals

    for j in range(s_ref.shape[-1] // w):
        ds = slice(j * w, (j + 1) * w)
        s1 = s_ref[0, 0, :, ds]
        s2 = s_ref[0, 1, :, ds]
        r1, v1 = top_ranks(s1)
        r2, v2 = top_ranks(s2)
        top = v1[0:1] + v2[0:1]

        def merge(_, carry):
            cnt, front, z = carry
            m = jnp.max(front, axis=0, keepdims=True)
            a_star = jnp.min(jnp.where(front == m, arow, k_top), axis=0, keepdims=True)
            hit = arow == a_star
            cnt = jnp.where(hit, cnt + 1, cnt)
            nxt = jnp.sum(jnp.where(hit, cnt, 0), axis=0, keepdims=True)
            v2_next = jnp.sum(jnp.where(arow == nxt, v2, 0.0), axis=0, keepdims=True)
            cand = jnp.where(nxt < k_top, v1 + v2_next, neg_inf)
            return cnt, jnp.where(hit, cand, front), z + jnp.exp(m - top)
        init = (jnp.zeros((k_top, w), jnp.int32), v1 + v2[0:1], jnp.zeros((1, w), F32))
        cnt, _, z = lax.fori_loop(0, k_top, merge, init)

        cc = jnp.zeros((nk, w), jnp.int32)
        for a in range(k_top):
            cc = jnp.where(r1 == a, cnt[a:a + 1], cc)
        r2_ref[0, :, ds] = r2.astype(F32)
        cc_ref[0, :, ds] = cc.astype(F32)
        e2_ref[0, :, ds] = jnp.exp(s2 - v2[0:1])
        e1_ref[0, :, ds] = jnp.exp(s1 - v1[0:1]) / z


def _topk(st):
    heads, _, nk, t = st.shape
    tm = TOPK_TM
    out = pl.BlockSpec((1, nk, tm), lambda i, h: (h, 0, i))
    return pl.pallas_call(
        _topk_kernel,
        grid=(t // tm, heads),
        in_specs=[pl.BlockSpec((1, 2, nk, tm), lambda i, h: (h, 0, 0, i))],
        out_specs=[out] * 4,
        out_shape=[jax.ShapeDtypeStruct((heads, nk, t), F32)] * 4,
        compiler_params=_params("parallel", "parallel"),
        name="peer_topk",
    )(st)


def _peer_kernel(hnt_ref, u_ref, vt_ref, r2_ref, e2_ref, cc_ref, e1_ref, h_ref, o_ref, acc_ref, ht_ref):
    e = pl.program_id(1)
    te = u_ref.shape[0]
    nk = r2_ref.shape[1]
    rows_per_step = te // nk

    @pl.when(e == 0)
    def _():
        acc_ref[...] = jnp.zeros_like(acc_ref)

    a_t = jnp.dot(u_ref[...], hnt_ref[...], preferred_element_type=F32)
    for ii in range(rows_per_step):
        i = e * rows_per_step + ii
        gate = None
        for hh in range(PEER_HEADS):
            cc = cc_ref[hh, pl.ds(i, 1), :]
            e1 = e1_ref[hh, pl.ds(i, 1), :]
            term = jnp.where(r2_ref[hh] < cc, e2_ref[hh], 0.0) * e1
            gate = term if gate is None else gate + term
        a = a_t[ii * nk:(ii + 1) * nk, :]
        act = 0.5 * a * (1.0 + lax.erf(a * np.float32(1.0 / np.sqrt(2.0))))
        ht_ref[ii * nk:(ii + 1) * nk, :] = (act * gate).astype(BF16)
    acc_ref[...] += jnp.dot(vt_ref[...], ht_ref[...], preferred_element_type=F32)

    @pl.when(e == pl.num_programs(1) - 1)
    def _():
        o_ref[...] = h_ref[...] + acc_ref[...].T


def _peer(hnt, u_bf, vt_bf, r2, e2, cc, e1, h2):
    d, t = hnt.shape
    n_exp = u_bf.shape[0]
    heads, nk, _ = r2.shape
    tm, te = PEER_TM, PEER_TE
    tab = pl.BlockSpec((heads, nk, tm), lambda i, e: (0, 0, i))
    return pl.pallas_call(
        _peer_kernel,
        grid=(t // tm, n_exp // te),
        in_specs=[pl.BlockSpec((d, tm), lambda i, e: (0, i)),
                  pl.BlockSpec((te, d), lambda i, e: (e, 0)),
                  pl.BlockSpec((d, te), lambda i, e: (0, e)),
                  tab, tab, tab, tab,
                  pl.BlockSpec((tm, d), lambda i, e: (i, 0))],
        out_specs=pl.BlockSpec((tm, d), lambda i, e: (i, 0)),
        out_shape=jax.ShapeDtypeStruct((t, d), F32),
        scratch_shapes=[pltpu.VMEM((d, tm), F32), pltpu.VMEM((te, tm), BF16)],
        compiler_params=_params("parallel", "arbitrary"),
        name="peer_experts",
    )(hnt, u_bf, vt_bf, r2, e2, cc, e1, h2)


def _rmsnorm_kernel(x_ref, g_ref, o_ref):
    x = x_ref[...]
    o_ref[...] = x * lax.rsqrt(jnp.mean(x * x, axis=-1, keepdims=True) + EPS) * g_ref[...]


def _rmsnorm(x2, g):
    t, d = x2.shape
    tm = 512
    return pl.pallas_call(
        _rmsnorm_kernel,
        grid=(t // tm,),
        in_specs=[pl.BlockSpec((tm, d), lambda i: (i, 0)), _const_spec((1, d))],
        out_specs=pl.BlockSpec((tm, d), lambda i: (i, 0)),
        out_shape=jax.ShapeDtypeStruct((t, d), F32),
        compiler_params=_params("parallel"),
        name="final_norm",
    )(x2, g.reshape(1, d))


def _rotary_tables(seq):
    half = RET_DK // 2
    inv = 1.0 / (ROPE_BASE ** jnp.linspace(0.0, 1.0, half, dtype=F32))
    ang = jnp.arange(seq, dtype=F32)[:, None] * inv[None, :]
    cos, sin = jnp.cos(ang), jnp.sin(ang)
    cos_t = jnp.tile(jnp.concatenate([cos, cos], axis=-1), (1, RET_HEADS))
    sin_t = jnp.tile(jnp.concatenate([-sin, sin], axis=-1), (1, RET_HEADS))
    return cos_t, sin_t


def _pool_weights(pool_w, pool_scale):
    halves = POOL_WIDTH // 128
    per = 128 // POOL_GROUP
    blocks = []
    for hf in range(halves):
        blk = jnp.zeros((128, 128), F32)
        for g in range(per):
            sl = slice(g * POOL_GROUP, (g + 1) * POOL_GROUP)
            blk = blk.at[sl, sl].set(pool_w[hf * per + g])
        blocks.append(blk)
    return jnp.stack(blocks).astype(BF16), pool_scale.reshape(halves, 1, 128)


def kernel(x, norm_mix, w_in, pool_w, pool_scale, ret_decay, na_rpb, w_br_pool, w_br_ret, w_br_na,
           w_out, norm_ffn, peer_w_query, peer_sub_keys, peer_u, peer_v, norm_final):
    batch, seq, d = x.shape
    depth = w_in.shape[0]
    rows = seq // GRID_W
    kr = min(NA_ROWS_MAX, rows)
    cos_t, sin_t = _rotary_tables(seq)
    x2 = x.reshape(batch * seq, d)
    for l in range(depth):
        pool, r_q, r_k, r_v, r_g, n_q, n_k, n_v, gates = _inproj(x2, norm_mix[l], w_in[l].astype(BF16))
        w_halves, scale_halves = _pool_weights(pool_w[l], pool_scale[l])
        y_pool = _pool(pool, w_halves, scale_halves, batch, seq)
        lg = jax.nn.log_sigmoid(ret_decay[l].astype(F32))
        y_ret = _retention(lg, r_q, r_k, r_v, r_g, cos_t, sin_t, batch, seq)
        y_na = _na(n_q, n_k, n_v, _na_bias_table(na_rpb[l], rows, kr), batch, seq)
        h2, hnt, st = _merge(x2, y_pool, y_ret, y_na, gates,
                             w_br_pool[l].astype(BF16), w_br_ret[l].astype(BF16), w_br_na[l].astype(BF16),
                             w_out[l].astype(BF16), norm_ffn[l], peer_w_query[l].astype(BF16),
                             peer_sub_keys[l].astype(BF16))
        r2, e2, cc, e1 = _topk(st)
        x2 = _peer(hnt, peer_u[l].astype(BF16), peer_v[l].T.astype(BF16), r2, e2, cc, e1, h2)
    return _rmsnorm(x2, norm_final).reshape(batch, seq, d)
```

```python
import functools

import numpy as np
import jax
import jax.numpy as jnp
from jax import lax
from jax.experimental import pallas as pl
from jax.experimental.pallas import tpu as pltpu

F32 = jnp.float32
BF16 = jnp.bfloat16

EPS = 1e-6
GRID_W = 64
NEG = -1e30

POOL_WINDOWS = (2, 4, 8, 16)
POOL_GROUP = 64
POOL_WIDTH = POOL_GROUP * len(POOL_WINDOWS)

RET_HEADS = 4
RET_DK = 64
RET_DV = 128
RET_QK = RET_HEADS * RET_DK
RET_V = RET_HEADS * RET_DV
ROPE_BASE = 10000.0

NA_HEADS = 4
NA_DH = 64
NA_WIDTH = NA_HEADS * NA_DH
NA_ROWS_MAX = 8
NA_COLS = 16

N_BRANCH = 3

PEER_KEYS = 128
PEER_HEADS = 8
PEER_DK = 128
PEER_TOPK = 16

V7X_VMEM_LIMIT_BYTES = 56 * 1024 * 1024

INPROJ_TM = 256
RET_CHUNK = 256
NA_ROWS_PER_STEP = 8
MERGE_TM = 256
TOPK_TM = 512
TOPK_SUB = 128
PEER_TM = 512
PEER_TE = 1024
PEER_LANES = 128


def _params(*sem):
    return pltpu.CompilerParams(dimension_semantics=sem, vmem_limit_bytes=V7X_VMEM_LIMIT_BYTES)


def _const_spec(shape):
    nd = len(shape)
    return pl.BlockSpec(shape, lambda *_: (0,) * nd)


def _inproj_kernel(x_ref, g_ref, w_ref, pool_ref, rq_ref, rk_ref, rv_ref, rg_ref,
                   nq_ref, nk_ref, nv_ref, gate_ref):
    x = x_ref[...]
    xn = x * lax.rsqrt(jnp.mean(x * x, axis=-1, keepdims=True) + EPS) * g_ref[...]
    xb = xn.astype(BF16)
    off = 0
    for ref, scale in ((pool_ref, None), (rq_ref, None), (rk_ref, None), (rv_ref, None), (rg_ref, None),
                       (nq_ref, NA_DH ** -0.5), (nk_ref, None), (nv_ref, None)):
        width = ref.shape[-1]
        y = jnp.dot(xb, w_ref[:, off:off + width], preferred_element_type=F32)
        if scale is not None:
            y = y * scale
        ref[...] = y.astype(ref.dtype)
        off += width
    d = x.shape[-1]
    for b in range(N_BRANCH):
        gate_ref[:, b * d:(b + 1) * d] = jnp.dot(
            xb, w_ref[:, off + b * d:off + (b + 1) * d], preferred_element_type=F32)


def _inproj(x2, g, w_in_bf):
    t, d = x2.shape
    tm = INPROJ_TM
    widths = (POOL_WIDTH, RET_QK, RET_QK, RET_V, RET_V, NA_WIDTH, NA_WIDTH, NA_WIDTH, N_BRANCH * d)
    dtypes = (F32, F32, F32, BF16, F32, BF16, BF16, BF16, F32)
    tok = lambda w: pl.BlockSpec((tm, w), lambda i: (i, 0))
    return pl.pallas_call(
        _inproj_kernel,
        grid=(t // tm,),
        in_specs=[tok(d), _const_spec((1, d)), _const_spec(w_in_bf.shape)],
        out_specs=[tok(w) for w in widths],
        out_shape=[jax.ShapeDtypeStruct((t, w), dt) for w, dt in zip(widths, dtypes)],
        compiler_params=_params("parallel"),
        name="inproj",
    )(x2, g.reshape(1, d), w_in_bf)


def _pool_kernel(p_ref, w_ref, scale_ref, o_ref, *, halves):
    x = p_ref[...]
    s = x.shape[0]
    t = lax.broadcasted_iota(jnp.int32, x.shape, 0)
    lane = lax.broadcasted_iota(jnp.int32, x.shape, 1)
    half = pl.program_id(1)

    def down(v, k):
        return jnp.where(t >= k, pltpu.roll(v, k, 0), 0.0)

    def up(v, k):
        return jnp.where(t < s - k, pltpu.roll(v, s - k, 0), 0.0)

    trail = {1: x}
    lead = {1: x}
    for m in (2, 4, 8):
        trail[m] = trail[m // 2] + down(trail[m // 2], m // 2)
        lead[m] = lead[m // 2] + up(lead[m // 2], m // 2)

    def window(hw):
        return down(trail[hw], 1) + lead[hw]

    def pick(lo_first, hi_first, lo_second, hi_second):
        first = jnp.where(lane < POOL_GROUP, lo_first, hi_first)
        second = jnp.where(lane < POOL_GROUP, lo_second, hi_second)
        return jnp.where(half == 0, first, second)

    hw = [w // 2 for w in POOL_WINDOWS]
    wsum = pick(window(hw[0]), window(hw[1]), window(hw[2]), window(hw[3]))
    hwv = pick(hw[0], hw[1], hw[2], hw[3])
    cnt = (jnp.minimum(t + hwv, s) - jnp.maximum(t - hwv, 0)).astype(F32)
    dlt = wsum / cnt - x
    y = jnp.dot(dlt.astype(BF16), w_ref[0], preferred_element_type=F32)
    o_ref[...] = y * scale_ref[0]


def _pool(p_pool, w_halves, scale_halves, batch, seq):
    t = p_pool.shape[0]
    halves = POOL_WIDTH // 128
    return pl.pallas_call(
        functools.partial(_pool_kernel, halves=halves),
        grid=(batch, halves),
        in_specs=[pl.BlockSpec((seq, 128), lambda b, h: (b, h)),
                  pl.BlockSpec((1, 128, 128), lambda b, h: (h, 0, 0)),
                  pl.BlockSpec((1, 1, 128), lambda b, h: (h, 0, 0))],
        out_specs=pl.BlockSpec((seq, 128), lambda b, h: (b, h)),
        out_shape=jax.ShapeDtypeStruct((t, POOL_WIDTH), F32),
        compiler_params=_params("parallel", "parallel"),
        name="pool",
    )(p_pool, w_halves, scale_halves)


def _rotary(x, cos, sin_signed, first_half):
    width = x.shape[-1]
    half = RET_DK // 2
    swapped = jnp.where(first_half, pltpu.roll(x, width - half, 1), pltpu.roll(x, half, 1))
    return x * cos + swapped * sin_signed


def _retention_kernel(lg_ref, q_ref, k_ref, v_ref, cos_ref, sin_ref, *rest, reverse):
    if reverse:
        fwd_ref, g_ref, o_ref, state_ref = rest
    else:
        o_ref, state_ref = rest
    c = q_ref.shape[0]

    @pl.when(pl.program_id(1) == 0)
    def _():
        state_ref[...] = jnp.zeros_like(state_ref)

    lane = lax.broadcasted_iota(jnp.int32, (c, RET_QK), 1)
    first_half = (lane % RET_DK) < (RET_DK // 2)
    cos = cos_ref[...]
    sin = sin_ref[...]
    q = _rotary(q_ref[...], cos, sin, first_half)
    k = _rotary(k_ref[...], cos, sin, first_half) * (RET_DK ** -0.5)
    v = v_ref[...]

    row = lax.broadcasted_iota(jnp.int32, (c, c), 0)
    col = lax.broadcasted_iota(jnp.int32, (c, c), 1)
    pos = lax.broadcasted_iota(jnp.int32, (c, 1), 0).astype(F32)
    if reverse:
        diff = col - row
        mask = diff > 0
        q_pow = c - pos
        k_pow = pos
    else:
        diff = row - col
        mask = diff >= 0
        q_pow = pos + 1.0
        k_pow = (c - 1) - pos
    dist = jnp.where(mask, diff, 0).astype(F32)
    direction = 1 if reverse else 0

    for h in range(RET_HEADS):
        lg = lg_ref[direction, h]
        qh = q[:, h * RET_DK:(h + 1) * RET_DK]
        kh = k[:, h * RET_DK:(h + 1) * RET_DK]
        vh = v[:, h * RET_DV:(h + 1) * RET_DV]
        decay = jnp.where(mask, jnp.exp(lg * dist), 0.0)
        scores = lax.dot_general(qh.astype(BF16), kh.astype(BF16), (((1,), (1,)), ((), ())),
                                 preferred_element_type=F32) * decay
        intra = jnp.dot(scores.astype(BF16), vh, preferred_element_type=F32)
        q_dec = qh * jnp.exp(lg * q_pow)
        k_dec = kh * jnp.exp(lg * k_pow)
        state = state_ref[h]
        cross = jnp.dot(q_dec.astype(BF16), state.astype(BF16), preferred_element_type=F32)
        chunk_kv = lax.dot_general(k_dec.astype(BF16), vh, (((0,), (0,)), ((), ())),
                                   preferred_element_type=F32)
        chunk_decay = jnp.exp(lg * jnp.full((1, RET_DV), float(c), F32))
        state_ref[h] = state * chunk_decay + chunk_kv
        out = intra + cross
        sl = slice(h * RET_DV, (h + 1) * RET_DV)
        if reverse:
            y = fwd_ref[:, sl] + out
            y = y * lax.rsqrt(jnp.mean(y * y, axis=-1, keepdims=True) + EPS)
            g = g_ref[:, sl]
            o_ref[:, sl] = (g / (1.0 + jnp.exp(-g))) * y
        else:
            o_ref[:, sl] = out


def _retention(lg, r_q, r_k, r_v, r_g, cos_t, sin_t, batch, seq):
    t = r_q.shape[0]
    c = RET_CHUNK
    n = seq // c
    scratch = [pltpu.VMEM((RET_HEADS, RET_DK, RET_DV), F32)]
    smem = pl.BlockSpec(memory_space=pltpu.SMEM)

    def specs(chunk_of):
        tok = lambda w: pl.BlockSpec((c, w), lambda b, i: (b * n + chunk_of(i), 0))
        tab = pl.BlockSpec((c, RET_QK), lambda b, i: (chunk_of(i), 0))
        return tok, tab

    tok, tab = specs(lambda i: i)
    fwd = pl.pallas_call(
        functools.partial(_retention_kernel, reverse=False),
        grid=(batch, n),
        in_specs=[smem, tok(RET_QK), tok(RET_QK), tok(RET_V), tab, tab],
        out_specs=tok(RET_V),
        out_shape=jax.ShapeDtypeStruct((t, RET_V), F32),
        scratch_shapes=scratch,
        compiler_params=_params("parallel", "arbitrary"),
        name="retention_fwd",
    )(lg, r_q, r_k, r_v, cos_t, sin_t)
    tok, tab = specs(lambda i: n - 1 - i)
    return pl.pallas_call(
        functools.partial(_retention_kernel, reverse=True),
        grid=(batch, n),
        in_specs=[smem, tok(RET_QK), tok(RET_QK), tok(RET_V), tab, tab, tok(RET_V), tok(RET_V)],
        out_specs=tok(RET_V),
        out_shape=jax.ShapeDtypeStruct((t, RET_V), F32),
        scratch_shapes=scratch,
        compiler_params=_params("parallel", "arbitrary"),
        name="retention_bwd",
    )(lg, r_q, r_k, r_v, cos_t, sin_t, fwd, r_g)


def _na_kernel(q_ref, k_ref, v_ref, bias_ref, o_ref, *, rows, kr):
    step = pl.program_id(1)
    rps = q_ref.shape[0] // GRID_W
    nkeys = kr * GRID_W
    lo = kr // 2
    hi_rows = rows - kr - lo

    def body(i, carry):
        r = step * rps + i
        r0 = jnp.clip(r - lo, 0, rows - kr)
        ver = jnp.where(r < lo, r, jnp.where(r > rows - kr + lo, r - (rows - kr), lo))
        kk = k_ref[pl.ds(pl.multiple_of(r0 * GRID_W, GRID_W), nkeys), :]
        vv = v_ref[pl.ds(pl.multiple_of(r0 * GRID_W, GRID_W), nkeys), :]
        qq = q_ref[pl.ds(pl.multiple_of(i * GRID_W, GRID_W), GRID_W), :]
        outs = []
        for h in range(NA_HEADS):
            sl = slice(h * NA_DH, (h + 1) * NA_DH)
            s = lax.dot_general(qq[:, sl], kk[:, sl], (((1,), (1,)), ((), ())),
                                preferred_element_type=F32) + bias_ref[ver, h]
            m = jnp.max(s, axis=-1, keepdims=True)
            p = jnp.exp(s - m)
            p = p / jnp.sum(p, axis=-1, keepdims=True)
            outs.append(jnp.dot(p.astype(BF16), vv[:, sl], preferred_element_type=F32))
        o_ref[pl.ds(pl.multiple_of(i * GRID_W, GRID_W), GRID_W), :] = jnp.concatenate(outs, axis=-1)
        return carry

    del hi_rows
    lax.fori_loop(0, rps, body, 0)


def _na_bias_table(rpb, rows, kr):
    lo = kr // 2
    reps = list(range(lo)) + [lo] + list(range(rows - kr + lo + 1, rows))
    c = np.arange(GRID_W)
    col_start = np.clip(c - NA_COLS // 2, 0, GRID_W - NA_COLS)
    col_in = (c[None, :] >= col_start[:, None]) & (c[None, :] < col_start[:, None] + NA_COLS)
    dj = np.clip(c[None, :] - c[:, None], -(NA_COLS - 1), NA_COLS - 1) + (NA_COLS - 1)
    pick = ((dj[None] == np.arange(2 * NA_COLS - 1)[:, None, None]) & col_in[None]).astype(np.float32)
    di0 = [int(np.clip(r - lo, 0, rows - kr)) - r + (NA_ROWS_MAX - 1) for r in reps]
    rpb_rows = jnp.stack([rpb[:, d0:d0 + kr, :] for d0 in di0]).astype(F32)
    bias = jnp.einsum('vhij,jck->vhcik', rpb_rows, jnp.asarray(pick), precision=lax.Precision.HIGHEST)
    bias = bias + jnp.asarray(np.where(col_in, 0.0, NEG).astype(np.float32))[None, None, :, None, :]
    return bias.reshape(len(reps), NA_HEADS, GRID_W, kr * GRID_W)


def _na(n_q, n_k, n_v, bias, batch, seq):
    t = n_q.shape[0]
    rows = seq // GRID_W
    kr = min(NA_ROWS_MAX, rows)
    rps = min(NA_ROWS_PER_STEP, rows)
    blk = rps * GRID_W
    steps = rows // rps
    return pl.pallas_call(
        functools.partial(_na_kernel, rows=rows, kr=kr),
        grid=(batch, steps),
        in_specs=[pl.BlockSpec((blk, NA_WIDTH), lambda b, i: (b * steps + i, 0)),
                  pl.BlockSpec((seq, NA_WIDTH), lambda b, i: (b, 0)),
                  pl.BlockSpec((seq, NA_WIDTH), lambda b, i: (b, 0)),
                  _const_spec(bias.shape)],
        out_specs=pl.BlockSpec((blk, NA_WIDTH), lambda b, i: (b * steps + i, 0)),
        out_shape=jax.ShapeDtypeStruct((t, NA_WIDTH), F32),
        compiler_params=_params("parallel", "arbitrary"),
        name="natten",
    )(n_q, n_k, n_v, bias)


def _merge_kernel(x_ref, yp_ref, yr_ref, yn_ref, gate_ref, wp_ref, wr_ref, wn_ref, wo_ref,
                  gf_ref, wq_ref, keys_ref, h_ref, hnt_ref, st_ref):
    d = x_ref.shape[-1]
    merged = None
    for b, (y_ref, w_ref) in enumerate(((yp_ref, wp_ref), (yr_ref, wr_ref), (yn_ref, wn_ref))):
        gate = 1.0 / (1.0 + jnp.exp(-gate_ref[:, b * d:(b + 1) * d]))
        term = gate * jnp.dot(y_ref[...].astype(BF16), w_ref[...], preferred_element_type=F32)
        merged = term if merged is None else merged + term
    h = x_ref[...] + jnp.dot(merged.astype(BF16), wo_ref[...], preferred_element_type=F32)
    h_ref[...] = h
    hn = h * lax.rsqrt(jnp.mean(h * h, axis=-1, keepdims=True) + EPS) * gf_ref[...]
    hnb = hn.astype(BF16)
    hnt_ref[...] = hn.T.astype(BF16)
    q = jnp.dot(hnb, wq_ref[...], preferred_element_type=F32).astype(BF16)
    for hh in range(PEER_HEADS):
        for p in range(2):
            col = (hh * 2 + p) * PEER_DK
            st_ref[hh, p] = lax.dot_general(keys_ref[p], q[:, col:col + PEER_DK],
                                            (((1,), (1,)), ((), ())), preferred_element_type=F32)


def _merge(x2, y_pool, y_ret, y_na, gates, wp, wr, wn, wo, g_ffn, wq, keys):
    t, d = x2.shape
    tm = MERGE_TM
    tok = lambda w: pl.BlockSpec((tm, w), lambda i: (i, 0))
    return pl.pallas_call(
        _merge_kernel,
        grid=(t // tm,),
        in_specs=[tok(d), tok(POOL_WIDTH), tok(RET_V), tok(NA_WIDTH), tok(N_BRANCH * d),
                  _const_spec(wp.shape), _const_spec(wr.shape), _const_spec(wn.shape), _const_spec(wo.shape),
                  _const_spec((1, d)), _const_spec(wq.shape), _const_spec(keys.shape)],
        out_specs=[tok(d),
                   pl.BlockSpec((d, tm), lambda i: (0, i)),
                   pl.BlockSpec((PEER_HEADS, 2, PEER_KEYS, tm), lambda i: (0, 0, 0, i))],
        out_shape=[jax.ShapeDtypeStruct((t, d), F32),
                   jax.ShapeDtypeStruct((d, t), BF16),
                   jax.ShapeDtypeStruct((PEER_HEADS, 2, PEER_KEYS, t), F32)],
        compiler_params=_params("parallel"),
        name="merge",
    )(x2, y_pool, y_ret, y_na, gates, wp, wr, wn, wo, g_ffn.reshape(1, d), wq, keys)


def _topk_kernel(s_ref, r2_ref, e2_ref, cc_ref, e1_ref):
    nk = s_ref.shape[2]
    k_top = PEER_TOPK
    w = TOPK_SUB
    key = lax.broadcasted_iota(jnp.int32, (nk, w), 0)
    arow = lax.broadcasted_iota(jnp.int32, (k_top, w), 0)
    neg_inf = -jnp.inf

    def top_ranks(s):
        def body(a, carry):
            s, rank, vals = carry
            m = jnp.max(s, axis=0, keepdims=True)
            idx = jnp.min(jnp.where(s == m, key, nk), axis=0, keepdims=True)
            hit = key == idx
            return (jnp.where(hit, neg_inf, s), jnp.where(hit, a, rank), jnp.where(arow == a, m, vals))
        init = (s, jnp.full((nk, w), k_top, jnp.int32), jnp.zeros((k_top, w), F32))
        _, rank, vals = lax.fori_loop(0, k_top, body, init)
        return rank, vals

    for j in range(s_ref.shape[-1] // w):
        ds = slice(j * w, (j + 1) * w)
        s1 = s_ref[0, 0, :, ds]
        s2 = s_ref[0, 1, :, ds]
        r1, v1 = top_ranks(s1)
        r2, v2 = top_ranks(s2)
        top = v1[0:1] + v2[0:1]

        def merge(_, carry):
            cnt, front, z = carry
            m = jnp.max(front, axis=0, keepdims=True)
            a_star = jnp.min(jnp.where(front == m, arow, k_top), axis=0, keepdims=True)
            hit = arow == a_star
            cnt = jnp.where(hit, cnt + 1, cnt)
            nxt = jnp.sum(jnp.where(hit, cnt, 0), axis=0, keepdims=True)
            v2_next = jnp.sum(jnp.where(arow == nxt, v2, 0.0), axis=0, keepdims=True)
            cand = jnp.where(nxt < k_top, v1 + v2_next, neg_inf)
            return cnt, jnp.where(hit, cand, front), z + jnp.exp(m - top)
        init = (jnp.zeros((k_top, w), jnp.int32), v1 + v2[0:1], jnp.zeros((1, w), F32))
        cnt, _, z = lax.fori_loop(0, k_top, merge, init)

        cc = jnp.zeros((nk, w), jnp.int32)
        for a in range(k_top):
            cc = jnp.where(r1 == a, cnt[a:a + 1], cc)
        r2_ref[0, :, ds] = r2.astype(F32).astype(BF16)
        cc_ref[0, :, ds] = cc.astype(F32)
        e2_ref[0, :, ds] = jnp.exp(s2 - v2[0:1]).astype(BF16)
        e1_ref[0, :, ds] = jnp.exp(s1 - v1[0:1]) / z


def _topk(st):
    heads, _, nk, t = st.shape
    tm = TOPK_TM
    out = pl.BlockSpec((1, nk, tm), lambda i, h: (h, 0, i))
    return pl.pallas_call(
        _topk_kernel,
        grid=(t // tm, heads),
        in_specs=[pl.BlockSpec((1, 2, nk, tm), lambda i, h: (h, 0, 0, i))],
        out_specs=[out] * 4,
        out_shape=[jax.ShapeDtypeStruct((heads, nk, t), dt) for dt in (BF16, BF16, F32, F32)],
        compiler_params=_params("parallel", "parallel"),
        name="peer_topk",
    )(st)


def _peer_kernel(hnt_ref, u_ref, vt_ref, r2_in_ref, e2_in_ref, cc_ref, e1_ref, h_ref, o_ref,
                 acc_ref, a_ref, ht_ref, r2_ref, e2_ref):
    e = pl.program_id(1)
    te = u_ref.shape[0]
    tm = hnt_ref.shape[1]
    nk = r2_ref.shape[1]
    rows_per_step = te // nk

    @pl.when(e == 0)
    def _():
        acc_ref[...] = jnp.zeros_like(acc_ref)
        r2_ref[...] = r2_in_ref[...]
        e2_ref[...] = e2_in_ref[...]

    a_ref[...] = jnp.dot(u_ref[...], hnt_ref[...], preferred_element_type=F32)
    for ii in range(rows_per_step):
        i = e * rows_per_step + ii
        rows = slice(ii * nk, (ii + 1) * nk)
        cc_rows = [cc_ref[hh, pl.ds(i, 1), :] for hh in range(PEER_HEADS)]
        e1_rows = [e1_ref[hh, pl.ds(i, 1), :] for hh in range(PEER_HEADS)]
        for lt in range(tm // PEER_LANES):
            lanes = slice(lt * PEER_LANES, (lt + 1) * PEER_LANES)
            gate = None
            for hh in range(PEER_HEADS):
                cc = jnp.broadcast_to(cc_rows[hh][:, lanes], (nk, PEER_LANES)).astype(BF16)
                e1 = jnp.broadcast_to(e1_rows[hh][:, lanes], (nk, PEER_LANES)).astype(BF16)
                term = jnp.where(r2_ref[hh, :, lanes] < cc, e2_ref[hh, :, lanes], jnp.zeros((), BF16)) * e1
                gate = term if gate is None else gate + term
            a = a_ref[rows, lanes]
            act = 0.5 * a * (1.0 + lax.erf(a * np.float32(1.0 / np.sqrt(2.0))))
            ht_ref[rows, lanes] = act.astype(BF16) * gate
    acc_ref[...] += jnp.dot(vt_ref[...], ht_ref[...], preferred_element_type=F32)

    @pl.when(e == pl.num_programs(1) - 1)
    def _():
        o_ref[...] = h_ref[...] + acc_ref[...].T


def _peer(hnt, u_bf, vt_bf, r2, e2, cc, e1, h2):
    d, t = hnt.shape
    n_exp = u_bf.shape[0]
    heads, nk, _ = r2.shape
    tm, te = PEER_TM, PEER_TE
    tab = pl.BlockSpec((heads, nk, tm), lambda i, e: (0, 0, i))
    return pl.pallas_call(
        _peer_kernel,
        grid=(t // tm, n_exp // te),
        in_specs=[pl.BlockSpec((d, tm), lambda i, e: (0, i)),
                  pl.BlockSpec((te, d), lambda i, e: (e, 0)),
                  pl.BlockSpec((d, te), lambda i, e: (0, e)),
                  tab, tab, tab, tab,
                  pl.BlockSpec((tm, d), lambda i, e: (i, 0))],
        out_specs=pl.BlockSpec((tm, d), lambda i, e: (i, 0)),
        out_shape=jax.ShapeDtypeStruct((t, d), F32),
        scratch_shapes=[pltpu.VMEM((d, tm), F32), pltpu.VMEM((te, tm), F32), pltpu.VMEM((te, tm), BF16),
                        pltpu.VMEM((heads, nk, tm), BF16), pltpu.VMEM((heads, nk, tm), BF16)],
        compiler_params=_params("parallel", "arbitrary"),
        name="peer_experts",
    )(hnt, u_bf, vt_bf, r2, e2, cc, e1, h2)


def _rmsnorm_kernel(x_ref, g_ref, o_ref):
    x = x_ref[...]
    o_ref[...] = x * lax.rsqrt(jnp.mean(x * x, axis=-1, keepdims=True) + EPS) * g_ref[...]


def _rmsnorm(x2, g):
    t, d = x2.shape
    tm = 512
    return pl.pallas_call(
        _rmsnorm_kernel,
        grid=(t // tm,),
        in_specs=[pl.BlockSpec((tm, d), lambda i: (i, 0)), _const_spec((1, d))],
        out_specs=pl.BlockSpec((tm, d), lambda i: (i, 0)),
        out_shape=jax.ShapeDtypeStruct((t, d), F32),
        compiler_params=_params("parallel"),
        name="final_norm",
    )(x2, g.reshape(1, d))


def _rotary_tables(seq):
    half = RET_DK // 2
    inv = 1.0 / (ROPE_BASE ** jnp.linspace(0.0, 1.0, half, dtype=F32))
    ang = jnp.arange(seq, dtype=F32)[:, None] * inv[None, :]
    cos, sin = jnp.cos(ang), jnp.sin(ang)
    cos_t = jnp.tile(jnp.concatenate([cos, cos], axis=-1), (1, RET_HEADS))
    sin_t = jnp.tile(jnp.concatenate([-sin, sin], axis=-1), (1, RET_HEADS))
    return cos_t, sin_t


def _pool_weights(pool_w, pool_scale):
    halves = POOL_WIDTH // 128
    per = 128 // POOL_GROUP
    blocks = []
    for hf in range(halves):
        blk = jnp.zeros((128, 128), F32)
        for g in range(per):
            sl = slice(g * POOL_GROUP, (g + 1) * POOL_GROUP)
            blk = blk.at[sl, sl].set(pool_w[hf * per + g])
        blocks.append(blk)
    return jnp.stack(blocks).astype(BF16), pool_scale.reshape(halves, 1, 128)


def kernel(x, norm_mix, w_in, pool_w, pool_scale, ret_decay, na_rpb, w_br_pool, w_br_ret, w_br_na,
           w_out, norm_ffn, peer_w_query, peer_sub_keys, peer_u, peer_v, norm_final):
    batch, seq, d = x.shape
    depth = w_in.shape[0]
    rows = seq // GRID_W
    kr = min(NA_ROWS_MAX, rows)
    cos_t, sin_t = _rotary_tables(seq)
    x2 = x.reshape(batch * seq, d)
    for l in range(depth):
        pool, r_q, r_k, r_v, r_g, n_q, n_k, n_v, gates = _inproj(x2, norm_mix[l], w_in[l].astype(BF16))
        w_halves, scale_halves = _pool_weights(pool_w[l], pool_scale[l])
        y_pool = _pool(pool, w_halves, scale_halves, batch, seq)
        lg = jax.nn.log_sigmoid(ret_decay[l].astype(F32))
        y_ret = _retention(lg, r_q, r_k, r_v, r_g, cos_t, sin_t, batch, seq)
        y_na = _na(n_q, n_k, n_v, _na_bias_table(na_rpb[l], rows, kr), batch, seq)
        h2, hnt, st = _merge(x2, y_pool, y_ret, y_na, gates,
                             w_br_pool[l].astype(BF16), w_br_ret[l].astype(BF16), w_br_na[l].astype(BF16),
                             w_out[l].astype(BF16), norm_ffn[l], peer_w_query[l].astype(BF16),
                             peer_sub_keys[l].astype(BF16))
        r2, e2, cc, e1 = _topk(st)
        x2 = _peer(hnt, peer_u[l].astype(BF16), peer_v[l].T.astype(BF16), r2, e2, cc, e1, h2)
    return _rmsnorm(x2, norm_final).reshape(batch, seq, d)
```

```python
import functools

import numpy as np
import jax
import jax.numpy as jnp
from jax import lax
from jax.experimental import pallas as pl
from jax.experimental.pallas import tpu as pltpu

F32 = jnp.float32
BF16 = jnp.bfloat16

EPS = 1e-6
GRID_W = 64
NEG = -1e30

POOL_WINDOWS = (2, 4, 8, 16)
POOL_GROUP = 64
POOL_WIDTH = POOL_GROUP * len(POOL_WINDOWS)

RET_HEADS = 4
RET_DK = 64
RET_DV = 128
RET_QK = RET_HEADS * RET_DK
RET_V = RET_HEADS * RET_DV
ROPE_BASE = 10000.0

NA_HEADS = 4
NA_DH = 64
NA_WIDTH = NA_HEADS * NA_DH
NA_ROWS_MAX = 8
NA_COLS = 16

N_BRANCH = 3

PEER_KEYS = 128
PEER_HEADS = 8
PEER_DK = 128
PEER_TOPK = 16

V7X_VMEM_LIMIT_BYTES = 56 * 1024 * 1024

INPROJ_TM = 256
RET_CHUNK = 256
NA_ROWS_PER_STEP = 8
MERGE_TM = 256
TOPK_TM = 1024
TOPK_SUB = 128
PEER_TM = 512
PEER_TE = 1024
PEER_LANES = 128


def _params(*sem):
    return pltpu.CompilerParams(dimension_semantics=sem, vmem_limit_bytes=V7X_VMEM_LIMIT_BYTES)


def _const_spec(shape):
    nd = len(shape)
    return pl.BlockSpec(shape, lambda *_: (0,) * nd)


def _inproj_kernel(x_ref, g_ref, w_ref, pool_ref, rq_ref, rk_ref, rv_ref, rg_ref,
                   nq_ref, nk_ref, nv_ref, gate_ref):
    x = x_ref[...]
    xn = x * lax.rsqrt(jnp.mean(x * x, axis=-1, keepdims=True) + EPS) * g_ref[...]
    xb = xn.astype(BF16)
    off = 0
    for ref, scale in ((pool_ref, None), (rq_ref, None), (rk_ref, None), (rv_ref, None), (rg_ref, None),
                       (nq_ref, NA_DH ** -0.5), (nk_ref, None), (nv_ref, None)):
        width = ref.shape[-1]
        y = jnp.dot(xb, w_ref[:, off:off + width], preferred_element_type=F32)
        if scale is not None:
            y = y * scale
        ref[...] = y.astype(ref.dtype)
        off += width
    d = x.shape[-1]
    for b in range(N_BRANCH):
        gate_ref[:, b * d:(b + 1) * d] = jnp.dot(
            xb, w_ref[:, off + b * d:off + (b + 1) * d], preferred_element_type=F32)


def _inproj(x2, g, w_in_bf):
    t, d = x2.shape
    tm = INPROJ_TM
    widths = (POOL_WIDTH, RET_QK, RET_QK, RET_V, RET_V, NA_WIDTH, NA_WIDTH, NA_WIDTH, N_BRANCH * d)
    dtypes = (F32, F32, F32, BF16, F32, BF16, BF16, BF16, F32)
    tok = lambda w: pl.BlockSpec((tm, w), lambda i: (i, 0))
    return pl.pallas_call(
        _inproj_kernel,
        grid=(t // tm,),
        in_specs=[tok(d), _const_spec((1, d)), _const_spec(w_in_bf.shape)],
        out_specs=[tok(w) for w in widths],
        out_shape=[jax.ShapeDtypeStruct((t, w), dt) for w, dt in zip(widths, dtypes)],
        compiler_params=_params("parallel"),
        name="inproj",
    )(x2, g.reshape(1, d), w_in_bf)


def _pool_kernel(p_ref, w_ref, scale_ref, o_ref, *, halves):
    x = p_ref[...]
    s = x.shape[0]
    t = lax.broadcasted_iota(jnp.int32, x.shape, 0)
    lane = lax.broadcasted_iota(jnp.int32, x.shape, 1)
    half = pl.program_id(1)

    def down(v, k):
        return jnp.where(t >= k, pltpu.roll(v, k, 0), 0.0)

    def up(v, k):
        return jnp.where(t < s - k, pltpu.roll(v, s - k, 0), 0.0)

    trail = {1: x}
    lead = {1: x}
    for m in (2, 4, 8):
        trail[m] = trail[m // 2] + down(trail[m // 2], m // 2)
        lead[m] = lead[m // 2] + up(lead[m // 2], m // 2)

    def window(hw):
        return down(trail[hw], 1) + lead[hw]

    def pick(lo_first, hi_first, lo_second, hi_second):
        first = jnp.where(lane < POOL_GROUP, lo_first, hi_first)
        second = jnp.where(lane < POOL_GROUP, lo_second, hi_second)
        return jnp.where(half == 0, first, second)

    hw = [w // 2 for w in POOL_WINDOWS]
    wsum = pick(window(hw[0]), window(hw[1]), window(hw[2]), window(hw[3]))
    hwv = pick(hw[0], hw[1], hw[2], hw[3])
    cnt = (jnp.minimum(t + hwv, s) - jnp.maximum(t - hwv, 0)).astype(F32)
    dlt = wsum / cnt - x
    y = jnp.dot(dlt.astype(BF16), w_ref[0], preferred_element_type=F32)
    o_ref[...] = y * scale_ref[0]


def _pool(p_pool, w_halves, scale_halves, batch, seq):
    t = p_pool.shape[0]
    halves = POOL_WIDTH // 128
    return pl.pallas_call(
        functools.partial(_pool_kernel, halves=halves),
        grid=(batch, halves),
        in_specs=[pl.BlockSpec((seq, 128), lambda b, h: (b, h)),
                  pl.BlockSpec((1, 128, 128), lambda b, h: (h, 0, 0)),
                  pl.BlockSpec((1, 1, 128), lambda b, h: (h, 0, 0))],
        out_specs=pl.BlockSpec((seq, 128), lambda b, h: (b, h)),
        out_shape=jax.ShapeDtypeStruct((t, POOL_WIDTH), F32),
        compiler_params=_params("parallel", "parallel"),
        name="pool",
    )(p_pool, w_halves, scale_halves)


def _rotary(x, cos, sin_signed, first_half):
    width = x.shape[-1]
    half = RET_DK // 2
    swapped = jnp.where(first_half, pltpu.roll(x, width - half, 1), pltpu.roll(x, half, 1))
    return x * cos + swapped * sin_signed


def _retention_kernel(lg_ref, q_ref, k_ref, v_ref, cos_ref, sin_ref, *rest, reverse):
    if reverse:
        fwd_ref, g_ref, o_ref, state_ref = rest
    else:
        o_ref, state_ref = rest
    c = q_ref.shape[0]

    @pl.when(pl.program_id(1) == 0)
    def _():
        state_ref[...] = jnp.zeros_like(state_ref)

    lane = lax.broadcasted_iota(jnp.int32, (c, RET_QK), 1)
    first_half = (lane % RET_DK) < (RET_DK // 2)
    cos = cos_ref[...]
    sin = sin_ref[...]
    q = _rotary(q_ref[...], cos, sin, first_half)
    k = _rotary(k_ref[...], cos, sin, first_half) * (RET_DK ** -0.5)
    v = v_ref[...]

    row = lax.broadcasted_iota(jnp.int32, (c, c), 0)
    col = lax.broadcasted_iota(jnp.int32, (c, c), 1)
    pos = lax.broadcasted_iota(jnp.int32, (c, 1), 0).astype(F32)
    if reverse:
        diff = col - row
        mask = diff > 0
        q_pow = c - pos
        k_pow = pos
    else:
        diff = row - col
        mask = diff >= 0
        q_pow = pos + 1.0
        k_pow = (c - 1) - pos
    dist = jnp.where(mask, diff, 0).astype(F32)
    direction = 1 if reverse else 0

    for h in range(RET_HEADS):
        lg = lg_ref[direction, h]
        qh = q[:, h * RET_DK:(h + 1) * RET_DK]
        kh = k[:, h * RET_DK:(h + 1) * RET_DK]
        vh = v[:, h * RET_DV:(h + 1) * RET_DV]
        decay = jnp.where(mask, jnp.exp(lg * dist), 0.0)
        scores = lax.dot_general(qh.astype(BF16), kh.astype(BF16), (((1,), (1,)), ((), ())),
                                 preferred_element_type=F32) * decay
        intra = jnp.dot(scores.astype(BF16), vh, preferred_element_type=F32)
        q_dec = qh * jnp.exp(lg * q_pow)
        k_dec = kh * jnp.exp(lg * k_pow)
        state = state_ref[h]
        cross = jnp.dot(q_dec.astype(BF16), state.astype(BF16), preferred_element_type=F32)
        chunk_kv = lax.dot_general(k_dec.astype(BF16), vh, (((0,), (0,)), ((), ())),
                                   preferred_element_type=F32)
        chunk_decay = jnp.exp(lg * jnp.full((1, RET_DV), float(c), F32))
        state_ref[h] = state * chunk_decay + chunk_kv
        out = intra + cross
        sl = slice(h * RET_DV, (h + 1) * RET_DV)
        if reverse:
            y = fwd_ref[:, sl] + out
            y = y * lax.rsqrt(jnp.mean(y * y, axis=-1, keepdims=True) + EPS)
            g = g_ref[:, sl]
            o_ref[:, sl] = (g / (1.0 + jnp.exp(-g))) * y
        else:
            o_ref[:, sl] = out


def _retention(lg, r_q, r_k, r_v, r_g, cos_t, sin_t, batch, seq):
    t = r_q.shape[0]
    c = RET_CHUNK
    n = seq // c
    scratch = [pltpu.VMEM((RET_HEADS, RET_DK, RET_DV), F32)]
    smem = pl.BlockSpec(memory_space=pltpu.SMEM)

    def specs(chunk_of):
        tok = lambda w: pl.BlockSpec((c, w), lambda b, i: (b * n + chunk_of(i), 0))
        tab = pl.BlockSpec((c, RET_QK), lambda b, i: (chunk_of(i), 0))
        return tok, tab

    tok, tab = specs(lambda i: i)
    fwd = pl.pallas_call(
        functools.partial(_retention_kernel, reverse=False),
        grid=(batch, n),
        in_specs=[smem, tok(RET_QK), tok(RET_QK), tok(RET_V), tab, tab],
        out_specs=tok(RET_V),
        out_shape=jax.ShapeDtypeStruct((t, RET_V), F32),
        scratch_shapes=scratch,
        compiler_params=_params("parallel", "arbitrary"),
        name="retention_fwd",
    )(lg, r_q, r_k, r_v, cos_t, sin_t)
    tok, tab = specs(lambda i: n - 1 - i)
    return pl.pallas_call(
        functools.partial(_retention_kernel, reverse=True),
        grid=(batch, n),
        in_specs=[smem, tok(RET_QK), tok(RET_QK), tok(RET_V), tab, tab, tok(RET_V), tok(RET_V)],
        out_specs=tok(RET_V),
        out_shape=jax.ShapeDtypeStruct((t, RET_V), F32),
        scratch_shapes=scratch,
        compiler_params=_params("parallel", "arbitrary"),
        name="retention_bwd",
    )(lg, r_q, r_k, r_v, cos_t, sin_t, fwd, r_g)


def _na_kernel(q_ref, k_ref, v_ref, bias_ref, o_ref, *, rows, kr):
    step = pl.program_id(1)
    rps = q_ref.shape[0] // GRID_W
    nkeys = kr * GRID_W
    lo = kr // 2
    hi_rows = rows - kr - lo

    def body(i, carry):
        r = step * rps + i
        r0 = jnp.clip(r - lo, 0, rows - kr)
        ver = jnp.where(r < lo, r, jnp.where(r > rows - kr + lo, r - (rows - kr), lo))
        kk = k_ref[pl.ds(pl.multiple_of(r0 * GRID_W, GRID_W), nkeys), :]
        vv = v_ref[pl.ds(pl.multiple_of(r0 * GRID_W, GRID_W), nkeys), :]
        qq = q_ref[pl.ds(pl.multiple_of(i * GRID_W, GRID_W), GRID_W), :]
        outs = []
        for h in range(NA_HEADS):
            sl = slice(h * NA_DH, (h + 1) * NA_DH)
            s = lax.dot_general(qq[:, sl], kk[:, sl], (((1,), (1,)), ((), ())),
                                preferred_element_type=F32) + bias_ref[ver, h]
            m = jnp.max(s, axis=-1, keepdims=True)
            p = jnp.exp(s - m)
            p = p / jnp.sum(p, axis=-1, keepdims=True)
            outs.append(jnp.dot(p.astype(BF16), vv[:, sl], preferred_element_type=F32))
        o_ref[pl.ds(pl.multiple_of(i * GRID_W, GRID_W), GRID_W), :] = jnp.concatenate(outs, axis=-1)
        return carry

    del hi_rows
    lax.fori_loop(0, rps, body, 0)


def _na_bias_table(rpb, rows, kr):
    lo = kr // 2
    reps = list(range(lo)) + [lo] + list(range(rows - kr + lo + 1, rows))
    c = np.arange(GRID_W)
    col_start = np.clip(c - NA_COLS // 2, 0, GRID_W - NA_COLS)
    col_in = (c[None, :] >= col_start[:, None]) & (c[None, :] < col_start[:, None] + NA_COLS)
    dj = np.clip(c[None, :] - c[:, None], -(NA_COLS - 1), NA_COLS - 1) + (NA_COLS - 1)
    pick = ((dj[None] == np.arange(2 * NA_COLS - 1)[:, None, None]) & col_in[None]).astype(np.float32)
    di0 = [int(np.clip(r - lo, 0, rows - kr)) - r + (NA_ROWS_MAX - 1) for r in reps]
    rpb_rows = jnp.stack([rpb[:, d0:d0 + kr, :] for d0 in di0]).astype(F32)
    bias = jnp.einsum('vhij,jck->vhcik', rpb_rows, jnp.asarray(pick), precision=lax.Precision.HIGHEST)
    bias = bias + jnp.asarray(np.where(col_in, 0.0, NEG).astype(np.float32))[None, None, :, None, :]
    return bias.reshape(len(reps), NA_HEADS, GRID_W, kr * GRID_W)


def _na(n_q, n_k, n_v, bias, batch, seq):
    t = n_q.shape[0]
    rows = seq // GRID_W
    kr = min(NA_ROWS_MAX, rows)
    rps = min(NA_ROWS_PER_STEP, rows)
    blk = rps * GRID_W
    steps = rows // rps
    return pl.pallas_call(
        functools.partial(_na_kernel, rows=rows, kr=kr),
        grid=(batch, steps),
        in_specs=[pl.BlockSpec((blk, NA_WIDTH), lambda b, i: (b * steps + i, 0)),
                  pl.BlockSpec((seq, NA_WIDTH), lambda b, i: (b, 0)),
                  pl.BlockSpec((seq, NA_WIDTH), lambda b, i: (b, 0)),
                  _const_spec(bias.shape)],
        out_specs=pl.BlockSpec((blk, NA_WIDTH), lambda b, i: (b * steps + i, 0)),
        out_shape=jax.ShapeDtypeStruct((t, NA_WIDTH), F32),
        compiler_params=_params("parallel", "arbitrary"),
        name="natten",
    )(n_q, n_k, n_v, bias)


def _merge_kernel(x_ref, yp_ref, yr_ref, yn_ref, gate_ref, wp_ref, wr_ref, wn_ref, wo_ref,
                  gf_ref, wq_ref, keys_ref, h_ref, hnt_ref, st_ref):
    d = x_ref.shape[-1]
    merged = None
    for b, (y_ref, w_ref) in enumerate(((yp_ref, wp_ref), (yr_ref, wr_ref), (yn_ref, wn_ref))):
        gate = 1.0 / (1.0 + jnp.exp(-gate_ref[:, b * d:(b + 1) * d]))
        term = gate * jnp.dot(y_ref[...].astype(BF16), w_ref[...], preferred_element_type=F32)
        merged = term if merged is None else merged + term
    h = x_ref[...] + jnp.dot(merged.astype(BF16), wo_ref[...], preferred_element_type=F32)
    h_ref[...] = h
    hn = h * lax.rsqrt(jnp.mean(h * h, axis=-1, keepdims=True) + EPS) * gf_ref[...]
    hnb = hn.astype(BF16)
    hnt_ref[...] = hn.T.astype(BF16)
    q = jnp.dot(hnb, wq_ref[...], preferred_element_type=F32).astype(BF16)
    for hh in range(PEER_HEADS):
        for p in range(2):
            col = (hh * 2 + p) * PEER_DK
            st_ref[hh, p] = lax.dot_general(keys_ref[p], q[:, col:col + PEER_DK],
                                            (((1,), (1,)), ((), ())), preferred_element_type=F32)


def _merge(x2, y_pool, y_ret, y_na, gates, wp, wr, wn, wo, g_ffn, wq, keys):
    t, d = x2.shape
    tm = MERGE_TM
    tok = lambda w: pl.BlockSpec((tm, w), lambda i: (i, 0))
    return pl.pallas_call(
        _merge_kernel,
        grid=(t // tm,),
        in_specs=[tok(d), tok(POOL_WIDTH), tok(RET_V), tok(NA_WIDTH), tok(N_BRANCH * d),
                  _const_spec(wp.shape), _const_spec(wr.shape), _const_spec(wn.shape), _const_spec(wo.shape),
                  _const_spec((1, d)), _const_spec(wq.shape), _const_spec(keys.shape)],
        out_specs=[tok(d),
                   pl.BlockSpec((d, tm), lambda i: (0, i)),
                   pl.BlockSpec((PEER_HEADS, 2, PEER_KEYS, tm), lambda i: (0, 0, 0, i))],
        out_shape=[jax.ShapeDtypeStruct((t, d), F32),
                   jax.ShapeDtypeStruct((d, t), BF16),
                   jax.ShapeDtypeStruct((PEER_HEADS, 2, PEER_KEYS, t), F32)],
        compiler_params=_params("parallel"),
        name="merge",
    )(x2, y_pool, y_ret, y_na, gates, wp, wr, wn, wo, g_ffn.reshape(1, d), wq, keys)


def _topk_exact_tile(s_ref, r2_ref, e2_ref, cc_ref, e1_ref, ds):
    nk = s_ref.shape[2]
    k_top = PEER_TOPK
    w = ds.stop - ds.start
    key = lax.broadcasted_iota(jnp.int32, (nk, w), 0)
    arow = lax.broadcasted_iota(jnp.int32, (k_top, w), 0)
    neg_inf = -jnp.inf

    def top_ranks(s):
        def body(a, carry):
            s, rank, vals = carry
            m = jnp.max(s, axis=0, keepdims=True)
            idx = jnp.min(jnp.where(s == m, key, nk), axis=0, keepdims=True)
            hit = key == idx
            return (jnp.where(hit, neg_inf, s), jnp.where(hit, a, rank), jnp.where(arow == a, m, vals))
        init = (s, jnp.full((nk, w), k_top, jnp.int32), jnp.zeros((k_top, w), F32))
        _, rank, vals = lax.fori_loop(0, k_top, body, init)
        return rank, vals

    s1 = s_ref[0, 0, :, ds]
    s2 = s_ref[0, 1, :, ds]
    r1, v1 = top_ranks(s1)
    r2, v2 = top_ranks(s2)
    top = v1[0:1] + v2[0:1]

    def merge(_, carry):
        cnt, front, z = carry
        m = jnp.max(front, axis=0, keepdims=True)
        a_star = jnp.min(jnp.where(front == m, arow, k_top), axis=0, keepdims=True)
        hit = arow == a_star
        cnt = jnp.where(hit, cnt + 1, cnt)
        nxt = jnp.sum(jnp.where(hit, cnt, 0), axis=0, keepdims=True)
        v2_next = jnp.sum(jnp.where(arow == nxt, v2, 0.0), axis=0, keepdims=True)
        cand = jnp.where(nxt < k_top, v1 + v2_next, neg_inf)
        return cnt, jnp.where(hit, cand, front), z + jnp.exp(m - top)
    init = (jnp.zeros((k_top, w), jnp.int32), v1 + v2[0:1], jnp.zeros((1, w), F32))
    cnt, _, z = lax.fori_loop(0, k_top, merge, init)

    cc = jnp.zeros((nk, w), jnp.int32)
    for a in range(k_top):
        cc = jnp.where(r1 == a, cnt[a:a + 1], cc)
    r2_ref[0, :, ds] = r2.astype(F32).astype(BF16)
    cc_ref[0, :, ds] = cc.astype(F32)
    e2_ref[0, :, ds] = jnp.exp(s2 - v2[0:1]).astype(BF16)
    e1_ref[0, :, ds] = jnp.exp(s1 - v1[0:1]) * (1.0 / z)


def _topk_kernel(s_ref, r2_ref, e2_ref, cc_ref, e1_ref, v1_ref, v2_ref):
    nk = s_ref.shape[2]
    tm = s_ref.shape[3]
    k_top = PEER_TOPK
    w = TOPK_SUB
    subs = [slice(j * w, (j + 1) * w) for j in range(tm // w)]
    neg_inf = -jnp.inf
    no_tie_rank_sum = float(sum(range(k_top)) + k_top * (nk - k_top))
    mark_base, mark_step = 2.0 ** 100, 2.0 ** 96
    lowest_score = -(2.0 ** 99)

    redo = jnp.zeros((1, w), F32)
    for j, ds in enumerate(subs):
        for p, (rank_ref, vals_ref) in enumerate(((cc_ref, v1_ref), (r2_ref, v2_ref))):
            s = s_ref[0, p, :, ds]
            floor_ok = jnp.min(s, axis=0, keepdims=True) > lowest_score
            for a in range(k_top):
                m = jnp.max(s, axis=0, keepdims=True)
                s = jnp.where(s == m, -(mark_base + a * mark_step), s)
                vals_ref[a, j:j + 1, :] = m
            rank = jnp.where(s < lowest_score, s * (-1.0 / mark_step) - mark_base / mark_step, float(k_top))
            exact = floor_ok & (jnp.sum(rank, axis=0, keepdims=True) == no_tie_rank_sum)
            redo = redo + jnp.where(exact, 0.0, 1.0)
            rank_ref[0, :, ds] = rank.astype(rank_ref.dtype)

    v1 = v1_ref[...]
    v2 = v2_ref[...]
    arow = lax.broadcasted_iota(jnp.int32, v1.shape, 0).astype(F32)
    top = v1[0:1] + v2[0:1]
    cnt = jnp.zeros(v1.shape, F32)
    front = v1 + v2[0:1]
    z = jnp.zeros(top.shape, F32)
    for _ in range(k_top):
        m = jnp.max(front, axis=0, keepdims=True)
        a_star = jnp.min(jnp.where(front == m, arow, float(k_top)), axis=0, keepdims=True)
        hit = arow == a_star
        cnt = jnp.where(hit, cnt + 1.0, cnt)
        nxt = jnp.sum(jnp.where(hit, cnt, 0.0), axis=0, keepdims=True)
        v2_next = jnp.sum(jnp.where(arow == nxt, v2, 0.0), axis=0, keepdims=True)
        cand = jnp.where(nxt < float(k_top), v1 + v2_next, neg_inf)
        front = jnp.where(hit, cand, front)
        z = z + jnp.exp(m - top)
    inv_z = 1.0 / z

    for j, ds in enumerate(subs):
        r1 = cc_ref[0, :, ds]
        cc = jnp.zeros((nk, w), F32)
        for a in range(k_top):
            cc = jnp.where(r1 == float(a), cnt[a, j:j + 1, :], cc)
        cc_ref[0, :, ds] = cc
        e1_ref[0, :, ds] = jnp.exp(s_ref[0, 0, :, ds] - v1[0, j:j + 1, :]) * inv_z[0, j:j + 1, :]
        e2_ref[0, :, ds] = jnp.exp(s_ref[0, 1, :, ds] - v2[0, j:j + 1, :]).astype(BF16)

    @pl.when(jnp.sum(redo) > 0.0)
    def _():
        for ds in subs:
            _topk_exact_tile(s_ref, r2_ref, e2_ref, cc_ref, e1_ref, ds)


def _topk(st):
    heads, _, nk, t = st.shape
    tm = TOPK_TM
    out = pl.BlockSpec((1, nk, tm), lambda i, h: (h, 0, i))
    return pl.pallas_call(
        _topk_kernel,
        grid=(t // tm, heads),
        in_specs=[pl.BlockSpec((1, 2, nk, tm), lambda i, h: (h, 0, 0, i))],
        out_specs=[out] * 4,
        out_shape=[jax.ShapeDtypeStruct((heads, nk, t), dt) for dt in (BF16, BF16, F32, F32)],
        scratch_shapes=[pltpu.VMEM((PEER_TOPK, tm // TOPK_SUB, TOPK_SUB), F32)] * 2,
        compiler_params=_params("parallel", "parallel"),
        name="peer_topk",
    )(st)


def _peer_kernel(hnt_ref, u_ref, vt_ref, r2_in_ref, e2_in_ref, cc_ref, e1_ref, h_ref, o_ref,
                 acc_ref, a_ref, ht_ref, r2_ref, e2_ref):
    e = pl.program_id(1)
    te = u_ref.shape[0]
    tm = hnt_ref.shape[1]
    nk = r2_ref.shape[1]
    rows_per_step = te // nk

    @pl.when(e == 0)
    def _():
        acc_ref[...] = jnp.zeros_like(acc_ref)
        r2_ref[...] = r2_in_ref[...]
        e2_ref[...] = e2_in_ref[...]

    a_ref[...] = jnp.dot(u_ref[...], hnt_ref[...], preferred_element_type=F32)
    for ii in range(rows_per_step):
        i = e * rows_per_step + ii
        rows = slice(ii * nk, (ii + 1) * nk)
        cc_rows = [cc_ref[hh, pl.ds(i, 1), :] for hh in range(PEER_HEADS)]
        e1_rows = [e1_ref[hh, pl.ds(i, 1), :] for hh in range(PEER_HEADS)]
        for lt in range(tm // PEER_LANES):
            lanes = slice(lt * PEER_LANES, (lt + 1) * PEER_LANES)
            gate = None
            for hh in range(PEER_HEADS):
                cc = jnp.broadcast_to(cc_rows[hh][:, lanes], (nk, PEER_LANES)).astype(BF16)
                e1 = jnp.broadcast_to(e1_rows[hh][:, lanes], (nk, PEER_LANES)).astype(BF16)
                term = jnp.where(r2_ref[hh, :, lanes] < cc, e2_ref[hh, :, lanes], jnp.zeros((), BF16)) * e1
                gate = term if gate is None else gate + term
            a = a_ref[rows, lanes]
            act = 0.5 * a * (1.0 + lax.erf(a * np.float32(1.0 / np.sqrt(2.0))))
            ht_ref[rows, lanes] = act.astype(BF16) * gate
    acc_ref[...] += jnp.dot(vt_ref[...], ht_ref[...], preferred_element_type=F32)

    @pl.when(e == pl.num_programs(1) - 1)
    def _():
        o_ref[...] = h_ref[...] + acc_ref[...].T


def _peer(hnt, u_bf, vt_bf, r2, e2, cc, e1, h2):
    d, t = hnt.shape
    n_exp = u_bf.shape[0]
    heads, nk, _ = r2.shape
    tm, te = PEER_TM, PEER_TE
    tab = pl.BlockSpec((heads, nk, tm), lambda i, e: (0, 0, i))
    return pl.pallas_call(
        _peer_kernel,
        grid=(t // tm, n_exp // te),
        in_specs=[pl.BlockSpec((d, tm), lambda i, e: (0, i)),
                  pl.BlockSpec((te, d), lambda i, e: (e, 0)),
                  pl.BlockSpec((d, te), lambda i, e: (0, e)),
                  tab, tab, tab, tab,
                  pl.BlockSpec((tm, d), lambda i, e: (i, 0))],
        out_specs=pl.BlockSpec((tm, d), lambda i, e: (i, 0)),
        out_shape=jax.ShapeDtypeStruct((t, d), F32),
        scratch_shapes=[pltpu.VMEM((d, tm), F32), pltpu.VMEM((te, tm), F32), pltpu.VMEM((te, tm), BF16),
                        pltpu.VMEM((heads, nk, tm), BF16), pltpu.VMEM((heads, nk, tm), BF16)],
        compiler_params=_params("parallel", "arbitrary"),
        name="peer_experts",
    )(hnt, u_bf, vt_bf, r2, e2, cc, e1, h2)


def _rmsnorm_kernel(x_ref, g_ref, o_ref):
    x = x_ref[...]
    o_ref[...] = x * lax.rsqrt(jnp.mean(x * x, axis=-1, keepdims=True) + EPS) * g_ref[...]


def _rmsnorm(x2, g):
    t, d = x2.shape
    tm = 512
    return pl.pallas_call(
        _rmsnorm_kernel,
        grid=(t // tm,),
        in_specs=[pl.BlockSpec((tm, d), lambda i: (i, 0)), _const_spec((1, d))],
        out_specs=pl.BlockSpec((tm, d), lambda i: (i, 0)),
        out_shape=jax.ShapeDtypeStruct((t, d), F32),
        compiler_params=_params("parallel"),
        name="final_norm",
    )(x2, g.reshape(1, d))


def _rotary_tables(seq):
    half = RET_DK // 2
    inv = 1.0 / (ROPE_BASE ** jnp.linspace(0.0, 1.0, half, dtype=F32))
    ang = jnp.arange(seq, dtype=F32)[:, None] * inv[None, :]
    cos, sin = jnp.cos(ang), jnp.sin(ang)
    cos_t = jnp.tile(jnp.concatenate([cos, cos], axis=-1), (1, RET_HEADS))
    sin_t = jnp.tile(jnp.concatenate([-sin, sin], axis=-1), (1, RET_HEADS))
    return cos_t, sin_t


def _pool_weights(pool_w, pool_scale):
    halves = POOL_WIDTH // 128
    per = 128 // POOL_GROUP
    blocks = []
    for hf in range(halves):
        blk = jnp.zeros((128, 128), F32)
        for g in range(per):
            sl = slice(g * POOL_GROUP, (g + 1) * POOL_GROUP)
            blk = blk.at[sl, sl].set(pool_w[hf * per + g])
        blocks.append(blk)
    return jnp.stack(blocks).astype(BF16), pool_scale.reshape(halves, 1, 128)


def kernel(x, norm_mix, w_in, pool_w, pool_scale, ret_decay, na_rpb, w_br_pool, w_br_ret, w_br_na,
           w_out, norm_ffn, peer_w_query, peer_sub_keys, peer_u, peer_v, norm_final):
    batch, seq, d = x.shape
    depth = w_in.shape[0]
    rows = seq // GRID_W
    kr = min(NA_ROWS_MAX, rows)
    cos_t, sin_t = _rotary_tables(seq)
    x2 = x.reshape(batch * seq, d)
    for l in range(depth):
        pool, r_q, r_k, r_v, r_g, n_q, n_k, n_v, gates = _inproj(x2, norm_mix[l], w_in[l].astype(BF16))
        w_halves, scale_halves = _pool_weights(pool_w[l], pool_scale[l])
        y_pool = _pool(pool, w_halves, scale_halves, batch, seq)
        lg = jax.nn.log_sigmoid(ret_decay[l].astype(F32))
        y_ret = _retention(lg, r_q, r_k, r_v, r_g, cos_t, sin_t, batch, seq)
        y_na = _na(n_q, n_k, n_v, _na_bias_table(na_rpb[l], rows, kr), batch, seq)
        h2, hnt, st = _merge(x2, y_pool, y_ret, y_na, gates,
                             w_br_pool[l].astype(BF16), w_br_ret[l].astype(BF16), w_br_na[l].astype(BF16),
                             w_out[l].astype(BF16), norm_ffn[l], peer_w_query[l].astype(BF16),
                             peer_sub_keys[l].astype(BF16))
        r2, e2, cc, e1 = _topk(st)
        x2 = _peer(hnt, peer_u[l].astype(BF16), peer_v[l].T.astype(BF16), r2, e2, cc, e1, h2)
    return _rmsnorm(x2, norm_final).reshape(batch, seq, d)
```

```python
import functools

import numpy as np
import jax
import jax.numpy as jnp
from jax import lax
from jax.experimental import pallas as pl
from jax.experimental.pallas import tpu as pltpu

F32 = jnp.float32
BF16 = jnp.bfloat16

EPS = 1e-6
GRID_W = 64
NEG = -1e30

POOL_WINDOWS = (2, 4, 8, 16)
POOL_GROUP = 64
POOL_WIDTH = POOL_GROUP * len(POOL_WINDOWS)

RET_HEADS = 4
RET_DK = 64
RET_DV = 128
RET_QK = RET_HEADS * RET_DK
RET_V = RET_HEADS * RET_DV
ROPE_BASE = 10000.0

NA_HEADS = 4
NA_DH = 64
NA_WIDTH = NA_HEADS * NA_DH
NA_ROWS_MAX = 8
NA_COLS = 16

N_BRANCH = 3

PEER_KEYS = 128
PEER_HEADS = 8
PEER_DK = 128
PEER_TOPK = 16

V7X_VMEM_LIMIT_BYTES = 56 * 1024 * 1024
V7X_MXU_DIM = 256

INPROJ_TM = 256
RET_CHUNK = 256
NA_ROWS_PER_STEP = 8
MERGE_TM = 256
TOPK_TM = 1024
TOPK_SUB = 128
PEER_TM = 512
PEER_TE = 1024
PEER_LANES = 128


def _params(*sem):
    return pltpu.CompilerParams(dimension_semantics=sem, vmem_limit_bytes=V7X_VMEM_LIMIT_BYTES)


def _const_spec(shape):
    nd = len(shape)
    return pl.BlockSpec(shape, lambda *_: (0,) * nd)


def _inproj_kernel(x_ref, g_ref, w_ref, pool_ref, rq_ref, rk_ref, rv_ref, rg_ref,
                   nq_ref, nk_ref, nv_ref, gate_ref):
    x = x_ref[...]
    xn = x * lax.rsqrt(jnp.mean(x * x, axis=-1, keepdims=True) + EPS) * g_ref[...]
    xb = xn.astype(BF16)
    off = 0
    for ref, scale in ((pool_ref, None), (rq_ref, None), (rk_ref, None), (rv_ref, None), (rg_ref, None),
                       (nq_ref, NA_DH ** -0.5), (nk_ref, None), (nv_ref, None)):
        width = ref.shape[-1]
        y = jnp.dot(xb, w_ref[:, off:off + width], preferred_element_type=F32)
        if scale is not None:
            y = y * scale
        ref[...] = y.astype(ref.dtype)
        off += width
    d = x.shape[-1]
    for b in range(N_BRANCH):
        gate_ref[:, b * d:(b + 1) * d] = jnp.dot(
            xb, w_ref[:, off + b * d:off + (b + 1) * d], preferred_element_type=F32)


def _inproj(x2, g, w_in_bf):
    t, d = x2.shape
    tm = INPROJ_TM
    widths = (POOL_WIDTH, RET_QK, RET_QK, RET_V, RET_V, NA_WIDTH, NA_WIDTH, NA_WIDTH, N_BRANCH * d)
    dtypes = (F32, F32, F32, BF16, F32, BF16, BF16, BF16, F32)
    tok = lambda w: pl.BlockSpec((tm, w), lambda i: (i, 0))
    return pl.pallas_call(
        _inproj_kernel,
        grid=(t // tm,),
        in_specs=[tok(d), _const_spec((1, d)), _const_spec(w_in_bf.shape)],
        out_specs=[tok(w) for w in widths],
        out_shape=[jax.ShapeDtypeStruct((t, w), dt) for w, dt in zip(widths, dtypes)],
        compiler_params=_params("parallel"),
        name="inproj",
    )(x2, g.reshape(1, d), w_in_bf)


def _pool_kernel(p_ref, w_ref, scale_ref, o_ref, *, halves):
    x = p_ref[...]
    s = x.shape[0]
    t = lax.broadcasted_iota(jnp.int32, x.shape, 0)
    lane = lax.broadcasted_iota(jnp.int32, x.shape, 1)
    half = pl.program_id(1)

    def down(v, k):
        return jnp.where(t >= k, pltpu.roll(v, k, 0), 0.0)

    def up(v, k):
        return jnp.where(t < s - k, pltpu.roll(v, s - k, 0), 0.0)

    trail = {1: x}
    lead = {1: x}
    for m in (2, 4, 8):
        trail[m] = trail[m // 2] + down(trail[m // 2], m // 2)
        lead[m] = lead[m // 2] + up(lead[m // 2], m // 2)

    def window(hw):
        return down(trail[hw], 1) + lead[hw]

    def pick(lo_first, hi_first, lo_second, hi_second):
        first = jnp.where(lane < POOL_GROUP, lo_first, hi_first)
        second = jnp.where(lane < POOL_GROUP, lo_second, hi_second)
        return jnp.where(half == 0, first, second)

    hw = [w // 2 for w in POOL_WINDOWS]
    wsum = pick(window(hw[0]), window(hw[1]), window(hw[2]), window(hw[3]))
    hwv = pick(hw[0], hw[1], hw[2], hw[3])
    cnt = (jnp.minimum(t + hwv, s) - jnp.maximum(t - hwv, 0)).astype(F32)
    dlt = wsum / cnt - x
    y = jnp.dot(dlt.astype(BF16), w_ref[0], preferred_element_type=F32)
    o_ref[...] = y * scale_ref[0]


def _pool(p_pool, w_halves, scale_halves, batch, seq):
    t = p_pool.shape[0]
    halves = POOL_WIDTH // 128
    return pl.pallas_call(
        functools.partial(_pool_kernel, halves=halves),
        grid=(batch, halves),
        in_specs=[pl.BlockSpec((seq, 128), lambda b, h: (b, h)),
                  pl.BlockSpec((1, 128, 128), lambda b, h: (h, 0, 0)),
                  pl.BlockSpec((1, 1, 128), lambda b, h: (h, 0, 0))],
        out_specs=pl.BlockSpec((seq, 128), lambda b, h: (b, h)),
        out_shape=jax.ShapeDtypeStruct((t, POOL_WIDTH), F32),
        compiler_params=_params("parallel", "parallel"),
        name="pool",
    )(p_pool, w_halves, scale_halves)


def _rotary(x, cos, sin_signed, first_half):
    width = x.shape[-1]
    half = RET_DK // 2
    swapped = jnp.where(first_half, pltpu.roll(x, width - half, 1), pltpu.roll(x, half, 1))
    return x * cos + swapped * sin_signed


def _retention_kernel(lg_ref, q_ref, k_ref, v_ref, cos_ref, sin_ref, *rest, reverse):
    if reverse:
        fwd_ref, g_ref, o_ref, state_ref = rest
    else:
        o_ref, state_ref = rest
    c = q_ref.shape[0]

    @pl.when(pl.program_id(1) == 0)
    def _():
        state_ref[...] = jnp.zeros_like(state_ref)

    lane = lax.broadcasted_iota(jnp.int32, (c, RET_QK), 1)
    first_half = (lane % RET_DK) < (RET_DK // 2)
    cos = cos_ref[...]
    sin = sin_ref[...]
    q = _rotary(q_ref[...], cos, sin, first_half)
    k = _rotary(k_ref[...], cos, sin, first_half) * (RET_DK ** -0.5)
    v = v_ref[...]

    row = lax.broadcasted_iota(jnp.int32, (c, c), 0)
    col = lax.broadcasted_iota(jnp.int32, (c, c), 1)
    pos = lax.broadcasted_iota(jnp.int32, (c, 1), 0).astype(F32)
    if reverse:
        diff = col - row
        mask = diff > 0
        q_pow = c - pos
        k_pow = pos
    else:
        diff = row - col
        mask = diff >= 0
        q_pow = pos + 1.0
        k_pow = (c - 1) - pos
    dist = jnp.where(mask, diff, 0).astype(F32)
    direction = 1 if reverse else 0

    for h in range(RET_HEADS):
        lg = lg_ref[direction, h]
        qh = q[:, h * RET_DK:(h + 1) * RET_DK]
        kh = k[:, h * RET_DK:(h + 1) * RET_DK]
        vh = v[:, h * RET_DV:(h + 1) * RET_DV]
        decay = jnp.where(mask, jnp.exp(lg * dist), 0.0)
        scores = lax.dot_general(qh.astype(BF16), kh.astype(BF16), (((1,), (1,)), ((), ())),
                                 preferred_element_type=F32) * decay
        intra = jnp.dot(scores.astype(BF16), vh, preferred_element_type=F32)
        q_dec = qh * jnp.exp(lg * q_pow)
        k_dec = kh * jnp.exp(lg * k_pow)
        state = state_ref[h]
        cross = jnp.dot(q_dec.astype(BF16), state.astype(BF16), preferred_element_type=F32)
        chunk_kv = lax.dot_general(k_dec.astype(BF16), vh, (((0,), (0,)), ((), ())),
                                   preferred_element_type=F32)
        chunk_decay = jnp.exp(lg * jnp.full((1, RET_DV), float(c), F32))
        state_ref[h] = state * chunk_decay + chunk_kv
        out = intra + cross
        sl = slice(h * RET_DV, (h + 1) * RET_DV)
        if reverse:
            y = fwd_ref[:, sl] + out
            y = y * lax.rsqrt(jnp.mean(y * y, axis=-1, keepdims=True) + EPS)
            g = g_ref[:, sl]
            o_ref[:, sl] = (g / (1.0 + jnp.exp(-g))) * y
        else:
            o_ref[:, sl] = out


def _retention(lg, r_q, r_k, r_v, r_g, cos_t, sin_t, batch, seq):
    t = r_q.shape[0]
    c = RET_CHUNK
    n = seq // c
    scratch = [pltpu.VMEM((RET_HEADS, RET_DK, RET_DV), F32)]
    smem = pl.BlockSpec(memory_space=pltpu.SMEM)

    def specs(chunk_of):
        tok = lambda w: pl.BlockSpec((c, w), lambda b, i: (b * n + chunk_of(i), 0))
        tab = pl.BlockSpec((c, RET_QK), lambda b, i: (chunk_of(i), 0))
        return tok, tab

    tok, tab = specs(lambda i: i)
    fwd = pl.pallas_call(
        functools.partial(_retention_kernel, reverse=False),
        grid=(batch, n),
        in_specs=[smem, tok(RET_QK), tok(RET_QK), tok(RET_V), tab, tab],
        out_specs=tok(RET_V),
        out_shape=jax.ShapeDtypeStruct((t, RET_V), F32),
        scratch_shapes=scratch,
        compiler_params=_params("parallel", "arbitrary"),
        name="retention_fwd",
    )(lg, r_q, r_k, r_v, cos_t, sin_t)
    tok, tab = specs(lambda i: n - 1 - i)
    return pl.pallas_call(
        functools.partial(_retention_kernel, reverse=True),
        grid=(batch, n),
        in_specs=[smem, tok(RET_QK), tok(RET_QK), tok(RET_V), tab, tab, tok(RET_V), tok(RET_V)],
        out_specs=tok(RET_V),
        out_shape=jax.ShapeDtypeStruct((t, RET_V), F32),
        scratch_shapes=scratch,
        compiler_params=_params("parallel", "arbitrary"),
        name="retention_bwd",
    )(lg, r_q, r_k, r_v, cos_t, sin_t, fwd, r_g)


def _na_kernel(q_ref, k_ref, v_ref, bias_ref, o_ref, *, rows, kr):
    step = pl.program_id(1)
    rps = q_ref.shape[0] // GRID_W
    nkeys = kr * GRID_W
    lo = kr // 2
    hi_rows = rows - kr - lo

    def body(i, carry):
        r = step * rps + i
        r0 = jnp.clip(r - lo, 0, rows - kr)
        ver = jnp.where(r < lo, r, jnp.where(r > rows - kr + lo, r - (rows - kr), lo))
        kk = k_ref[pl.ds(pl.multiple_of(r0 * GRID_W, GRID_W), nkeys), :]
        vv = v_ref[pl.ds(pl.multiple_of(r0 * GRID_W, GRID_W), nkeys), :]
        qq = q_ref[pl.ds(pl.multiple_of(i * GRID_W, GRID_W), GRID_W), :]
        outs = []
        for h in range(NA_HEADS):
            sl = slice(h * NA_DH, (h + 1) * NA_DH)
            s = lax.dot_general(qq[:, sl], kk[:, sl], (((1,), (1,)), ((), ())),
                                preferred_element_type=F32) + bias_ref[ver, h]
            m = jnp.max(s, axis=-1, keepdims=True)
            p = jnp.exp(s - m)
            p = p / jnp.sum(p, axis=-1, keepdims=True)
            outs.append(jnp.dot(p.astype(BF16), vv[:, sl], preferred_element_type=F32))
        o_ref[pl.ds(pl.multiple_of(i * GRID_W, GRID_W), GRID_W), :] = jnp.concatenate(outs, axis=-1)
        return carry

    del hi_rows
    lax.fori_loop(0, rps, body, 0)


def _na_bias_table(rpb, rows, kr):
    lo = kr // 2
    reps = list(range(lo)) + [lo] + list(range(rows - kr + lo + 1, rows))
    c = np.arange(GRID_W)
    col_start = np.clip(c - NA_COLS // 2, 0, GRID_W - NA_COLS)
    col_in = (c[None, :] >= col_start[:, None]) & (c[None, :] < col_start[:, None] + NA_COLS)
    dj = np.clip(c[None, :] - c[:, None], -(NA_COLS - 1), NA_COLS - 1) + (NA_COLS - 1)
    pick = ((dj[None] == np.arange(2 * NA_COLS - 1)[:, None, None]) & col_in[None]).astype(np.float32)
    di0 = [int(np.clip(r - lo, 0, rows - kr)) - r + (NA_ROWS_MAX - 1) for r in reps]
    rpb_rows = jnp.stack([rpb[:, d0:d0 + kr, :] for d0 in di0]).astype(F32)
    bias = jnp.einsum('vhij,jck->vhcik', rpb_rows, jnp.asarray(pick), precision=lax.Precision.HIGHEST)
    bias = bias + jnp.asarray(np.where(col_in, 0.0, NEG).astype(np.float32))[None, None, :, None, :]
    return bias.reshape(len(reps), NA_HEADS, GRID_W, kr * GRID_W)


def _na(n_q, n_k, n_v, bias, batch, seq):
    t = n_q.shape[0]
    rows = seq // GRID_W
    kr = min(NA_ROWS_MAX, rows)
    rps = min(NA_ROWS_PER_STEP, rows)
    blk = rps * GRID_W
    steps = rows // rps
    return pl.pallas_call(
        functools.partial(_na_kernel, rows=rows, kr=kr),
        grid=(batch, steps),
        in_specs=[pl.BlockSpec((blk, NA_WIDTH), lambda b, i: (b * steps + i, 0)),
                  pl.BlockSpec((seq, NA_WIDTH), lambda b, i: (b, 0)),
                  pl.BlockSpec((seq, NA_WIDTH), lambda b, i: (b, 0)),
                  _const_spec(bias.shape)],
        out_specs=pl.BlockSpec((blk, NA_WIDTH), lambda b, i: (b * steps + i, 0)),
        out_shape=jax.ShapeDtypeStruct((t, NA_WIDTH), F32),
        compiler_params=_params("parallel", "arbitrary"),
        name="natten",
    )(n_q, n_k, n_v, bias)


def _merge_kernel(x_ref, yp_ref, yr_ref, yn_ref, gate_ref, wp_ref, wr_ref, wn_ref, wo_ref,
                  gf_ref, wq_ref, keys_ref, h_ref, hnt_ref, st_ref):
    d = x_ref.shape[-1]
    merged = None
    for b, (y_ref, w_ref) in enumerate(((yp_ref, wp_ref), (yr_ref, wr_ref), (yn_ref, wn_ref))):
        gate = 1.0 / (1.0 + jnp.exp(-gate_ref[:, b * d:(b + 1) * d]))
        term = gate * jnp.dot(y_ref[...].astype(BF16), w_ref[...], preferred_element_type=F32)
        merged = term if merged is None else merged + term
    h = x_ref[...] + jnp.dot(merged.astype(BF16), wo_ref[...], preferred_element_type=F32)
    h_ref[...] = h
    hn = h * lax.rsqrt(jnp.mean(h * h, axis=-1, keepdims=True) + EPS) * gf_ref[...]
    hnb = hn.astype(BF16)
    hnt_ref[...] = hn.T.astype(BF16)
    q = jnp.dot(hnb, wq_ref[...], preferred_element_type=F32).astype(BF16)
    for hh in range(PEER_HEADS):
        for p in range(2):
            col = (hh * 2 + p) * PEER_DK
            st_ref[hh, p] = lax.dot_general(keys_ref[p], q[:, col:col + PEER_DK],
                                            (((1,), (1,)), ((), ())), preferred_element_type=F32)


def _merge(x2, y_pool, y_ret, y_na, gates, wp, wr, wn, wo, g_ffn, wq, keys):
    t, d = x2.shape
    tm = MERGE_TM
    tok = lambda w: pl.BlockSpec((tm, w), lambda i: (i, 0))
    return pl.pallas_call(
        _merge_kernel,
        grid=(t // tm,),
        in_specs=[tok(d), tok(POOL_WIDTH), tok(RET_V), tok(NA_WIDTH), tok(N_BRANCH * d),
                  _const_spec(wp.shape), _const_spec(wr.shape), _const_spec(wn.shape), _const_spec(wo.shape),
                  _const_spec((1, d)), _const_spec(wq.shape), _const_spec(keys.shape)],
        out_specs=[tok(d),
                   pl.BlockSpec((d, tm), lambda i: (0, i)),
                   pl.BlockSpec((PEER_HEADS, 2, PEER_KEYS, tm), lambda i: (0, 0, 0, i))],
        out_shape=[jax.ShapeDtypeStruct((t, d), F32),
                   jax.ShapeDtypeStruct((d, t), BF16),
                   jax.ShapeDtypeStruct((PEER_HEADS, 2, PEER_KEYS, t), F32)],
        compiler_params=_params("parallel"),
        name="merge",
    )(x2, y_pool, y_ret, y_na, gates, wp, wr, wn, wo, g_ffn.reshape(1, d), wq, keys)


def _topk_exact_tile(s_ref, r2_ref, e2_ref, cc_ref, e1_ref, ds):
    nk = s_ref.shape[2]
    k_top = PEER_TOPK
    w = ds.stop - ds.start
    key = lax.broadcasted_iota(jnp.int32, (nk, w), 0)
    arow = lax.broadcasted_iota(jnp.int32, (k_top, w), 0)
    neg_inf = -jnp.inf

    def top_ranks(s):
        def body(a, carry):
            s, rank, vals = carry
            m = jnp.max(s, axis=0, keepdims=True)
            idx = jnp.min(jnp.where(s == m, key, nk), axis=0, keepdims=True)
            hit = key == idx
            return (jnp.where(hit, neg_inf, s), jnp.where(hit, a, rank), jnp.where(arow == a, m, vals))
        init = (s, jnp.full((nk, w), k_top, jnp.int32), jnp.zeros((k_top, w), F32))
        _, rank, vals = lax.fori_loop(0, k_top, body, init)
        return rank, vals

    s1 = s_ref[0, 0, :, ds]
    s2 = s_ref[0, 1, :, ds]
    r1, v1 = top_ranks(s1)
    r2, v2 = top_ranks(s2)
    top = v1[0:1] + v2[0:1]

    def merge(_, carry):
        cnt, front, z = carry
        m = jnp.max(front, axis=0, keepdims=True)
        a_star = jnp.min(jnp.where(front == m, arow, k_top), axis=0, keepdims=True)
        hit = arow == a_star
        cnt = jnp.where(hit, cnt + 1, cnt)
        nxt = jnp.sum(jnp.where(hit, cnt, 0), axis=0, keepdims=True)
        v2_next = jnp.sum(jnp.where(arow == nxt, v2, 0.0), axis=0, keepdims=True)
        cand = jnp.where(nxt < k_top, v1 + v2_next, neg_inf)
        return cnt, jnp.where(hit, cand, front), z + jnp.exp(m - top)
    init = (jnp.zeros((k_top, w), jnp.int32), v1 + v2[0:1], jnp.zeros((1, w), F32))
    cnt, _, z = lax.fori_loop(0, k_top, merge, init)

    cc = jnp.zeros((nk, w), jnp.int32)
    for a in range(k_top):
        cc = jnp.where(r1 == a, cnt[a:a + 1], cc)
    r2_ref[0, :, ds] = r2.astype(F32).astype(BF16)
    cc_ref[0, :, ds] = cc.astype(F32)
    e2_ref[0, :, ds] = jnp.exp(s2 - v2[0:1]).astype(BF16)
    e1_ref[0, :, ds] = jnp.exp(s1 - v1[0:1]) * (1.0 / z)


def _topk_kernel(s_ref, r2_ref, e2_ref, cc_ref, e1_ref, v1_ref, v2_ref):
    nk = s_ref.shape[2]
    tm = s_ref.shape[3]
    k_top = PEER_TOPK
    w = TOPK_SUB
    subs = [slice(j * w, (j + 1) * w) for j in range(tm // w)]
    neg_inf = -jnp.inf
    no_tie_rank_sum = float(sum(range(k_top)) + k_top * (nk - k_top))
    mark_base, mark_step = 2.0 ** 100, 2.0 ** 96
    lowest_score = -(2.0 ** 99)

    redo = jnp.zeros((1, w), F32)
    for j, ds in enumerate(subs):
        for p, (rank_ref, vals_ref) in enumerate(((cc_ref, v1_ref), (r2_ref, v2_ref))):
            s = s_ref[0, p, :, ds]
            floor_ok = jnp.min(s, axis=0, keepdims=True) > lowest_score
            for a in range(k_top):
                m = jnp.max(s, axis=0, keepdims=True)
                s = jnp.where(s == m, -(mark_base + a * mark_step), s)
                vals_ref[a, j:j + 1, :] = m
            rank = jnp.where(s < lowest_score, s * (-1.0 / mark_step) - mark_base / mark_step, float(k_top))
            exact = floor_ok & (jnp.sum(rank, axis=0, keepdims=True) == no_tie_rank_sum)
            redo = redo + jnp.where(exact, 0.0, 1.0)
            rank_ref[0, :, ds] = rank.astype(rank_ref.dtype)

    v1 = v1_ref[...]
    v2 = v2_ref[...]
    arow = lax.broadcasted_iota(jnp.int32, v1.shape, 0).astype(F32)
    top = v1[0:1] + v2[0:1]
    cnt = jnp.zeros(v1.shape, F32)
    front = v1 + v2[0:1]
    z = jnp.zeros(top.shape, F32)
    for _ in range(k_top):
        m = jnp.max(front, axis=0, keepdims=True)
        a_star = jnp.min(jnp.where(front == m, arow, float(k_top)), axis=0, keepdims=True)
        hit = arow == a_star
        cnt = jnp.where(hit, cnt + 1.0, cnt)
        nxt = jnp.sum(jnp.where(hit, cnt, 0.0), axis=0, keepdims=True)
        v2_next = jnp.sum(jnp.where(arow == nxt, v2, 0.0), axis=0, keepdims=True)
        cand = jnp.where(nxt < float(k_top), v1 + v2_next, neg_inf)
        front = jnp.where(hit, cand, front)
        z = z + jnp.exp(m - top)
    inv_z = 1.0 / z

    for j, ds in enumerate(subs):
        r1 = cc_ref[0, :, ds]
        cc = jnp.zeros((nk, w), F32)
        for a in range(k_top):
            cc = jnp.where(r1 == float(a), cnt[a, j:j + 1, :], cc)
        cc_ref[0, :, ds] = cc
        e1_ref[0, :, ds] = jnp.exp(s_ref[0, 0, :, ds] - v1[0, j:j + 1, :]) * inv_z[0, j:j + 1, :]
        e2_ref[0, :, ds] = jnp.exp(s_ref[0, 1, :, ds] - v2[0, j:j + 1, :]).astype(BF16)

    @pl.when(jnp.sum(redo) > 0.0)
    def _():
        for ds in subs:
            _topk_exact_tile(s_ref, r2_ref, e2_ref, cc_ref, e1_ref, ds)


def _topk(st):
    heads, _, nk, t = st.shape
    tm = TOPK_TM
    out = pl.BlockSpec((1, nk, tm), lambda i, h: (h, 0, i))
    return pl.pallas_call(
        _topk_kernel,
        grid=(t // tm, heads),
        in_specs=[pl.BlockSpec((1, 2, nk, tm), lambda i, h: (h, 0, 0, i))],
        out_specs=[out] * 4,
        out_shape=[jax.ShapeDtypeStruct((heads, nk, t), dt) for dt in (BF16, BF16, F32, F32)],
        scratch_shapes=[pltpu.VMEM((PEER_TOPK, tm // TOPK_SUB, TOPK_SUB), F32)] * 2,
        compiler_params=_params("parallel", "parallel"),
        name="peer_topk",
    )(st)


def _peer_item(g, lag, n_items, n_exp_tiles):
    item = jnp.clip(g - lag, 0, n_items - 1)
    return item // n_exp_tiles, item % n_exp_tiles


def _peer_kernel(hnt_ref, u_ref, vt_ref, r2_in_ref, e2_in_ref, cc_ref, e1_ref, h_ref, o_ref,
                 acc_ref, a0_ref, a1_ref, ht0_ref, ht1_ref, r2_ref, e2_ref, ccs_ref, e1s_ref, *, n_exp_tiles):
    g = pl.program_id(0)
    n_items = pl.num_programs(0) - 2
    te = u_ref.shape[0]
    d, tm = hnt_ref.shape
    nk = r2_ref.shape[2]
    rows_per_step = te // nk
    lane_tiles = tm // PEER_LANES
    _, e_gate = _peer_item(g, 1, n_items, n_exp_tiles)
    _, e_acc = _peer_item(g, 2, n_items, n_exp_tiles)

    @pl.when(g == 0)
    def _():
        a1_ref[...] = jnp.zeros_like(a1_ref)
        ht0_ref[...] = jnp.zeros_like(ht0_ref)
        ht1_ref[...] = jnp.zeros_like(ht1_ref)

    @pl.when(e_gate == 0)
    def _():
        for hh in range(PEER_HEADS):
            for lt in range(lane_tiles):
                lanes = slice(lt * PEER_LANES, (lt + 1) * PEER_LANES)
                r2_ref[hh, lt] = r2_in_ref[hh, :, lanes]
                e2_ref[hh, lt] = e2_in_ref[hh, :, lanes]

    @pl.when(e_acc == 0)
    def _():
        acc_ref[...] = jnp.zeros_like(acc_ref)

    def stages(a_new_ref, a_ref, ht_new_ref, ht_ref):
        q = V7X_MXU_DIM
        tiles_per_q = q // PEER_LANES
        mxu_jobs = []
        for nl in range(tm // q):
            for ml in range(max(te, d) // q):
                rws = slice(ml * q, (ml + 1) * q)
                if ml < te // q:
                    mxu_jobs.append((True, nl, rws))
                if ml < d // q:
                    mxu_jobs.append((False, nl, rws))
        n_blocks = rows_per_step * lane_tiles
        blocks_per_job = -(-n_blocks // len(mxu_jobs))

        def run_job(is_first_matmul, nl, rws):
            tiles = range(nl * tiles_per_q, (nl + 1) * tiles_per_q)
            if is_first_matmul:
                out = jnp.dot(u_ref[rws, :], hnt_ref[:, nl * q:(nl + 1) * q], preferred_element_type=F32)
            else:
                h_cols = jnp.concatenate([ht_ref[lt] for lt in tiles], axis=1)
                out = jnp.dot(vt_ref[rws, :], h_cols, preferred_element_type=F32)
            for k, lt in enumerate(tiles):
                piece = out[:, k * PEER_LANES:(k + 1) * PEER_LANES]
                if is_first_matmul:
                    a_new_ref[lt, rws, :] = piece
                else:
                    acc_ref[lt, rws, :] += piece

        first_row = pl.multiple_of(e_gate * rows_per_step, rows_per_step)
        for hh in range(PEER_HEADS):
            ccs_ref[hh] = cc_ref[hh, pl.ds(first_row, rows_per_step), :]
            e1s_ref[hh] = e1_ref[hh, pl.ds(first_row, rows_per_step), :]

        block = 0
        for ii in range(rows_per_step):
            rows = slice(ii * nk, (ii + 1) * nk)
            cc_rows = [ccs_ref[hh, ii:ii + 1, :] for hh in range(PEER_HEADS)]
            e1_rows = [e1s_ref[hh, ii:ii + 1, :] for hh in range(PEER_HEADS)]
            for lt in range(lane_tiles):
                if block % blocks_per_job == 0 and mxu_jobs:
                    run_job(*mxu_jobs.pop(0))
                block += 1
                lanes = slice(lt * PEER_LANES, (lt + 1) * PEER_LANES)
                gate = None
                for hh in range(PEER_HEADS):
                    cc = jnp.broadcast_to(cc_rows[hh][:, lanes], (nk, PEER_LANES)).astype(BF16)
                    e1 = jnp.broadcast_to(e1_rows[hh][:, lanes], (nk, PEER_LANES)).astype(BF16)
                    term = jnp.where(r2_ref[hh, lt] < cc, e2_ref[hh, lt], jnp.zeros((), BF16)) * e1
                    gate = term if gate is None else gate + term
                a = a_ref[lt, rows, :]
                act = 0.5 * a * (1.0 + lax.erf(a * np.float32(1.0 / np.sqrt(2.0))))
                ht_new_ref[lt, rows, :] = act.astype(BF16) * gate
        for job in mxu_jobs:
            run_job(*job)

    @pl.when(g % 2 == 0)
    def _():
        stages(a0_ref, a1_ref, ht1_ref, ht0_ref)

    @pl.when(g % 2 == 1)
    def _():
        stages(a1_ref, a0_ref, ht0_ref, ht1_ref)

    @pl.when((e_acc == n_exp_tiles - 1) & (g >= 2))
    def _():
        for lt in range(lane_tiles):
            toks = slice(lt * PEER_LANES, (lt + 1) * PEER_LANES)
            o_ref[toks, :] = h_ref[toks, :] + acc_ref[lt].T


def _peer(hnt, u_bf, vt_bf, r2, e2, cc, e1, h2):
    d, t = hnt.shape
    n_exp = u_bf.shape[0]
    heads, nk, _ = r2.shape
    tm, te = PEER_TM, PEER_TE
    n_e = n_exp // te
    n_items = (t // tm) * n_e
    lt = tm // PEER_LANES
    tok = lambda lag: (lambda g: _peer_item(g, lag, n_items, n_e)[0])
    exp = lambda lag: (lambda g: _peer_item(g, lag, n_items, n_e)[1])
    tab = pl.BlockSpec((heads, nk, tm), lambda g: (0, 0, tok(1)(g)))
    return pl.pallas_call(
        functools.partial(_peer_kernel, n_exp_tiles=n_e),
        grid=(n_items + 2,),
        in_specs=[pl.BlockSpec((d, tm), lambda g: (0, tok(0)(g))),
                  pl.BlockSpec((te, d), lambda g: (exp(0)(g), 0)),
                  pl.BlockSpec((d, te), lambda g: (0, exp(2)(g))),
                  tab, tab, tab, tab,
                  pl.BlockSpec((tm, d), lambda g: (tok(2)(g), 0))],
        out_specs=pl.BlockSpec((tm, d), lambda g: (tok(2)(g), 0)),
        out_shape=jax.ShapeDtypeStruct((t, d), F32),
        scratch_shapes=[pltpu.VMEM((lt, d, PEER_LANES), F32),
                        pltpu.VMEM((lt, te, PEER_LANES), F32), pltpu.VMEM((lt, te, PEER_LANES), F32),
                        pltpu.VMEM((lt, te, PEER_LANES), BF16), pltpu.VMEM((lt, te, PEER_LANES), BF16),
                        pltpu.VMEM((heads, lt, nk, PEER_LANES), BF16),
                        pltpu.VMEM((heads, lt, nk, PEER_LANES), BF16),
                        pltpu.VMEM((heads, te // nk, tm), F32), pltpu.VMEM((heads, te // nk, tm), F32)],
        compiler_params=_params("arbitrary"),
        name="peer_experts",
    )(hnt, u_bf, vt_bf, r2, e2, cc, e1, h2)


def _rmsnorm_kernel(x_ref, g_ref, o_ref):
    x = x_ref[...]
    o_ref[...] = x * lax.rsqrt(jnp.mean(x * x, axis=-1, keepdims=True) + EPS) * g_ref[...]


def _rmsnorm(x2, g):
    t, d = x2.shape
    tm = 512
    return pl.pallas_call(
        _rmsnorm_kernel,
        grid=(t // tm,),
        in_specs=[pl.BlockSpec((tm, d), lambda i: (i, 0)), _const_spec((1, d))],
        out_specs=pl.BlockSpec((tm, d), lambda i: (i, 0)),
        out_shape=jax.ShapeDtypeStruct((t, d), F32),
        compiler_params=_params("parallel"),
        name="final_norm",
    )(x2, g.reshape(1, d))


def _rotary_tables(seq):
    half = RET_DK // 2
    inv = 1.0 / (ROPE_BASE ** jnp.linspace(0.0, 1.0, half, dtype=F32))
    ang = jnp.arange(seq, dtype=F32)[:, None] * inv[None, :]
    cos, sin = jnp.cos(ang), jnp.sin(ang)
    cos_t = jnp.tile(jnp.concatenate([cos, cos], axis=-1), (1, RET_HEADS))
    sin_t = jnp.tile(jnp.concatenate([-sin, sin], axis=-1), (1, RET_HEADS))
    return cos_t, sin_t


def _pool_weights(pool_w, pool_scale):
    halves = POOL_WIDTH // 128
    per = 128 // POOL_GROUP
    blocks = []
    for hf in range(halves):
        blk = jnp.zeros((128, 128), F32)
        for g in range(per):
            sl = slice(g * POOL_GROUP, (g + 1) * POOL_GROUP)
            blk = blk.at[sl, sl].set(pool_w[hf * per + g])
        blocks.append(blk)
    return jnp.stack(blocks).astype(BF16), pool_scale.reshape(halves, 1, 128)


def kernel(x, norm_mix, w_in, pool_w, pool_scale, ret_decay, na_rpb, w_br_pool, w_br_ret, w_br_na,
           w_out, norm_ffn, peer_w_query, peer_sub_keys, peer_u, peer_v, norm_final):
    batch, seq, d = x.shape
    depth = w_in.shape[0]
    rows = seq // GRID_W
    kr = min(NA_ROWS_MAX, rows)
    cos_t, sin_t = _rotary_tables(seq)
    x2 = x.reshape(batch * seq, d)
    for l in range(depth):
        pool, r_q, r_k, r_v, r_g, n_q, n_k, n_v, gates = _inproj(x2, norm_mix[l], w_in[l].astype(BF16))
        w_halves, scale_halves = _pool_weights(pool_w[l], pool_scale[l])
        y_pool = _pool(pool, w_halves, scale_halves, batch, seq)
        lg = jax.nn.log_sigmoid(ret_decay[l].astype(F32))
        y_ret = _retention(lg, r_q, r_k, r_v, r_g, cos_t, sin_t, batch, seq)
        y_na = _na(n_q, n_k, n_v, _na_bias_table(na_rpb[l], rows, kr), batch, seq)
        h2, hnt, st = _merge(x2, y_pool, y_ret, y_na, gates,
                             w_br_pool[l].astype(BF16), w_br_ret[l].astype(BF16), w_br_na[l].astype(BF16),
                             w_out[l].astype(BF16), norm_ffn[l], peer_w_query[l].astype(BF16),
                             peer_sub_keys[l].astype(BF16))
        r2, e2, cc, e1 = _topk(st)
        x2 = _peer(hnt, peer_u[l].astype(BF16), peer_v[l].T.astype(BF16), r2, e2, cc, e1, h2)
    return _rmsnorm(x2, norm_final).reshape(batch, seq, d)
```

```python
import functools

import numpy as np
import jax
import jax.numpy as jnp
from jax import lax
from jax.experimental import pallas as pl
from jax.experimental.pallas import tpu as pltpu

F32 = jnp.float32
BF16 = jnp.bfloat16

EPS = 1e-6
GRID_W = 64
NEG = -1e30

POOL_WINDOWS = (2, 4, 8, 16)
POOL_GROUP = 64
POOL_WIDTH = POOL_GROUP * len(POOL_WINDOWS)

RET_HEADS = 4
RET_DK = 64
RET_DV = 128
RET_QK = RET_HEADS * RET_DK
RET_V = RET_HEADS * RET_DV
ROPE_BASE = 10000.0

NA_HEADS = 4
NA_DH = 64
NA_WIDTH = NA_HEADS * NA_DH
NA_ROWS_MAX = 8
NA_COLS = 16

N_BRANCH = 3

PEER_KEYS = 128
PEER_HEADS = 8
PEER_DK = 128
PEER_TOPK = 16

V7X_VMEM_LIMIT_BYTES = 56 * 1024 * 1024
V7X_MXU_DIM = 256

INPROJ_TM = 256
RET_CHUNK = 256
NA_ROWS_PER_STEP = 8
MERGE_TM = 256
TOPK_TM = 1024
TOPK_SUB = 128
PEER_TM = 512
PEER_TE = 1024
PEER_LANES = 128


def _params(*sem):
    return pltpu.CompilerParams(dimension_semantics=sem, vmem_limit_bytes=V7X_VMEM_LIMIT_BYTES)


def _const_spec(shape):
    nd = len(shape)
    return pl.BlockSpec(shape, lambda *_: (0,) * nd)


def _inproj_kernel(x_ref, g_ref, w_ref, pool_ref, rq_ref, rk_ref, rv_ref, rg_ref,
                   nq_ref, nk_ref, nv_ref, gate_ref):
    x = x_ref[...]
    xn = x * lax.rsqrt(jnp.mean(x * x, axis=-1, keepdims=True) + EPS) * g_ref[...]
    xb = xn.astype(BF16)
    off = 0
    for ref, scale in ((pool_ref, None), (rq_ref, None), (rk_ref, None), (rv_ref, None), (rg_ref, None),
                       (nq_ref, NA_DH ** -0.5), (nk_ref, None), (nv_ref, None)):
        width = ref.shape[-1]
        y = jnp.dot(xb, w_ref[:, off:off + width], preferred_element_type=F32)
        if scale is not None:
            y = y * scale
        ref[...] = y.astype(ref.dtype)
        off += width
    d = x.shape[-1]
    for b in range(N_BRANCH):
        gate_ref[:, b * d:(b + 1) * d] = jnp.dot(
            xb, w_ref[:, off + b * d:off + (b + 1) * d], preferred_element_type=F32)


def _inproj(x2, g, w_in_bf):
    t, d = x2.shape
    tm = INPROJ_TM
    widths = (POOL_WIDTH, RET_QK, RET_QK, RET_V, RET_V, NA_WIDTH, NA_WIDTH, NA_WIDTH, N_BRANCH * d)
    dtypes = (F32, F32, F32, BF16, F32, BF16, BF16, BF16, F32)
    tok = lambda w: pl.BlockSpec((tm, w), lambda i: (i, 0))
    return pl.pallas_call(
        _inproj_kernel,
        grid=(t // tm,),
        in_specs=[tok(d), _const_spec((1, d)), _const_spec(w_in_bf.shape)],
        out_specs=[tok(w) for w in widths],
        out_shape=[jax.ShapeDtypeStruct((t, w), dt) for w, dt in zip(widths, dtypes)],
        compiler_params=_params("parallel"),
        name="inproj",
    )(x2, g.reshape(1, d), w_in_bf)


def _pool_kernel(p_ref, w_ref, scale_ref, o_ref, *, halves):
    x = p_ref[...]
    s = x.shape[0]
    t = lax.broadcasted_iota(jnp.int32, x.shape, 0)
    lane = lax.broadcasted_iota(jnp.int32, x.shape, 1)
    half = pl.program_id(1)

    def down(v, k):
        return jnp.where(t >= k, pltpu.roll(v, k, 0), 0.0)

    def up(v, k):
        return jnp.where(t < s - k, pltpu.roll(v, s - k, 0), 0.0)

    trail = {1: x}
    lead = {1: x}
    for m in (2, 4, 8):
        trail[m] = trail[m // 2] + down(trail[m // 2], m // 2)
        lead[m] = lead[m // 2] + up(lead[m // 2], m // 2)

    def window(hw):
        return down(trail[hw], 1) + lead[hw]

    def pick(lo_first, hi_first, lo_second, hi_second):
        first = jnp.where(lane < POOL_GROUP, lo_first, hi_first)
        second = jnp.where(lane < POOL_GROUP, lo_second, hi_second)
        return jnp.where(half == 0, first, second)

    hw = [w // 2 for w in POOL_WINDOWS]
    wsum = pick(window(hw[0]), window(hw[1]), window(hw[2]), window(hw[3]))
    hwv = pick(hw[0], hw[1], hw[2], hw[3])
    cnt = (jnp.minimum(t + hwv, s) - jnp.maximum(t - hwv, 0)).astype(F32)
    dlt = wsum / cnt - x
    y = jnp.dot(dlt.astype(BF16), w_ref[0], preferred_element_type=F32)
    o_ref[...] = y * scale_ref[0]


def _pool(p_pool, w_halves, scale_halves, batch, seq):
    t = p_pool.shape[0]
    halves = POOL_WIDTH // 128
    return pl.pallas_call(
        functools.partial(_pool_kernel, halves=halves),
        grid=(batch, halves),
        in_specs=[pl.BlockSpec((seq, 128), lambda b, h: (b, h)),
                  pl.BlockSpec((1, 128, 128), lambda b, h: (h, 0, 0)),
                  pl.BlockSpec((1, 1, 128), lambda b, h: (h, 0, 0))],
        out_specs=pl.BlockSpec((seq, 128), lambda b, h: (b, h)),
        out_shape=jax.ShapeDtypeStruct((t, POOL_WIDTH), F32),
        compiler_params=_params("parallel", "parallel"),
        name="pool",
    )(p_pool, w_halves, scale_halves)


def _rotary(x, cos, sin_signed, first_half):
    width = x.shape[-1]
    half = RET_DK // 2
    swapped = jnp.where(first_half, pltpu.roll(x, width - half, 1), pltpu.roll(x, half, 1))
    return x * cos + swapped * sin_signed


def _retention_kernel(lg_ref, q_ref, k_ref, v_ref, cos_ref, sin_ref, *rest, reverse):
    if reverse:
        fwd_ref, g_ref, o_ref, state_ref = rest
    else:
        o_ref, state_ref = rest
    c = q_ref.shape[0]

    @pl.when(pl.program_id(1) == 0)
    def _():
        state_ref[...] = jnp.zeros_like(state_ref)

    lane = lax.broadcasted_iota(jnp.int32, (c, RET_QK), 1)
    first_half = (lane % RET_DK) < (RET_DK // 2)
    cos = cos_ref[...]
    sin = sin_ref[...]
    q = _rotary(q_ref[...], cos, sin, first_half)
    k = _rotary(k_ref[...], cos, sin, first_half) * (RET_DK ** -0.5)
    v = v_ref[...]

    row = lax.broadcasted_iota(jnp.int32, (c, c), 0)
    col = lax.broadcasted_iota(jnp.int32, (c, c), 1)
    pos = lax.broadcasted_iota(jnp.int32, (c, 1), 0).astype(F32)
    if reverse:
        diff = col - row
        mask = diff > 0
        q_pow = c - pos
        k_pow = pos
    else:
        diff = row - col
        mask = diff >= 0
        q_pow = pos + 1.0
        k_pow = (c - 1) - pos
    dist = jnp.where(mask, diff, 0).astype(F32)
    direction = 1 if reverse else 0

    for h in range(RET_HEADS):
        lg = lg_ref[direction, h]
        qh = q[:, h * RET_DK:(h + 1) * RET_DK]
        kh = k[:, h * RET_DK:(h + 1) * RET_DK]
        vh = v[:, h * RET_DV:(h + 1) * RET_DV]
        decay = jnp.where(mask, jnp.exp(lg * dist), 0.0)
        scores = lax.dot_general(qh.astype(BF16), kh.astype(BF16), (((1,), (1,)), ((), ())),
                                 preferred_element_type=F32) * decay
        intra = jnp.dot(scores.astype(BF16), vh, preferred_element_type=F32)
        q_dec = qh * jnp.exp(lg * q_pow)
        k_dec = kh * jnp.exp(lg * k_pow)
        state = state_ref[h]
        cross = jnp.dot(q_dec.astype(BF16), state.astype(BF16), preferred_element_type=F32)
        chunk_kv = lax.dot_general(k_dec.astype(BF16), vh, (((0,), (0,)), ((), ())),
                                   preferred_element_type=F32)
        chunk_decay = jnp.exp(lg * jnp.full((1, RET_DV), float(c), F32))
        state_ref[h] = state * chunk_decay + chunk_kv
        out = intra + cross
        sl = slice(h * RET_DV, (h + 1) * RET_DV)
        if reverse:
            y = fwd_ref[:, sl] + out
            y = y * lax.rsqrt(jnp.mean(y * y, axis=-1, keepdims=True) + EPS)
            g = g_ref[:, sl]
            o_ref[:, sl] = (g / (1.0 + jnp.exp(-g))) * y
        else:
            o_ref[:, sl] = out


def _retention(lg, r_q, r_k, r_v, r_g, cos_t, sin_t, batch, seq):
    t = r_q.shape[0]
    c = RET_CHUNK
    n = seq // c
    scratch = [pltpu.VMEM((RET_HEADS, RET_DK, RET_DV), F32)]
    smem = pl.BlockSpec(memory_space=pltpu.SMEM)

    def specs(chunk_of):
        tok = lambda w: pl.BlockSpec((c, w), lambda b, i: (b * n + chunk_of(i), 0))
        tab = pl.BlockSpec((c, RET_QK), lambda b, i: (chunk_of(i), 0))
        return tok, tab

    tok, tab = specs(lambda i: i)
    fwd = pl.pallas_call(
        functools.partial(_retention_kernel, reverse=False),
        grid=(batch, n),
        in_specs=[smem, tok(RET_QK), tok(RET_QK), tok(RET_V), tab, tab],
        out_specs=tok(RET_V),
        out_shape=jax.ShapeDtypeStruct((t, RET_V), F32),
        scratch_shapes=scratch,
        compiler_params=_params("parallel", "arbitrary"),
        name="retention_fwd",
    )(lg, r_q, r_k, r_v, cos_t, sin_t)
    tok, tab = specs(lambda i: n - 1 - i)
    return pl.pallas_call(
        functools.partial(_retention_kernel, reverse=True),
        grid=(batch, n),
        in_specs=[smem, tok(RET_QK), tok(RET_QK), tok(RET_V), tab, tab, tok(RET_V), tok(RET_V)],
        out_specs=tok(RET_V),
        out_shape=jax.ShapeDtypeStruct((t, RET_V), F32),
        scratch_shapes=scratch,
        compiler_params=_params("parallel", "arbitrary"),
        name="retention_bwd",
    )(lg, r_q, r_k, r_v, cos_t, sin_t, fwd, r_g)


def _na_kernel(q_ref, k_ref, v_ref, bias_ref, o_ref, *, rows, kr):
    step = pl.program_id(1)
    rps = q_ref.shape[0] // GRID_W
    nkeys = kr * GRID_W
    lo = kr // 2
    hi_rows = rows - kr - lo

    def body(i, carry):
        r = step * rps + i
        r0 = jnp.clip(r - lo, 0, rows - kr)
        ver = jnp.where(r < lo, r, jnp.where(r > rows - kr + lo, r - (rows - kr), lo))
        kk = k_ref[pl.ds(pl.multiple_of(r0 * GRID_W, GRID_W), nkeys), :]
        vv = v_ref[pl.ds(pl.multiple_of(r0 * GRID_W, GRID_W), nkeys), :]
        qq = q_ref[pl.ds(pl.multiple_of(i * GRID_W, GRID_W), GRID_W), :]
        outs = []
        for h in range(NA_HEADS):
            sl = slice(h * NA_DH, (h + 1) * NA_DH)
            s = lax.dot_general(qq[:, sl], kk[:, sl], (((1,), (1,)), ((), ())),
                                preferred_element_type=F32) + bias_ref[ver, h]
            m = jnp.max(s, axis=-1, keepdims=True)
            p = jnp.exp(s - m)
            p = p / jnp.sum(p, axis=-1, keepdims=True)
            outs.append(jnp.dot(p.astype(BF16), vv[:, sl], preferred_element_type=F32))
        o_ref[pl.ds(pl.multiple_of(i * GRID_W, GRID_W), GRID_W), :] = jnp.concatenate(outs, axis=-1)
        return carry

    del hi_rows
    lax.fori_loop(0, rps, body, 0)


def _na_bias_table(rpb, rows, kr):
    lo = kr // 2
    reps = list(range(lo)) + [lo] + list(range(rows - kr + lo + 1, rows))
    c = np.arange(GRID_W)
    col_start = np.clip(c - NA_COLS // 2, 0, GRID_W - NA_COLS)
    col_in = (c[None, :] >= col_start[:, None]) & (c[None, :] < col_start[:, None] + NA_COLS)
    dj = np.clip(c[None, :] - c[:, None], -(NA_COLS - 1), NA_COLS - 1) + (NA_COLS - 1)
    pick = ((dj[None] == np.arange(2 * NA_COLS - 1)[:, None, None]) & col_in[None]).astype(np.float32)
    di0 = [int(np.clip(r - lo, 0, rows - kr)) - r + (NA_ROWS_MAX - 1) for r in reps]
    rpb_rows = jnp.stack([rpb[:, d0:d0 + kr, :] for d0 in di0]).astype(F32)
    bias = jnp.einsum('vhij,jck->vhcik', rpb_rows, jnp.asarray(pick), precision=lax.Precision.HIGHEST)
    bias = bias + jnp.asarray(np.where(col_in, 0.0, NEG).astype(np.float32))[None, None, :, None, :]
    return bias.reshape(len(reps), NA_HEADS, GRID_W, kr * GRID_W)


def _na(n_q, n_k, n_v, bias, batch, seq):
    t = n_q.shape[0]
    rows = seq // GRID_W
    kr = min(NA_ROWS_MAX, rows)
    rps = min(NA_ROWS_PER_STEP, rows)
    blk = rps * GRID_W
    steps = rows // rps
    return pl.pallas_call(
        functools.partial(_na_kernel, rows=rows, kr=kr),
        grid=(batch, steps),
        in_specs=[pl.BlockSpec((blk, NA_WIDTH), lambda b, i: (b * steps + i, 0)),
                  pl.BlockSpec((seq, NA_WIDTH), lambda b, i: (b, 0)),
                  pl.BlockSpec((seq, NA_WIDTH), lambda b, i: (b, 0)),
                  _const_spec(bias.shape)],
        out_specs=pl.BlockSpec((blk, NA_WIDTH), lambda b, i: (b * steps + i, 0)),
        out_shape=jax.ShapeDtypeStruct((t, NA_WIDTH), F32),
        compiler_params=_params("parallel", "arbitrary"),
        name="natten",
    )(n_q, n_k, n_v, bias)


def _merge_kernel(x_ref, yp_ref, yr_ref, yn_ref, gate_ref, wp_ref, wr_ref, wn_ref, wo_ref,
                  gf_ref, wq_ref, keys_ref, h_ref, hnt_ref, st_ref):
    d = x_ref.shape[-1]
    merged = None
    for b, (y_ref, w_ref) in enumerate(((yp_ref, wp_ref), (yr_ref, wr_ref), (yn_ref, wn_ref))):
        gate = 1.0 / (1.0 + jnp.exp(-gate_ref[:, b * d:(b + 1) * d]))
        term = gate * jnp.dot(y_ref[...].astype(BF16), w_ref[...], preferred_element_type=F32)
        merged = term if merged is None else merged + term
    h = x_ref[...] + jnp.dot(merged.astype(BF16), wo_ref[...], preferred_element_type=F32)
    h_ref[...] = h
    hn = h * lax.rsqrt(jnp.mean(h * h, axis=-1, keepdims=True) + EPS) * gf_ref[...]
    hnb = hn.astype(BF16)
    hnt_ref[...] = hn.T.astype(BF16)
    q = jnp.dot(hnb, wq_ref[...], preferred_element_type=F32).astype(BF16)
    for hh in range(PEER_HEADS):
        for p in range(2):
            col = (hh * 2 + p) * PEER_DK
            st_ref[hh, p] = lax.dot_general(keys_ref[p], q[:, col:col + PEER_DK],
                                            (((1,), (1,)), ((), ())), preferred_element_type=F32)


def _merge(x2, y_pool, y_ret, y_na, gates, wp, wr, wn, wo, g_ffn, wq, keys):
    t, d = x2.shape
    tm = MERGE_TM
    tok = lambda w: pl.BlockSpec((tm, w), lambda i: (i, 0))
    return pl.pallas_call(
        _merge_kernel,
        grid=(t // tm,),
        in_specs=[tok(d), tok(POOL_WIDTH), tok(RET_V), tok(NA_WIDTH), tok(N_BRANCH * d),
                  _const_spec(wp.shape), _const_spec(wr.shape), _const_spec(wn.shape), _const_spec(wo.shape),
                  _const_spec((1, d)), _const_spec(wq.shape), _const_spec(keys.shape)],
        out_specs=[tok(d),
                   pl.BlockSpec((d, tm), lambda i: (0, i)),
                   pl.BlockSpec((PEER_HEADS, 2, PEER_KEYS, tm), lambda i: (0, 0, 0, i))],
        out_shape=[jax.ShapeDtypeStruct((t, d), F32),
                   jax.ShapeDtypeStruct((d, t), BF16),
                   jax.ShapeDtypeStruct((PEER_HEADS, 2, PEER_KEYS, t), F32)],
        compiler_params=_params("parallel"),
        name="merge",
    )(x2, y_pool, y_ret, y_na, gates, wp, wr, wn, wo, g_ffn.reshape(1, d), wq, keys)


def _topk_exact_tile(s_ref, r2_ref, e2_ref, cc_ref, e1_ref, ds):
    nk = s_ref.shape[2]
    k_top = PEER_TOPK
    w = ds.stop - ds.start
    key = lax.broadcasted_iota(jnp.int32, (nk, w), 0)
    arow = lax.broadcasted_iota(jnp.int32, (k_top, w), 0)
    neg_inf = -jnp.inf

    def top_ranks(s):
        def body(a, carry):
            s, rank, vals = carry
            m = jnp.max(s, axis=0, keepdims=True)
            idx = jnp.min(jnp.where(s == m, key, nk), axis=0, keepdims=True)
            hit = key == idx
            return (jnp.where(hit, neg_inf, s), jnp.where(hit, a, rank), jnp.where(arow == a, m, vals))
        init = (s, jnp.full((nk, w), k_top, jnp.int32), jnp.zeros((k_top, w), F32))
        _, rank, vals = lax.fori_loop(0, k_top, body, init)
        return rank, vals

    s1 = s_ref[0, 0, :, ds]
    s2 = s_ref[0, 1, :, ds]
    r1, v1 = top_ranks(s1)
    r2, v2 = top_ranks(s2)
    top = v1[0:1] + v2[0:1]

    def merge(_, carry):
        cnt, front, z = carry
        m = jnp.max(front, axis=0, keepdims=True)
        a_star = jnp.min(jnp.where(front == m, arow, k_top), axis=0, keepdims=True)
        hit = arow == a_star
        cnt = jnp.where(hit, cnt + 1, cnt)
        nxt = jnp.sum(jnp.where(hit, cnt, 0), axis=0, keepdims=True)
        v2_next = jnp.sum(jnp.where(arow == nxt, v2, 0.0), axis=0, keepdims=True)
        cand = jnp.where(nxt < k_top, v1 + v2_next, neg_inf)
        return cnt, jnp.where(hit, cand, front), z + jnp.exp(m - top)
    init = (jnp.zeros((k_top, w), jnp.int32), v1 + v2[0:1], jnp.zeros((1, w), F32))
    cnt, _, z = lax.fori_loop(0, k_top, merge, init)

    cc = jnp.zeros((nk, w), jnp.int32)
    for a in range(k_top):
        cc = jnp.where(r1 == a, cnt[a:a + 1], cc)
    r2_ref[0, :, ds] = r2.astype(F32).astype(BF16)
    cc_ref[0, :, ds] = cc.astype(F32)
    e2_ref[0, :, ds] = jnp.exp(s2 - v2[0:1]).astype(BF16)
    e1_ref[0, :, ds] = jnp.exp(s1 - v1[0:1]) * (1.0 / z)


def _topk_kernel(s_ref, r2_ref, e2_ref, cc_ref, e1_ref, v1_ref, v2_ref):
    nk = s_ref.shape[2]
    tm = s_ref.shape[3]
    k_top = PEER_TOPK
    w = TOPK_SUB
    subs = [slice(j * w, (j + 1) * w) for j in range(tm // w)]
    neg_inf = -jnp.inf
    no_tie_rank_sum = float(sum(range(k_top)) + k_top * (nk - k_top))
    mark_base, mark_step = 2.0 ** 100, 2.0 ** 96
    lowest_score = -(2.0 ** 99)

    redo = jnp.zeros((1, w), F32)
    for j, ds in enumerate(subs):
        for p, (rank_ref, vals_ref) in enumerate(((cc_ref, v1_ref), (r2_ref, v2_ref))):
            s = s_ref[0, p, :, ds]
            floor_ok = jnp.min(s, axis=0, keepdims=True) > lowest_score
            for a in range(k_top):
                m = jnp.max(s, axis=0, keepdims=True)
                s = jnp.where(s == m, -(mark_base + a * mark_step), s)
                vals_ref[a, j:j + 1, :] = m
            rank = jnp.where(s < lowest_score, s * (-1.0 / mark_step) - mark_base / mark_step, float(k_top))
            exact = floor_ok & (jnp.sum(rank, axis=0, keepdims=True) == no_tie_rank_sum)
            redo = redo + jnp.where(exact, 0.0, 1.0)
            rank_ref[0, :, ds] = rank.astype(rank_ref.dtype)

    v1 = v1_ref[...]
    v2 = v2_ref[...]
    arow = lax.broadcasted_iota(jnp.int32, v1.shape, 0).astype(F32)
    top = v1[0:1] + v2[0:1]
    cnt = jnp.zeros(v1.shape, F32)
    front = v1 + v2[0:1]
    z = jnp.zeros(top.shape, F32)
    for _ in range(k_top):
        m = jnp.max(front, axis=0, keepdims=True)
        a_star = jnp.min(jnp.where(front == m, arow, float(k_top)), axis=0, keepdims=True)
        hit = arow == a_star
        cnt = jnp.where(hit, cnt + 1.0, cnt)
        nxt = jnp.sum(jnp.where(hit, cnt, 0.0), axis=0, keepdims=True)
        v2_next = jnp.sum(jnp.where(arow == nxt, v2, 0.0), axis=0, keepdims=True)
        cand = jnp.where(nxt < float(k_top), v1 + v2_next, neg_inf)
        front = jnp.where(hit, cand, front)
        z = z + jnp.exp(m - top)
    inv_z = 1.0 / z

    for j, ds in enumerate(subs):
        r1 = cc_ref[0, :, ds]
        cc = jnp.zeros((nk, w), F32)
        for a in range(k_top):
            cc = jnp.where(r1 == float(a), cnt[a, j:j + 1, :], cc)
        cc_ref[0, :, ds] = cc
        e1_ref[0, :, ds] = jnp.exp(s_ref[0, 0, :, ds] - v1[0, j:j + 1, :]) * inv_z[0, j:j + 1, :]
        e2_ref[0, :, ds] = jnp.exp(s_ref[0, 1, :, ds] - v2[0, j:j + 1, :]).astype(BF16)

    @pl.when(jnp.sum(redo) > 0.0)
    def _():
        for ds in subs:
            _topk_exact_tile(s_ref, r2_ref, e2_ref, cc_ref, e1_ref, ds)


def _topk(st):
    heads, _, nk, t = st.shape
    tm = TOPK_TM
    out = pl.BlockSpec((1, nk, tm), lambda i, h: (h, 0, i))
    return pl.pallas_call(
        _topk_kernel,
        grid=(t // tm, heads),
        in_specs=[pl.BlockSpec((1, 2, nk, tm), lambda i, h: (h, 0, 0, i))],
        out_specs=[out] * 4,
        out_shape=[jax.ShapeDtypeStruct((heads, nk, t), dt) for dt in (BF16, BF16, F32, F32)],
        scratch_shapes=[pltpu.VMEM((PEER_TOPK, tm // TOPK_SUB, TOPK_SUB), F32)] * 2,
        compiler_params=_params("parallel", "parallel"),
        name="peer_topk",
    )(st)


def _peer_kernel(hnt_ref, u_ref, vt_ref, r2_in_ref, e2_in_ref, cc_ref, e1_ref, h_ref, o_ref,
                 acc_ref, a_ref, ht_ref, r2_ref, e2_ref, ccs_ref, e1s_ref):
    e = pl.program_id(1)
    te = u_ref.shape[0]
    tm = hnt_ref.shape[1]
    nk = r2_ref.shape[2]
    rows_per_step = te // nk
    lane_tiles = tm // PEER_LANES
    tiles = [slice(lt * PEER_LANES, (lt + 1) * PEER_LANES) for lt in range(lane_tiles)]

    @pl.when(e == 0)
    def _():
        acc_ref[...] = jnp.zeros_like(acc_ref)
        for hh in range(PEER_HEADS):
            for lt in range(lane_tiles):
                lanes = slice(lt * PEER_LANES, (lt + 1) * PEER_LANES)
                r2_ref[hh, lt] = r2_in_ref[hh, :, lanes]
                e2_ref[hh, lt] = e2_in_ref[hh, :, lanes]

    a_t = jnp.dot(u_ref[...], hnt_ref[...], preferred_element_type=F32)
    for lt, lanes in enumerate(tiles):
        a_ref[lt] = a_t[:, lanes]
    first_row = pl.multiple_of(e * rows_per_step, rows_per_step)
    for hh in range(PEER_HEADS):
        cc_rows = cc_ref[hh, pl.ds(first_row, rows_per_step), :]
        e1_rows = e1_ref[hh, pl.ds(first_row, rows_per_step), :]
        for ii in range(rows_per_step):
            ccs_ref[ii, hh, 0:1, :] = cc_rows[ii:ii + 1, :]
            e1s_ref[ii, hh, 0:1, :] = e1_rows[ii:ii + 1, :]

    def gate_rows(ii, carry):
        rows = pl.ds(pl.multiple_of(ii * nk, nk), nk)
        for lt, lanes in enumerate(tiles):
            gate = None
            for hh in range(PEER_HEADS):
                cc = jnp.broadcast_to(ccs_ref[ii, hh, 0:1, lanes], (nk, PEER_LANES)).astype(BF16)
                e1 = jnp.broadcast_to(e1s_ref[ii, hh, 0:1, lanes], (nk, PEER_LANES)).astype(BF16)
                term = jnp.where(r2_ref[hh, lt] < cc, e2_ref[hh, lt], jnp.zeros((), BF16)) * e1
                gate = term if gate is None else gate + term
            a = a_ref[lt, rows, :]
            act = 0.5 * a * (1.0 + lax.erf(a * np.float32(1.0 / np.sqrt(2.0))))
            ht_ref[lt, rows, :] = act.astype(BF16) * gate
        return carry

    lax.fori_loop(0, rows_per_step, gate_rows, 0)
    h_t = jnp.concatenate([ht_ref[lt] for lt in range(lane_tiles)], axis=1)
    acc_ref[...] += jnp.dot(vt_ref[...], h_t, preferred_element_type=F32)

    @pl.when(e == pl.num_programs(1) - 1)
    def _():
        o_ref[...] = h_ref[...] + acc_ref[...].T


def _peer(hnt, u_bf, vt_bf, r2, e2, cc, e1, h2):
    d, t = hnt.shape
    n_exp = u_bf.shape[0]
    heads, nk, _ = r2.shape
    tm, te = PEER_TM, PEER_TE
    lt = tm // PEER_LANES
    tab = pl.BlockSpec((heads, nk, tm), lambda i, e: (0, 0, i))
    return pl.pallas_call(
        _peer_kernel,
        grid=(t // tm, n_exp // te),
        in_specs=[pl.BlockSpec((d, tm), lambda i, e: (0, i)),
                  pl.BlockSpec((te, d), lambda i, e: (e, 0)),
                  pl.BlockSpec((d, te), lambda i, e: (0, e)),
                  tab, tab, tab, tab,
                  pl.BlockSpec((tm, d), lambda i, e: (i, 0))],
        out_specs=pl.BlockSpec((tm, d), lambda i, e: (i, 0)),
        out_shape=jax.ShapeDtypeStruct((t, d), F32),
        scratch_shapes=[pltpu.VMEM((d, tm), F32),
                        pltpu.VMEM((lt, te, PEER_LANES), F32), pltpu.VMEM((lt, te, PEER_LANES), BF16),
                        pltpu.VMEM((heads, lt, nk, PEER_LANES), BF16),
                        pltpu.VMEM((heads, lt, nk, PEER_LANES), BF16),
                        pltpu.VMEM((te // nk, heads, 8, tm), F32), pltpu.VMEM((te // nk, heads, 8, tm), F32)],
        compiler_params=_params("parallel", "arbitrary"),
        name="peer_experts",
    )(hnt, u_bf, vt_bf, r2, e2, cc, e1, h2)


def _rmsnorm_kernel(x_ref, g_ref, o_ref):
    x = x_ref[...]
    o_ref[...] = x * lax.rsqrt(jnp.mean(x * x, axis=-1, keepdims=True) + EPS) * g_ref[...]


def _rmsnorm(x2, g):
    t, d = x2.shape
    tm = 512
    return pl.pallas_call(
        _rmsnorm_kernel,
        grid=(t // tm,),
        in_specs=[pl.BlockSpec((tm, d), lambda i: (i, 0)), _const_spec((1, d))],
        out_specs=pl.BlockSpec((tm, d), lambda i: (i, 0)),
        out_shape=jax.ShapeDtypeStruct((t, d), F32),
        compiler_params=_params("parallel"),
        name="final_norm",
    )(x2, g.reshape(1, d))


def _rotary_tables(seq):
    half = RET_DK // 2
    inv = 1.0 / (ROPE_BASE ** jnp.linspace(0.0, 1.0, half, dtype=F32))
    ang = jnp.arange(seq, dtype=F32)[:, None] * inv[None, :]
    cos, sin = jnp.cos(ang), jnp.sin(ang)
    cos_t = jnp.tile(jnp.concatenate([cos, cos], axis=-1), (1, RET_HEADS))
    sin_t = jnp.tile(jnp.concatenate([-sin, sin], axis=-1), (1, RET_HEADS))
    return cos_t, sin_t


def _pool_weights(pool_w, pool_scale):
    halves = POOL_WIDTH // 128
    per = 128 // POOL_GROUP
    blocks = []
    for hf in range(halves):
        blk = jnp.zeros((128, 128), F32)
        for g in range(per):
            sl = slice(g * POOL_GROUP, (g + 1) * POOL_GROUP)
            blk = blk.at[sl, sl].set(pool_w[hf * per + g])
        blocks.append(blk)
    return jnp.stack(blocks).astype(BF16), pool_scale.reshape(halves, 1, 128)


def kernel(x, norm_mix, w_in, pool_w, pool_scale, ret_decay, na_rpb, w_br_pool, w_br_ret, w_br_na,
           w_out, norm_ffn, peer_w_query, peer_sub_keys, peer_u, peer_v, norm_final):
    batch, seq, d = x.shape
    depth = w_in.shape[0]
    rows = seq // GRID_W
    kr = min(NA_ROWS_MAX, rows)
    cos_t, sin_t = _rotary_tables(seq)
    x2 = x.reshape(batch * seq, d)
    for l in range(depth):
        pool, r_q, r_k, r_v, r_g, n_q, n_k, n_v, gates = _inproj(x2, norm_mix[l], w_in[l].astype(BF16))
        w_halves, scale_halves = _pool_weights(pool_w[l], pool_scale[l])
        y_pool = _pool(pool, w_halves, scale_halves, batch, seq)
        lg = jax.nn.log_sigmoid(ret_decay[l].astype(F32))
        y_ret = _retention(lg, r_q, r_k, r_v, r_g, cos_t, sin_t, batch, seq)
        y_na = _na(n_q, n_k, n_v, _na_bias_table(na_rpb[l], rows, kr), batch, seq)
        h2, hnt, st = _merge(x2, y_pool, y_ret, y_na, gates,
                             w_br_pool[l].astype(BF16), w_br_ret[l].astype(BF16), w_br_na[l].astype(BF16),
                             w_out[l].astype(BF16), norm_ffn[l], peer_w_query[l].astype(BF16),
                             peer_sub_keys[l].astype(BF16))
        r2, e2, cc, e1 = _topk(st)
        x2 = _peer(hnt, peer_u[l].astype(BF16), peer_v[l].T.astype(BF16), r2, e2, cc, e1, h2)
    return _rmsnorm(x2, norm_final).reshape(batch, seq, d)
```

```python
import functools

import numpy as np
import jax
import jax.numpy as jnp
from jax import lax
from jax.experimental import pallas as pl
from jax.experimental.pallas import tpu as pltpu

F32 = jnp.float32
BF16 = jnp.bfloat16

EPS = 1e-6
GRID_W = 64
NEG = -1e30

POOL_WINDOWS = (2, 4, 8, 16)
POOL_GROUP = 64
POOL_WIDTH = POOL_GROUP * len(POOL_WINDOWS)

RET_HEADS = 4
RET_DK = 64
RET_DV = 128
RET_QK = RET_HEADS * RET_DK
RET_V = RET_HEADS * RET_DV
ROPE_BASE = 10000.0

NA_HEADS = 4
NA_DH = 64
NA_WIDTH = NA_HEADS * NA_DH
NA_ROWS_MAX = 8
NA_COLS = 16

N_BRANCH = 3

PEER_KEYS = 128
PEER_HEADS = 8
PEER_DK = 128
PEER_TOPK = 16

V7X_VMEM_LIMIT_BYTES = 56 * 1024 * 1024
V7X_MXU_DIM = 256

INPROJ_TM = 256
RET_CHUNK = 256
NA_ROWS_PER_STEP = 8
MERGE_TM = 256
TOPK_TM = 1024
TOPK_SUB = 128
PEER_TM = 512
PEER_TE = 1024
PEER_LANES = 128


def _params(*sem):
    return pltpu.CompilerParams(dimension_semantics=sem, vmem_limit_bytes=V7X_VMEM_LIMIT_BYTES)


def _const_spec(shape):
    nd = len(shape)
    return pl.BlockSpec(shape, lambda *_: (0,) * nd)


def _inproj_kernel(x_ref, g_ref, w_ref, pool_ref, rq_ref, rk_ref, rv_ref, rg_ref,
                   nq_ref, nk_ref, nv_ref, gate_ref):
    x = x_ref[...]
    xn = x * lax.rsqrt(jnp.mean(x * x, axis=-1, keepdims=True) + EPS) * g_ref[...]
    xb = xn.astype(BF16)
    off = 0
    for ref, scale in ((pool_ref, None), (rq_ref, None), (rk_ref, None), (rv_ref, None), (rg_ref, None),
                       (nq_ref, NA_DH ** -0.5), (nk_ref, None), (nv_ref, None)):
        width = ref.shape[-1]
        y = jnp.dot(xb, w_ref[:, off:off + width], preferred_element_type=F32)
        if scale is not None:
            y = y * scale
        ref[...] = y.astype(ref.dtype)
        off += width
    d = x.shape[-1]
    for b in range(N_BRANCH):
        gate_ref[:, b * d:(b + 1) * d] = jnp.dot(
            xb, w_ref[:, off + b * d:off + (b + 1) * d], preferred_element_type=F32)


def _inproj(x2, g, w_in_bf):
    t, d = x2.shape
    tm = INPROJ_TM
    widths = (POOL_WIDTH, RET_QK, RET_QK, RET_V, RET_V, NA_WIDTH, NA_WIDTH, NA_WIDTH, N_BRANCH * d)
    dtypes = (F32, F32, F32, BF16, F32, BF16, BF16, BF16, F32)
    tok = lambda w: pl.BlockSpec((tm, w), lambda i: (i, 0))
    return pl.pallas_call(
        _inproj_kernel,
        grid=(t // tm,),
        in_specs=[tok(d), _const_spec((1, d)), _const_spec(w_in_bf.shape)],
        out_specs=[tok(w) for w in widths],
        out_shape=[jax.ShapeDtypeStruct((t, w), dt) for w, dt in zip(widths, dtypes)],
        compiler_params=_params("parallel"),
        name="inproj",
    )(x2, g.reshape(1, d), w_in_bf)


def _pool_kernel(p_ref, w_ref, scale_ref, o_ref, *, halves):
    x = p_ref[...]
    s = x.shape[0]
    t = lax.broadcasted_iota(jnp.int32, x.shape, 0)
    lane = lax.broadcasted_iota(jnp.int32, x.shape, 1)
    half = pl.program_id(1)

    def down(v, k):
        return jnp.where(t >= k, pltpu.roll(v, k, 0), 0.0)

    def up(v, k):
        return jnp.where(t < s - k, pltpu.roll(v, s - k, 0), 0.0)

    trail = {1: x}
    lead = {1: x}
    for m in (2, 4, 8):
        trail[m] = trail[m // 2] + down(trail[m // 2], m // 2)
        lead[m] = lead[m // 2] + up(lead[m // 2], m // 2)

    def window(hw):
        return down(trail[hw], 1) + lead[hw]

    def pick(lo_first, hi_first, lo_second, hi_second):
        first = jnp.where(lane < POOL_GROUP, lo_first, hi_first)
        second = jnp.where(lane < POOL_GROUP, lo_second, hi_second)
        return jnp.where(half == 0, first, second)

    hw = [w // 2 for w in POOL_WINDOWS]
    wsum = pick(window(hw[0]), window(hw[1]), window(hw[2]), window(hw[3]))
    hwv = pick(hw[0], hw[1], hw[2], hw[3])
    cnt = (jnp.minimum(t + hwv, s) - jnp.maximum(t - hwv, 0)).astype(F32)
    dlt = wsum / cnt - x
    y = jnp.dot(dlt.astype(BF16), w_ref[0], preferred_element_type=F32)
    o_ref[...] = y * scale_ref[0]


def _pool(p_pool, w_halves, scale_halves, batch, seq):
    t = p_pool.shape[0]
    halves = POOL_WIDTH // 128
    return pl.pallas_call(
        functools.partial(_pool_kernel, halves=halves),
        grid=(batch, halves),
        in_specs=[pl.BlockSpec((seq, 128), lambda b, h: (b, h)),
                  pl.BlockSpec((1, 128, 128), lambda b, h: (h, 0, 0)),
                  pl.BlockSpec((1, 1, 128), lambda b, h: (h, 0, 0))],
        out_specs=pl.BlockSpec((seq, 128), lambda b, h: (b, h)),
        out_shape=jax.ShapeDtypeStruct((t, POOL_WIDTH), F32),
        compiler_params=_params("parallel", "parallel"),
        name="pool",
    )(p_pool, w_halves, scale_halves)


def _rotary(x, cos, sin_signed, first_half):
    width = x.shape[-1]
    half = RET_DK // 2
    swapped = jnp.where(first_half, pltpu.roll(x, width - half, 1), pltpu.roll(x, half, 1))
    return x * cos + swapped * sin_signed


def _retention_kernel(lg_ref, q_ref, k_ref, v_ref, cos_ref, sin_ref, *rest, reverse):
    if reverse:
        fwd_ref, g_ref, o_ref, state_ref = rest
    else:
        o_ref, state_ref = rest
    c = q_ref.shape[0]

    @pl.when(pl.program_id(1) == 0)
    def _():
        state_ref[...] = jnp.zeros_like(state_ref)

    lane = lax.broadcasted_iota(jnp.int32, (c, RET_QK), 1)
    first_half = (lane % RET_DK) < (RET_DK // 2)
    cos = cos_ref[...]
    sin = sin_ref[...]
    q = _rotary(q_ref[...], cos, sin, first_half)
    k = _rotary(k_ref[...], cos, sin, first_half) * (RET_DK ** -0.5)
    v = v_ref[...]

    row = lax.broadcasted_iota(jnp.int32, (c, c), 0)
    col = lax.broadcasted_iota(jnp.int32, (c, c), 1)
    pos = lax.broadcasted_iota(jnp.int32, (c, 1), 0).astype(F32)
    if reverse:
        diff = col - row
        mask = diff > 0
        q_pow = c - pos
        k_pow = pos
    else:
        diff = row - col
        mask = diff >= 0
        q_pow = pos + 1.0
        k_pow = (c - 1) - pos
    dist = jnp.where(mask, diff, 0).astype(F32)
    direction = 1 if reverse else 0

    for h in range(RET_HEADS):
        lg = lg_ref[direction, h]
        qh = q[:, h * RET_DK:(h + 1) * RET_DK]
        kh = k[:, h * RET_DK:(h + 1) * RET_DK]
        vh = v[:, h * RET_DV:(h + 1) * RET_DV]
        decay = jnp.where(mask, jnp.exp(lg * dist), 0.0)
        scores = lax.dot_general(qh.astype(BF16), kh.astype(BF16), (((1,), (1,)), ((), ())),
                                 preferred_element_type=F32) * decay
        intra = jnp.dot(scores.astype(BF16), vh, preferred_element_type=F32)
        q_dec = qh * jnp.exp(lg * q_pow)
        k_dec = kh * jnp.exp(lg * k_pow)
        state = state_ref[h]
        cross = jnp.dot(q_dec.astype(BF16), state.astype(BF16), preferred_element_type=F32)
        chunk_kv = lax.dot_general(k_dec.astype(BF16), vh, (((0,), (0,)), ((), ())),
                                   preferred_element_type=F32)
        chunk_decay = jnp.exp(lg * jnp.full((1, RET_DV), float(c), F32))
        state_ref[h] = state * chunk_decay + chunk_kv
        out = intra + cross
        sl = slice(h * RET_DV, (h + 1) * RET_DV)
        if reverse:
            y = fwd_ref[:, sl] + out
            y = y * lax.rsqrt(jnp.mean(y * y, axis=-1, keepdims=True) + EPS)
            g = g_ref[:, sl]
            o_ref[:, sl] = (g / (1.0 + jnp.exp(-g))) * y
        else:
            o_ref[:, sl] = out


def _retention(lg, r_q, r_k, r_v, r_g, cos_t, sin_t, batch, seq):
    t = r_q.shape[0]
    c = RET_CHUNK
    n = seq // c
    scratch = [pltpu.VMEM((RET_HEADS, RET_DK, RET_DV), F32)]
    smem = pl.BlockSpec(memory_space=pltpu.SMEM)

    def specs(chunk_of):
        tok = lambda w: pl.BlockSpec((c, w), lambda b, i: (b * n + chunk_of(i), 0))
        tab = pl.BlockSpec((c, RET_QK), lambda b, i: (chunk_of(i), 0))
        return tok, tab

    tok, tab = specs(lambda i: i)
    fwd = pl.pallas_call(
        functools.partial(_retention_kernel, reverse=False),
        grid=(batch, n),
        in_specs=[smem, tok(RET_QK), tok(RET_QK), tok(RET_V), tab, tab],
        out_specs=tok(RET_V),
        out_shape=jax.ShapeDtypeStruct((t, RET_V), F32),
        scratch_shapes=scratch,
        compiler_params=_params("parallel", "arbitrary"),
        name="retention_fwd",
    )(lg, r_q, r_k, r_v, cos_t, sin_t)
    tok, tab = specs(lambda i: n - 1 - i)
    return pl.pallas_call(
        functools.partial(_retention_kernel, reverse=True),
        grid=(batch, n),
        in_specs=[smem, tok(RET_QK), tok(RET_QK), tok(RET_V), tab, tab, tok(RET_V), tok(RET_V)],
        out_specs=tok(RET_V),
        out_shape=jax.ShapeDtypeStruct((t, RET_V), F32),
        scratch_shapes=scratch,
        compiler_params=_params("parallel", "arbitrary"),
        name="retention_bwd",
    )(lg, r_q, r_k, r_v, cos_t, sin_t, fwd, r_g)


def _na_kernel(q_ref, k_ref, v_ref, bias_ref, o_ref, *, rows, kr):
    step = pl.program_id(1)
    rps = q_ref.shape[0] // GRID_W
    nkeys = kr * GRID_W
    lo = kr // 2
    hi_rows = rows - kr - lo

    def body(i, carry):
        r = step * rps + i
        r0 = jnp.clip(r - lo, 0, rows - kr)
        ver = jnp.where(r < lo, r, jnp.where(r > rows - kr + lo, r - (rows - kr), lo))
        kk = k_ref[pl.ds(pl.multiple_of(r0 * GRID_W, GRID_W), nkeys), :]
        vv = v_ref[pl.ds(pl.multiple_of(r0 * GRID_W, GRID_W), nkeys), :]
        qq = q_ref[pl.ds(pl.multiple_of(i * GRID_W, GRID_W), GRID_W), :]
        outs = []
        for h in range(NA_HEADS):
            sl = slice(h * NA_DH, (h + 1) * NA_DH)
            s = lax.dot_general(qq[:, sl], kk[:, sl], (((1,), (1,)), ((), ())),
                                preferred_element_type=F32) + bias_ref[ver, h]
            m = jnp.max(s, axis=-1, keepdims=True)
            p = jnp.exp(s - m)
            p = p / jnp.sum(p, axis=-1, keepdims=True)
            outs.append(jnp.dot(p.astype(BF16), vv[:, sl], preferred_element_type=F32))
        o_ref[pl.ds(pl.multiple_of(i * GRID_W, GRID_W), GRID_W), :] = jnp.concatenate(outs, axis=-1)
        return carry

    del hi_rows
    lax.fori_loop(0, rps, body, 0)


def _na_bias_table(rpb, rows, kr):
    lo = kr // 2
    reps = list(range(lo)) + [lo] + list(range(rows - kr + lo + 1, rows))
    c = np.arange(GRID_W)
    col_start = np.clip(c - NA_COLS // 2, 0, GRID_W - NA_COLS)
    col_in = (c[None, :] >= col_start[:, None]) & (c[None, :] < col_start[:, None] + NA_COLS)
    dj = np.clip(c[None, :] - c[:, None], -(NA_COLS - 1), NA_COLS - 1) + (NA_COLS - 1)
    pick = ((dj[None] == np.arange(2 * NA_COLS - 1)[:, None, None]) & col_in[None]).astype(np.float32)
    di0 = [int(np.clip(r - lo, 0, rows - kr)) - r + (NA_ROWS_MAX - 1) for r in reps]
    rpb_rows = jnp.stack([rpb[:, d0:d0 + kr, :] for d0 in di0]).astype(F32)
    bias = jnp.einsum('vhij,jck->vhcik', rpb_rows, jnp.asarray(pick), precision=lax.Precision.HIGHEST)
    bias = bias + jnp.asarray(np.where(col_in, 0.0, NEG).astype(np.float32))[None, None, :, None, :]
    return bias.reshape(len(reps), NA_HEADS, GRID_W, kr * GRID_W)


def _na(n_q, n_k, n_v, bias, batch, seq):
    t = n_q.shape[0]
    rows = seq // GRID_W
    kr = min(NA_ROWS_MAX, rows)
    rps = min(NA_ROWS_PER_STEP, rows)
    blk = rps * GRID_W
    steps = rows // rps
    return pl.pallas_call(
        functools.partial(_na_kernel, rows=rows, kr=kr),
        grid=(batch, steps),
        in_specs=[pl.BlockSpec((blk, NA_WIDTH), lambda b, i: (b * steps + i, 0)),
                  pl.BlockSpec((seq, NA_WIDTH), lambda b, i: (b, 0)),
                  pl.BlockSpec((seq, NA_WIDTH), lambda b, i: (b, 0)),
                  _const_spec(bias.shape)],
        out_specs=pl.BlockSpec((blk, NA_WIDTH), lambda b, i: (b * steps + i, 0)),
        out_shape=jax.ShapeDtypeStruct((t, NA_WIDTH), F32),
        compiler_params=_params("parallel", "arbitrary"),
        name="natten",
    )(n_q, n_k, n_v, bias)


def _merge_kernel(x_ref, yp_ref, yr_ref, yn_ref, gate_ref, wp_ref, wr_ref, wn_ref, wo_ref,
                  gf_ref, wq_ref, keys_ref, h_ref, hnt_ref, st_ref):
    d = x_ref.shape[-1]
    merged = None
    for b, (y_ref, w_ref) in enumerate(((yp_ref, wp_ref), (yr_ref, wr_ref), (yn_ref, wn_ref))):
        gate = 1.0 / (1.0 + jnp.exp(-gate_ref[:, b * d:(b + 1) * d]))
        term = gate * jnp.dot(y_ref[...].astype(BF16), w_ref[...], preferred_element_type=F32)
        merged = term if merged is None else merged + term
    h = x_ref[...] + jnp.dot(merged.astype(BF16), wo_ref[...], preferred_element_type=F32)
    h_ref[...] = h
    hn = h * lax.rsqrt(jnp.mean(h * h, axis=-1, keepdims=True) + EPS) * gf_ref[...]
    hnb = hn.astype(BF16)
    hnt_ref[...] = hn.T.astype(BF16)
    q = jnp.dot(hnb, wq_ref[...], preferred_element_type=F32).astype(BF16)
    for hh in range(PEER_HEADS):
        for p in range(2):
            col = (hh * 2 + p) * PEER_DK
            st_ref[hh, p] = lax.dot_general(keys_ref[p], q[:, col:col + PEER_DK],
                                            (((1,), (1,)), ((), ())), preferred_element_type=F32)


def _merge(x2, y_pool, y_ret, y_na, gates, wp, wr, wn, wo, g_ffn, wq, keys):
    t, d = x2.shape
    tm = MERGE_TM
    tok = lambda w: pl.BlockSpec((tm, w), lambda i: (i, 0))
    return pl.pallas_call(
        _merge_kernel,
        grid=(t // tm,),
        in_specs=[tok(d), tok(POOL_WIDTH), tok(RET_V), tok(NA_WIDTH), tok(N_BRANCH * d),
                  _const_spec(wp.shape), _const_spec(wr.shape), _const_spec(wn.shape), _const_spec(wo.shape),
                  _const_spec((1, d)), _const_spec(wq.shape), _const_spec(keys.shape)],
        out_specs=[tok(d),
                   pl.BlockSpec((d, tm), lambda i: (0, i)),
                   pl.BlockSpec((PEER_HEADS, 2, PEER_KEYS, tm), lambda i: (0, 0, 0, i))],
        out_shape=[jax.ShapeDtypeStruct((t, d), F32),
                   jax.ShapeDtypeStruct((d, t), BF16),
                   jax.ShapeDtypeStruct((PEER_HEADS, 2, PEER_KEYS, t), F32)],
        compiler_params=_params("parallel"),
        name="merge",
    )(x2, y_pool, y_ret, y_na, gates, wp, wr, wn, wo, g_ffn.reshape(1, d), wq, keys)


def _topk_exact_tile(s_ref, r2_ref, e2_ref, cc_ref, e1_ref, ds):
    nk = s_ref.shape[2]
    k_top = PEER_TOPK
    w = ds.stop - ds.start
    key = lax.broadcasted_iota(jnp.int32, (nk, w), 0)
    arow = lax.broadcasted_iota(jnp.int32, (k_top, w), 0)
    neg_inf = -jnp.inf

    def top_ranks(s):
        def body(a, carry):
            s, rank, vals = carry
            m = jnp.max(s, axis=0, keepdims=True)
            idx = jnp.min(jnp.where(s == m, key, nk), axis=0, keepdims=True)
            hit = key == idx
            return (jnp.where(hit, neg_inf, s), jnp.where(hit, a, rank), jnp.where(arow == a, m, vals))
        init = (s, jnp.full((nk, w), k_top, jnp.int32), jnp.zeros((k_top, w), F32))
        _, rank, vals = lax.fori_loop(0, k_top, body, init)
        return rank, vals

    s1 = s_ref[0, 0, :, ds]
    s2 = s_ref[0, 1, :, ds]
    r1, v1 = top_ranks(s1)
    r2, v2 = top_ranks(s2)
    top = v1[0:1] + v2[0:1]

    def merge(_, carry):
        cnt, front, z = carry
        m = jnp.max(front, axis=0, keepdims=True)
        a_star = jnp.min(jnp.where(front == m, arow, k_top), axis=0, keepdims=True)
        hit = arow == a_star
        cnt = jnp.where(hit, cnt + 1, cnt)
        nxt = jnp.sum(jnp.where(hit, cnt, 0), axis=0, keepdims=True)
        v2_next = jnp.sum(jnp.where(arow == nxt, v2, 0.0), axis=0, keepdims=True)
        cand = jnp.where(nxt < k_top, v1 + v2_next, neg_inf)
        return cnt, jnp.where(hit, cand, front), z + jnp.exp(m - top)
    init = (jnp.zeros((k_top, w), jnp.int32), v1 + v2[0:1], jnp.zeros((1, w), F32))
    cnt, _, z = lax.fori_loop(0, k_top, merge, init)

    cc = jnp.zeros((nk, w), jnp.int32)
    for a in range(k_top):
        cc = jnp.where(r1 == a, cnt[a:a + 1], cc)
    r2_ref[0, :, ds] = r2.astype(F32).astype(BF16)
    cc_ref[0, :, ds] = cc.astype(F32)
    e2_ref[0, :, ds] = jnp.exp(s2 - v2[0:1]).astype(BF16)
    e1_ref[0, :, ds] = jnp.exp(s1 - v1[0:1]) * (1.0 / z)


def _topk_kernel(s_ref, r2_ref, e2_ref, cc_ref, e1_ref, v1_ref, v2_ref):
    nk = s_ref.shape[2]
    tm = s_ref.shape[3]
    k_top = PEER_TOPK
    w = TOPK_SUB
    subs = [slice(j * w, (j + 1) * w) for j in range(tm // w)]
    neg_inf = -jnp.inf
    no_tie_rank_sum = float(sum(range(k_top)) + k_top * (nk - k_top))
    mark_base, mark_step = 2.0 ** 100, 2.0 ** 96
    lowest_score = -(2.0 ** 99)

    redo = jnp.zeros((1, w), F32)
    for j, ds in enumerate(subs):
        for p, (rank_ref, vals_ref) in enumerate(((cc_ref, v1_ref), (r2_ref, v2_ref))):
            s = s_ref[0, p, :, ds]
            floor_ok = jnp.min(s, axis=0, keepdims=True) > lowest_score
            for a in range(k_top):
                m = jnp.max(s, axis=0, keepdims=True)
                s = jnp.where(s == m, -(mark_base + a * mark_step), s)
                vals_ref[a, j:j + 1, :] = m
            rank = jnp.where(s < lowest_score, s * (-1.0 / mark_step) - mark_base / mark_step, float(k_top))
            exact = floor_ok & (jnp.sum(rank, axis=0, keepdims=True) == no_tie_rank_sum)
            redo = redo + jnp.where(exact, 0.0, 1.0)
            rank_ref[0, :, ds] = rank.astype(rank_ref.dtype)

    v1 = v1_ref[...]
    v2 = v2_ref[...]
    arow = lax.broadcasted_iota(jnp.int32, v1.shape, 0).astype(F32)
    top = v1[0:1] + v2[0:1]
    cnt = jnp.zeros(v1.shape, F32)
    front = v1 + v2[0:1]
    z = jnp.zeros(top.shape, F32)
    for _ in range(k_top):
        m = jnp.max(front, axis=0, keepdims=True)
        a_star = jnp.min(jnp.where(front == m, arow, float(k_top)), axis=0, keepdims=True)
        hit = arow == a_star
        cnt = jnp.where(hit, cnt + 1.0, cnt)
        nxt = jnp.sum(jnp.where(hit, cnt, 0.0), axis=0, keepdims=True)
        v2_next = jnp.sum(jnp.where(arow == nxt, v2, 0.0), axis=0, keepdims=True)
        cand = jnp.where(nxt < float(k_top), v1 + v2_next, neg_inf)
        front = jnp.where(hit, cand, front)
        z = z + jnp.exp(m - top)
    inv_z = 1.0 / z

    for j, ds in enumerate(subs):
        r1 = cc_ref[0, :, ds]
        cc = jnp.zeros((nk, w), F32)
        for a in range(k_top):
            cc = jnp.where(r1 == float(a), cnt[a, j:j + 1, :], cc)
        cc_ref[0, :, ds] = cc
        e1_ref[0, :, ds] = jnp.exp(s_ref[0, 0, :, ds] - v1[0, j:j + 1, :]) * inv_z[0, j:j + 1, :]
        e2_ref[0, :, ds] = jnp.exp(s_ref[0, 1, :, ds] - v2[0, j:j + 1, :]).astype(BF16)

    @pl.when(jnp.sum(redo) > 0.0)
    def _():
        for ds in subs:
            _topk_exact_tile(s_ref, r2_ref, e2_ref, cc_ref, e1_ref, ds)


def _topk(st):
    heads, _, nk, t = st.shape
    tm = TOPK_TM
    out = pl.BlockSpec((1, nk, tm), lambda i, h: (h, 0, i))
    return pl.pallas_call(
        _topk_kernel,
        grid=(t // tm, heads),
        in_specs=[pl.BlockSpec((1, 2, nk, tm), lambda i, h: (h, 0, 0, i))],
        out_specs=[out] * 4,
        out_shape=[jax.ShapeDtypeStruct((heads, nk, t), dt) for dt in (BF16, BF16, F32, F32)],
        scratch_shapes=[pltpu.VMEM((PEER_TOPK, tm // TOPK_SUB, TOPK_SUB), F32)] * 2,
        compiler_params=_params("parallel", "parallel"),
        name="peer_topk",
    )(st)


def _peer_kernel(hnt_ref, u_ref, vt_ref, r2_in_ref, e2_in_ref, cc_ref, e1_ref, h_ref, o_ref,
                 acc_ref, a_ref, ht_ref, r2_ref, e2_ref, ccs_ref, e1s_ref):
    e = pl.program_id(1)
    te = u_ref.shape[0]
    tm = hnt_ref.shape[1]
    nk = r2_ref.shape[2]
    rows_per_step = te // nk
    lane_tiles = tm // PEER_LANES
    tiles = [slice(lt * PEER_LANES, (lt + 1) * PEER_LANES) for lt in range(lane_tiles)]

    @pl.when(e == 0)
    def _():
        acc_ref[...] = jnp.zeros_like(acc_ref)
        for hh in range(PEER_HEADS):
            for lt in range(lane_tiles):
                lanes = slice(lt * PEER_LANES, (lt + 1) * PEER_LANES)
                r2_ref[hh, lt] = r2_in_ref[hh, :, lanes]
                e2_ref[hh, lt] = e2_in_ref[hh, :, lanes]

    a_t = jnp.dot(u_ref[...], hnt_ref[...], preferred_element_type=F32)
    for lt, lanes in enumerate(tiles):
        a_ref[lt] = a_t[:, lanes]
    first_row = pl.multiple_of(e * rows_per_step, rows_per_step)
    for hh in range(PEER_HEADS):
        cc_rows = cc_ref[hh, pl.ds(first_row, rows_per_step), :]
        e1_rows = e1_ref[hh, pl.ds(first_row, rows_per_step), :]
        for ii in range(rows_per_step):
            ccs_ref[ii, hh, 0:1, :] = cc_rows[ii:ii + 1, :]
            e1s_ref[ii, hh, 0:1, :] = e1_rows[ii:ii + 1, :]

    def gate_rows(ii, carry):
        rows = pl.ds(pl.multiple_of(ii * nk, nk), nk)
        for lt, lanes in enumerate(tiles):
            gate = None
            for hh in range(PEER_HEADS):
                cc = jnp.broadcast_to(ccs_ref[ii, hh, 0:1, lanes], (nk, PEER_LANES)).astype(BF16)
                e1 = jnp.broadcast_to(e1s_ref[ii, hh, 0:1, lanes], (nk, PEER_LANES)).astype(BF16)
                term = jnp.where(r2_ref[hh, lt] < cc, e2_ref[hh, lt], jnp.zeros((), BF16)) * e1
                gate = term if gate is None else gate + term
            a = a_ref[lt, rows, :]
            act = 0.5 * a * (1.0 + lax.erf(a * np.float32(1.0 / np.sqrt(2.0))))
            ht_ref[lt, rows, :] = act.astype(BF16) * gate
        return carry

    lax.fori_loop(0, rows_per_step, gate_rows, 0)
    h_t = jnp.concatenate([ht_ref[lt] for lt in range(lane_tiles)], axis=1)
    acc_ref[...] += jnp.dot(vt_ref[0], h_t, preferred_element_type=F32)

    @pl.when(e == pl.num_programs(1) - 1)
    def _():
        o_ref[...] = h_ref[...] + acc_ref[...].T


def _peer(hnt, u_bf, vt_bf, r2, e2, cc, e1, h2):
    d, t = hnt.shape
    n_exp = u_bf.shape[0]
    heads, nk, _ = r2.shape
    tm, te = PEER_TM, PEER_TE
    lt = tm // PEER_LANES
    tab = pl.BlockSpec((heads, nk, tm), lambda i, e: (0, 0, i))
    return pl.pallas_call(
        _peer_kernel,
        grid=(t // tm, n_exp // te),
        in_specs=[pl.BlockSpec((d, tm), lambda i, e: (0, i)),
                  pl.BlockSpec((te, d), lambda i, e: (e, 0)),
                  pl.BlockSpec((1, d, te), lambda i, e: (e, 0, 0)),
                  tab, tab, tab, tab,
                  pl.BlockSpec((tm, d), lambda i, e: (i, 0))],
        out_specs=pl.BlockSpec((tm, d), lambda i, e: (i, 0)),
        out_shape=jax.ShapeDtypeStruct((t, d), F32),
        scratch_shapes=[pltpu.VMEM((d, tm), F32),
                        pltpu.VMEM((lt, te, PEER_LANES), F32), pltpu.VMEM((lt, te, PEER_LANES), BF16),
                        pltpu.VMEM((heads, lt, nk, PEER_LANES), BF16),
                        pltpu.VMEM((heads, lt, nk, PEER_LANES), BF16),
                        pltpu.VMEM((te // nk, heads, 8, tm), F32), pltpu.VMEM((te // nk, heads, 8, tm), F32)],
        compiler_params=_params("parallel", "arbitrary"),
        name="peer_experts",
    )(hnt, u_bf, vt_bf, r2, e2, cc, e1, h2)


def _rmsnorm_kernel(x_ref, g_ref, o_ref):
    x = x_ref[...]
    o_ref[...] = x * lax.rsqrt(jnp.mean(x * x, axis=-1, keepdims=True) + EPS) * g_ref[...]


def _rmsnorm(x2, g):
    t, d = x2.shape
    tm = 512
    return pl.pallas_call(
        _rmsnorm_kernel,
        grid=(t // tm,),
        in_specs=[pl.BlockSpec((tm, d), lambda i: (i, 0)), _const_spec((1, d))],
        out_specs=pl.BlockSpec((tm, d), lambda i: (i, 0)),
        out_shape=jax.ShapeDtypeStruct((t, d), F32),
        compiler_params=_params("parallel"),
        name="final_norm",
    )(x2, g.reshape(1, d))


def _rotary_tables(seq):
    half = RET_DK // 2
    inv = 1.0 / (ROPE_BASE ** jnp.linspace(0.0, 1.0, half, dtype=F32))
    ang = jnp.arange(seq, dtype=F32)[:, None] * inv[None, :]
    cos, sin = jnp.cos(ang), jnp.sin(ang)
    cos_t = jnp.tile(jnp.concatenate([cos, cos], axis=-1), (1, RET_HEADS))
    sin_t = jnp.tile(jnp.concatenate([-sin, sin], axis=-1), (1, RET_HEADS))
    return cos_t, sin_t


def _pool_weights(pool_w, pool_scale):
    halves = POOL_WIDTH // 128
    per = 128 // POOL_GROUP
    blocks = []
    for hf in range(halves):
        blk = jnp.zeros((128, 128), F32)
        for g in range(per):
            sl = slice(g * POOL_GROUP, (g + 1) * POOL_GROUP)
            blk = blk.at[sl, sl].set(pool_w[hf * per + g])
        blocks.append(blk)
    return jnp.stack(blocks).astype(BF16), pool_scale.reshape(halves, 1, 128)


def kernel(x, norm_mix, w_in, pool_w, pool_scale, ret_decay, na_rpb, w_br_pool, w_br_ret, w_br_na,
           w_out, norm_ffn, peer_w_query, peer_sub_keys, peer_u, peer_v, norm_final):
    batch, seq, d = x.shape
    depth = w_in.shape[0]
    rows = seq // GRID_W
    kr = min(NA_ROWS_MAX, rows)
    cos_t, sin_t = _rotary_tables(seq)
    x2 = x.reshape(batch * seq, d)
    for l in range(depth):
        pool, r_q, r_k, r_v, r_g, n_q, n_k, n_v, gates = _inproj(x2, norm_mix[l], w_in[l].astype(BF16))
        w_halves, scale_halves = _pool_weights(pool_w[l], pool_scale[l])
        y_pool = _pool(pool, w_halves, scale_halves, batch, seq)
        lg = jax.nn.log_sigmoid(ret_decay[l].astype(F32))
        y_ret = _retention(lg, r_q, r_k, r_v, r_g, cos_t, sin_t, batch, seq)
        y_na = _na(n_q, n_k, n_v, _na_bias_table(na_rpb[l], rows, kr), batch, seq)
        h2, hnt, st = _merge(x2, y_pool, y_ret, y_na, gates,
                             w_br_pool[l].astype(BF16), w_br_ret[l].astype(BF16), w_br_na[l].astype(BF16),
                             w_out[l].astype(BF16), norm_ffn[l], peer_w_query[l].astype(BF16),
                             peer_sub_keys[l].astype(BF16))
        r2, e2, cc, e1 = _topk(st)
        vt = peer_v[l].reshape(-1, PEER_TE, d).transpose(0, 2, 1).astype(BF16)
        x2 = _peer(hnt, peer_u[l].astype(BF16), vt, r2, e2, cc, e1, h2)
    return _rmsnorm(x2, norm_final).reshape(batch, seq, d)
```

```python
import functools

import numpy as np
import jax
import jax.numpy as jnp
from jax import lax
from jax.experimental import pallas as pl
from jax.experimental.pallas import tpu as pltpu

F32 = jnp.float32
BF16 = jnp.bfloat16

EPS = 1e-6
GRID_W = 64
NEG = -1e30

POOL_WINDOWS = (2, 4, 8, 16)
POOL_GROUP = 64
POOL_WIDTH = POOL_GROUP * len(POOL_WINDOWS)

RET_HEADS = 4
RET_DK = 64
RET_DV = 128
RET_QK = RET_HEADS * RET_DK
RET_V = RET_HEADS * RET_DV
ROPE_BASE = 10000.0

NA_HEADS = 4
NA_DH = 64
NA_WIDTH = NA_HEADS * NA_DH
NA_ROWS_MAX = 8
NA_COLS = 16

N_BRANCH = 3

PEER_KEYS = 128
PEER_HEADS = 8
PEER_DK = 128
PEER_TOPK = 16

V7X_VMEM_LIMIT_BYTES = 56 * 1024 * 1024

INPROJ_TM = 256
RET_CHUNK = 256
NA_ROWS_PER_STEP = 8
NA_ROWS_PER_ITER = 8
MERGE_TM = 256
TOPK_TM = 1024
TOPK_SUB = 128
PEER_TM = 512
PEER_TE = 2048
PEER_LANES = 128


def _params(*sem):
    return pltpu.CompilerParams(dimension_semantics=sem, vmem_limit_bytes=V7X_VMEM_LIMIT_BYTES)


def _const_spec(shape):
    nd = len(shape)
    return pl.BlockSpec(shape, lambda *_: (0,) * nd)


def _inproj_kernel(x_ref, g_ref, w_ref, pool_ref, rq_ref, rk_ref, rv_ref, rg_ref,
                   nq_ref, nk_ref, nv_ref, gate_ref):
    x = x_ref[...]
    xn = x * lax.rsqrt(jnp.mean(x * x, axis=-1, keepdims=True) + EPS) * g_ref[...]
    xb = xn.astype(BF16)
    off = 0
    for ref, scale in ((pool_ref, None), (rq_ref, None), (rk_ref, None), (rv_ref, None), (rg_ref, None),
                       (nq_ref, NA_DH ** -0.5), (nk_ref, None), (nv_ref, None)):
        width = ref.shape[-1]
        y = jnp.dot(xb, w_ref[:, off:off + width], preferred_element_type=F32)
        if scale is not None:
            y = y * scale
        ref[...] = y.astype(ref.dtype)
        off += width
    d = x.shape[-1]
    for b in range(N_BRANCH):
        gate_ref[:, b * d:(b + 1) * d] = jnp.dot(
            xb, w_ref[:, off + b * d:off + (b + 1) * d], preferred_element_type=F32)


def _inproj(x2, g, w_in_bf):
    t, d = x2.shape
    tm = INPROJ_TM
    widths = (POOL_WIDTH, RET_QK, RET_QK, RET_V, RET_V, NA_WIDTH, NA_WIDTH, NA_WIDTH, N_BRANCH * d)
    dtypes = (F32, F32, F32, BF16, F32, BF16, BF16, BF16, F32)
    tok = lambda w: pl.BlockSpec((tm, w), lambda i: (i, 0))
    return pl.pallas_call(
        _inproj_kernel,
        grid=(t // tm,),
        in_specs=[tok(d), _const_spec((1, d)), _const_spec(w_in_bf.shape)],
        out_specs=[tok(w) for w in widths],
        out_shape=[jax.ShapeDtypeStruct((t, w), dt) for w, dt in zip(widths, dtypes)],
        compiler_params=_params("parallel"),
        name="inproj",
    )(x2, g.reshape(1, d), w_in_bf)


def _pool_kernel(p_ref, w_ref, scale_ref, o_ref, *, halves):
    x = p_ref[...]
    s = x.shape[0]
    t = lax.broadcasted_iota(jnp.int32, x.shape, 0)
    lane = lax.broadcasted_iota(jnp.int32, x.shape, 1)
    half = pl.program_id(1)

    def down(v, k):
        return jnp.where(t >= k, pltpu.roll(v, k, 0), 0.0)

    def up(v, k):
        return jnp.where(t < s - k, pltpu.roll(v, s - k, 0), 0.0)

    trail = {1: x}
    lead = {1: x}
    for m in (2, 4, 8):
        trail[m] = trail[m // 2] + down(trail[m // 2], m // 2)
        lead[m] = lead[m // 2] + up(lead[m // 2], m // 2)

    def window(hw):
        return down(trail[hw], 1) + lead[hw]

    def pick(lo_first, hi_first, lo_second, hi_second):
        first = jnp.where(lane < POOL_GROUP, lo_first, hi_first)
        second = jnp.where(lane < POOL_GROUP, lo_second, hi_second)
        return jnp.where(half == 0, first, second)

    hw = [w // 2 for w in POOL_WINDOWS]
    wsum = pick(window(hw[0]), window(hw[1]), window(hw[2]), window(hw[3]))
    hwv = pick(hw[0], hw[1], hw[2], hw[3])
    cnt = (jnp.minimum(t + hwv, s) - jnp.maximum(t - hwv, 0)).astype(F32)
    dlt = wsum / cnt - x
    y = jnp.dot(dlt.astype(BF16), w_ref[0], preferred_element_type=F32)
    o_ref[...] = y * scale_ref[0]


def _pool(p_pool, w_halves, scale_halves, batch, seq):
    t = p_pool.shape[0]
    halves = POOL_WIDTH // 128
    return pl.pallas_call(
        functools.partial(_pool_kernel, halves=halves),
        grid=(batch, halves),
        in_specs=[pl.BlockSpec((seq, 128), lambda b, h: (b, h)),
                  pl.BlockSpec((1, 128, 128), lambda b, h: (h, 0, 0)),
                  pl.BlockSpec((1, 1, 128), lambda b, h: (h, 0, 0))],
        out_specs=pl.BlockSpec((seq, 128), lambda b, h: (b, h)),
        out_shape=jax.ShapeDtypeStruct((t, POOL_WIDTH), F32),
        compiler_params=_params("parallel", "parallel"),
        name="pool",
    )(p_pool, w_halves, scale_halves)


def _rotary(x, cos, sin_signed, first_half):
    width = x.shape[-1]
    half = RET_DK // 2
    swapped = jnp.where(first_half, pltpu.roll(x, width - half, 1), pltpu.roll(x, half, 1))
    return x * cos + swapped * sin_signed


def _retention_kernel(lg_ref, q_ref, k_ref, v_ref, cos_ref, sin_ref, *rest, reverse):
    if reverse:
        fwd_ref, g_ref, o_ref, state_ref = rest
    else:
        o_ref, state_ref = rest
    c = q_ref.shape[0]

    @pl.when(pl.program_id(1) == 0)
    def _():
        state_ref[...] = jnp.zeros_like(state_ref)

    lane = lax.broadcasted_iota(jnp.int32, (c, RET_QK), 1)
    first_half = (lane % RET_DK) < (RET_DK // 2)
    cos = cos_ref[...]
    sin = sin_ref[...]
    q = _rotary(q_ref[...], cos, sin, first_half)
    k = _rotary(k_ref[...], cos, sin, first_half) * (RET_DK ** -0.5)
    v = v_ref[...]

    row = lax.broadcasted_iota(jnp.int32, (c, c), 0)
    col = lax.broadcasted_iota(jnp.int32, (c, c), 1)
    pos = lax.broadcasted_iota(jnp.int32, (c, 1), 0).astype(F32)
    if reverse:
        diff = col - row
        mask = diff > 0
        q_pow = c - pos
        k_pow = pos
    else:
        diff = row - col
        mask = diff >= 0
        q_pow = pos + 1.0
        k_pow = (c - 1) - pos
    dist = jnp.where(mask, diff, 0).astype(F32)
    direction = 1 if reverse else 0

    for h in range(RET_HEADS):
        lg = lg_ref[direction, h]
        qh = q[:, h * RET_DK:(h + 1) * RET_DK]
        kh = k[:, h * RET_DK:(h + 1) * RET_DK]
        vh = v[:, h * RET_DV:(h + 1) * RET_DV]
        decay = jnp.where(mask, jnp.exp(lg * dist), 0.0)
        scores = lax.dot_general(qh.astype(BF16), kh.astype(BF16), (((1,), (1,)), ((), ())),
                                 preferred_element_type=F32) * decay
        intra = jnp.dot(scores.astype(BF16), vh, preferred_element_type=F32)
        q_dec = qh * jnp.exp(lg * q_pow)
        k_dec = kh * jnp.exp(lg * k_pow)
        state = state_ref[h]
        cross = jnp.dot(q_dec.astype(BF16), state.astype(BF16), preferred_element_type=F32)
        chunk_kv = lax.dot_general(k_dec.astype(BF16), vh, (((0,), (0,)), ((), ())),
                                   preferred_element_type=F32)
        chunk_decay = jnp.exp(lg * jnp.full((1, RET_DV), float(c), F32))
        state_ref[h] = state * chunk_decay + chunk_kv
        out = intra + cross
        sl = slice(h * RET_DV, (h + 1) * RET_DV)
        if reverse:
            y = fwd_ref[:, sl] + out
            y = y * lax.rsqrt(jnp.mean(y * y, axis=-1, keepdims=True) + EPS)
            g = g_ref[:, sl]
            o_ref[:, sl] = (g / (1.0 + jnp.exp(-g))) * y
        else:
            o_ref[:, sl] = out


def _retention(lg, r_q, r_k, r_v, r_g, cos_t, sin_t, batch, seq):
    t = r_q.shape[0]
    c = RET_CHUNK
    n = seq // c
    scratch = [pltpu.VMEM((RET_HEADS, RET_DK, RET_DV), F32)]
    smem = pl.BlockSpec(memory_space=pltpu.SMEM)

    def specs(chunk_of):
        tok = lambda w: pl.BlockSpec((c, w), lambda b, i: (b * n + chunk_of(i), 0))
        tab = pl.BlockSpec((c, RET_QK), lambda b, i: (chunk_of(i), 0))
        return tok, tab

    tok, tab = specs(lambda i: i)
    fwd = pl.pallas_call(
        functools.partial(_retention_kernel, reverse=False),
        grid=(batch, n),
        in_specs=[smem, tok(RET_QK), tok(RET_QK), tok(RET_V), tab, tab],
        out_specs=tok(RET_V),
        out_shape=jax.ShapeDtypeStruct((t, RET_V), F32),
        scratch_shapes=scratch,
        compiler_params=_params("parallel", "arbitrary"),
        name="retention_fwd",
    )(lg, r_q, r_k, r_v, cos_t, sin_t)
    tok, tab = specs(lambda i: n - 1 - i)
    return pl.pallas_call(
        functools.partial(_retention_kernel, reverse=True),
        grid=(batch, n),
        in_specs=[smem, tok(RET_QK), tok(RET_QK), tok(RET_V), tab, tab, tok(RET_V), tok(RET_V)],
        out_specs=tok(RET_V),
        out_shape=jax.ShapeDtypeStruct((t, RET_V), F32),
        scratch_shapes=scratch,
        compiler_params=_params("parallel", "arbitrary"),
        name="retention_bwd",
    )(lg, r_q, r_k, r_v, cos_t, sin_t, fwd, r_g)


def _na_kernel(q_ref, k_ref, v_ref, bias_ref, o_ref, *, rows, kr):
    step = pl.program_id(1)
    rps = q_ref.shape[0] // GRID_W
    nkeys = kr * GRID_W
    lo = kr // 2
    lane_head = lax.broadcasted_iota(jnp.int32, (GRID_W, NA_WIDTH), 1) // NA_DH
    head_lanes = [lane_head == h for h in range(NA_HEADS)]

    def body(it, carry):
        jobs = []
        for j in range(NA_ROWS_PER_ITER):
            i = it * NA_ROWS_PER_ITER + j
            r = step * rps + i
            r0 = jnp.clip(r - lo, 0, rows - kr)
            ver = jnp.where(r < lo, r, jnp.where(r > rows - kr + lo, r - (rows - kr), lo))
            key_rows = pl.ds(pl.multiple_of(r0 * GRID_W, GRID_W), nkeys)
            qq = q_ref[pl.ds(pl.multiple_of(i * GRID_W, GRID_W), GRID_W), :]
            kk = k_ref[key_rows, :]
            scores = [lax.dot_general(jnp.where(mask, qq, jnp.zeros((), qq.dtype)), kk, (((1,), (1,)), ((), ())),
                                      preferred_element_type=F32) + bias_ref[ver, h]
                      for h, mask in enumerate(head_lanes)]
            jobs.append((i, key_rows, scores))
        probs = []
        for i, key_rows, scores in jobs:
            ps = []
            for s in scores:
                p = jnp.exp(s - jnp.max(s, axis=-1, keepdims=True))
                ps.append((p / jnp.sum(p, axis=-1, keepdims=True)).astype(BF16))
            probs.append((i, key_rows, ps))
        for i, key_rows, ps in probs:
            vv = v_ref[key_rows, :]
            out = jnp.zeros((GRID_W, NA_WIDTH), F32)
            for p, mask in zip(ps, head_lanes):
                out = jnp.where(mask, jnp.dot(p, vv, preferred_element_type=F32), out)
            o_ref[pl.ds(pl.multiple_of(i * GRID_W, GRID_W), GRID_W), :] = out
        return carry

    lax.fori_loop(0, rps // NA_ROWS_PER_ITER, body, 0)


def _na_bias_table(rpb, rows, kr):
    lo = kr // 2
    reps = list(range(lo)) + [lo] + list(range(rows - kr + lo + 1, rows))
    c = np.arange(GRID_W)
    col_start = np.clip(c - NA_COLS // 2, 0, GRID_W - NA_COLS)
    col_in = (c[None, :] >= col_start[:, None]) & (c[None, :] < col_start[:, None] + NA_COLS)
    dj = np.clip(c[None, :] - c[:, None], -(NA_COLS - 1), NA_COLS - 1) + (NA_COLS - 1)
    pick = ((dj[None] == np.arange(2 * NA_COLS - 1)[:, None, None]) & col_in[None]).astype(np.float32)
    di0 = [int(np.clip(r - lo, 0, rows - kr)) - r + (NA_ROWS_MAX - 1) for r in reps]
    rpb_rows = jnp.stack([rpb[:, d0:d0 + kr, :] for d0 in di0]).astype(F32)
    bias = jnp.einsum('vhij,jck->vhcik', rpb_rows, jnp.asarray(pick), precision=lax.Precision.HIGHEST)
    bias = bias + jnp.asarray(np.where(col_in, 0.0, NEG).astype(np.float32))[None, None, :, None, :]
    return bias.reshape(len(reps), NA_HEADS, GRID_W, kr * GRID_W)


def _na(n_q, n_k, n_v, bias, batch, seq):
    t = n_q.shape[0]
    rows = seq // GRID_W
    kr = min(NA_ROWS_MAX, rows)
    rps = min(NA_ROWS_PER_STEP, rows)
    blk = rps * GRID_W
    steps = rows // rps
    return pl.pallas_call(
        functools.partial(_na_kernel, rows=rows, kr=kr),
        grid=(batch, steps),
        in_specs=[pl.BlockSpec((blk, NA_WIDTH), lambda b, i: (b * steps + i, 0)),
                  pl.BlockSpec((seq, NA_WIDTH), lambda b, i: (b, 0)),
                  pl.BlockSpec((seq, NA_WIDTH), lambda b, i: (b, 0)),
                  _const_spec(bias.shape)],
        out_specs=pl.BlockSpec((blk, NA_WIDTH), lambda b, i: (b * steps + i, 0)),
        out_shape=jax.ShapeDtypeStruct((t, NA_WIDTH), F32),
        compiler_params=_params("parallel", "arbitrary"),
        name="natten",
    )(n_q, n_k, n_v, bias)


def _merge_kernel(x_ref, yp_ref, yr_ref, yn_ref, gate_ref, wp_ref, wr_ref, wn_ref, wo_ref,
                  gf_ref, wq_ref, keys_ref, h_ref, hnt_ref, st_ref):
    d = x_ref.shape[-1]
    merged = None
    for b, (y_ref, w_ref) in enumerate(((yp_ref, wp_ref), (yr_ref, wr_ref), (yn_ref, wn_ref))):
        gate = 1.0 / (1.0 + jnp.exp(-gate_ref[:, b * d:(b + 1) * d]))
        term = gate * jnp.dot(y_ref[...].astype(BF16), w_ref[...], preferred_element_type=F32)
        merged = term if merged is None else merged + term
    h = x_ref[...] + jnp.dot(merged.astype(BF16), wo_ref[...], preferred_element_type=F32)
    h_ref[...] = h
    hn = h * lax.rsqrt(jnp.mean(h * h, axis=-1, keepdims=True) + EPS) * gf_ref[...]
    hnb = hn.astype(BF16)
    hnt_ref[...] = hn.T.astype(BF16)
    q = jnp.dot(hnb, wq_ref[...], preferred_element_type=F32).astype(BF16)
    for hh in range(PEER_HEADS):
        for p in range(2):
            col = (hh * 2 + p) * PEER_DK
            st_ref[hh, p] = lax.dot_general(keys_ref[p], q[:, col:col + PEER_DK],
                                            (((1,), (1,)), ((), ())), preferred_element_type=F32)


def _merge(x2, y_pool, y_ret, y_na, gates, wp, wr, wn, wo, g_ffn, wq, keys):
    t, d = x2.shape
    tm = MERGE_TM
    tok = lambda w: pl.BlockSpec((tm, w), lambda i: (i, 0))
    return pl.pallas_call(
        _merge_kernel,
        grid=(t // tm,),
        in_specs=[tok(d), tok(POOL_WIDTH), tok(RET_V), tok(NA_WIDTH), tok(N_BRANCH * d),
                  _const_spec(wp.shape), _const_spec(wr.shape), _const_spec(wn.shape), _const_spec(wo.shape),
                  _const_spec((1, d)), _const_spec(wq.shape), _const_spec(keys.shape)],
        out_specs=[tok(d),
                   pl.BlockSpec((d, tm), lambda i: (0, i)),
                   pl.BlockSpec((PEER_HEADS, 2, PEER_KEYS, tm), lambda i: (0, 0, 0, i))],
        out_shape=[jax.ShapeDtypeStruct((t, d), F32),
                   jax.ShapeDtypeStruct((d, t), BF16),
                   jax.ShapeDtypeStruct((PEER_HEADS, 2, PEER_KEYS, t), F32)],
        compiler_params=_params("parallel"),
        name="merge",
    )(x2, y_pool, y_ret, y_na, gates, wp, wr, wn, wo, g_ffn.reshape(1, d), wq, keys)


def _topk_exact_tile(s_ref, r2_ref, e2_ref, cc_ref, e1_ref, ds):
    nk = s_ref.shape[2]
    k_top = PEER_TOPK
    w = ds.stop - ds.start
    key = lax.broadcasted_iota(jnp.int32, (nk, w), 0)
    arow = lax.broadcasted_iota(jnp.int32, (k_top, w), 0)
    neg_inf = -jnp.inf

    def top_ranks(s):
        def body(a, carry):
            s, rank, vals = carry
            m = jnp.max(s, axis=0, keepdims=True)
            idx = jnp.min(jnp.where(s == m, key, nk), axis=0, keepdims=True)
            hit = key == idx
            return (jnp.where(hit, neg_inf, s), jnp.where(hit, a, rank), jnp.where(arow == a, m, vals))
        init = (s, jnp.full((nk, w), k_top, jnp.int32), jnp.zeros((k_top, w), F32))
        _, rank, vals = lax.fori_loop(0, k_top, body, init)
        return rank, vals

    s1 = s_ref[0, 0, :, ds]
    s2 = s_ref[0, 1, :, ds]
    r1, v1 = top_ranks(s1)
    r2, v2 = top_ranks(s2)
    top = v1[0:1] + v2[0:1]

    def merge(_, carry):
        cnt, front, z = carry
        m = jnp.max(front, axis=0, keepdims=True)
        a_star = jnp.min(jnp.where(front == m, arow, k_top), axis=0, keepdims=True)
        hit = arow == a_star
        cnt = jnp.where(hit, cnt + 1, cnt)
        nxt = jnp.sum(jnp.where(hit, cnt, 0), axis=0, keepdims=True)
        v2_next = jnp.sum(jnp.where(arow == nxt, v2, 0.0), axis=0, keepdims=True)
        cand = jnp.where(nxt < k_top, v1 + v2_next, neg_inf)
        return cnt, jnp.where(hit, cand, front), z + jnp.exp(m - top)
    init = (jnp.zeros((k_top, w), jnp.int32), v1 + v2[0:1], jnp.zeros((1, w), F32))
    cnt, _, z = lax.fori_loop(0, k_top, merge, init)

    cc = jnp.zeros((nk, w), jnp.int32)
    for a in range(k_top):
        cc = jnp.where(r1 == a, cnt[a:a + 1], cc)
    r2_ref[0, :, ds] = r2.astype(F32).astype(BF16)
    cc_ref[0, :, ds] = cc.astype(F32)
    e2_ref[0, :, ds] = jnp.exp(s2 - v2[0:1]).astype(BF16)
    e1_ref[0, :, ds] = jnp.exp(s1 - v1[0:1]) * (1.0 / z)


def _topk_kernel(s_ref, r2_ref, e2_ref, cc_ref, e1_ref, v1_ref, v2_ref):
    nk = s_ref.shape[2]
    tm = s_ref.shape[3]
    k_top = PEER_TOPK
    w = TOPK_SUB
    subs = [slice(j * w, (j + 1) * w) for j in range(tm // w)]
    neg_inf = -jnp.inf
    no_tie_rank_sum = float(sum(range(k_top)) + k_top * (nk - k_top))
    mark_base, mark_step = 2.0 ** 100, 2.0 ** 96
    lowest_score = -(2.0 ** 99)

    redo = jnp.zeros((1, w), F32)
    for j, ds in enumerate(subs):
        for p, (rank_ref, vals_ref) in enumerate(((cc_ref, v1_ref), (r2_ref, v2_ref))):
            s = s_ref[0, p, :, ds]
            floor_ok = jnp.min(s, axis=0, keepdims=True) > lowest_score
            for a in range(k_top):
                m = jnp.max(s, axis=0, keepdims=True)
                s = jnp.where(s == m, -(mark_base + a * mark_step), s)
                vals_ref[a, j:j + 1, :] = m
            rank = jnp.where(s < lowest_score, s * (-1.0 / mark_step) - mark_base / mark_step, float(k_top))
            exact = floor_ok & (jnp.sum(rank, axis=0, keepdims=True) == no_tie_rank_sum)
            redo = redo + jnp.where(exact, 0.0, 1.0)
            rank_ref[0, :, ds] = rank.astype(rank_ref.dtype)

    v1 = v1_ref[...]
    v2 = v2_ref[...]
    arow = lax.broadcasted_iota(jnp.int32, v1.shape, 0).astype(F32)
    top = v1[0:1] + v2[0:1]
    cnt = jnp.zeros(v1.shape, F32)
    front = v1 + v2[0:1]
    z = jnp.zeros(top.shape, F32)
    for _ in range(k_top):
        m = jnp.max(front, axis=0, keepdims=True)
        a_star = jnp.min(jnp.where(front == m, arow, float(k_top)), axis=0, keepdims=True)
        hit = arow == a_star
        cnt = jnp.where(hit, cnt + 1.0, cnt)
        nxt = jnp.sum(jnp.where(hit, cnt, 0.0), axis=0, keepdims=True)
        v2_next = jnp.sum(jnp.where(arow == nxt, v2, 0.0), axis=0, keepdims=True)
        cand = jnp.where(nxt < float(k_top), v1 + v2_next, neg_inf)
        front = jnp.where(hit, cand, front)
        z = z + jnp.exp(m - top)
    inv_z = 1.0 / z

    for j, ds in enumerate(subs):
        r1 = cc_ref[0, :, ds]
        cc = jnp.zeros((nk, w), F32)
        for a in range(k_top):
            cc = jnp.where(r1 == float(a), cnt[a, j:j + 1, :], cc)
        cc_ref[0, :, ds] = cc
        e1_ref[0, :, ds] = jnp.exp(s_ref[0, 0, :, ds] - v1[0, j:j + 1, :]) * inv_z[0, j:j + 1, :]
        e2_ref[0, :, ds] = jnp.exp(s_ref[0, 1, :, ds] - v2[0, j:j + 1, :]).astype(BF16)

    @pl.when(jnp.sum(redo) > 0.0)
    def _():
        for ds in subs:
            _topk_exact_tile(s_ref, r2_ref, e2_ref, cc_ref, e1_ref, ds)


def _topk(st):
    heads, _, nk, t = st.shape
    tm = TOPK_TM
    out = pl.BlockSpec((1, nk, tm), lambda i, h: (h, 0, i))
    return pl.pallas_call(
        _topk_kernel,
        grid=(t // tm, heads),
        in_specs=[pl.BlockSpec((1, 2, nk, tm), lambda i, h: (h, 0, 0, i))],
        out_specs=[out] * 4,
        out_shape=[jax.ShapeDtypeStruct((heads, nk, t), dt) for dt in (BF16, BF16, F32, F32)],
        scratch_shapes=[pltpu.VMEM((PEER_TOPK, tm // TOPK_SUB, TOPK_SUB), F32)] * 2,
        compiler_params=_params("parallel", "parallel"),
        name="peer_topk",
    )(st)


def _peer_kernel(hnt_ref, u_ref, vt_ref, r2_in_ref, e2_in_ref, cc_ref, e1_ref, h_ref, o_ref,
                 acc_ref, a_ref, ht_ref, r2_ref, e2_ref):
    e = pl.program_id(1)
    te = u_ref.shape[0]
    tm = hnt_ref.shape[1]
    nk = r2_ref.shape[1]
    rows_per_step = te // nk

    @pl.when(e == 0)
    def _():
        acc_ref[...] = jnp.zeros_like(acc_ref)
        r2_ref[...] = r2_in_ref[...]
        e2_ref[...] = e2_in_ref[...]

    a_ref[...] = jnp.dot(u_ref[...], hnt_ref[...], preferred_element_type=F32)
    for ii in range(rows_per_step):
        i = e * rows_per_step + ii
        rows = slice(ii * nk, (ii + 1) * nk)
        cc_rows = [cc_ref[hh, pl.ds(i, 1), :] for hh in range(PEER_HEADS)]
        e1_rows = [e1_ref[hh, pl.ds(i, 1), :] for hh in range(PEER_HEADS)]
        for lt in range(tm // PEER_LANES):
            lanes = slice(lt * PEER_LANES, (lt + 1) * PEER_LANES)
            gate = None
            for hh in range(PEER_HEADS):
                cc = jnp.broadcast_to(cc_rows[hh][:, lanes], (nk, PEER_LANES)).astype(BF16)
                e1 = jnp.broadcast_to(e1_rows[hh][:, lanes], (nk, PEER_LANES)).astype(BF16)
                term = jnp.where(r2_ref[hh, :, lanes] < cc, e2_ref[hh, :, lanes], jnp.zeros((), BF16)) * e1
                gate = term if gate is None else gate + term
            a = a_ref[rows, lanes]
            act = 0.5 * a * (1.0 + lax.erf(a * np.float32(1.0 / np.sqrt(2.0))))
            ht_ref[rows, lanes] = act.astype(BF16) * gate
    acc_ref[...] += jnp.dot(vt_ref[0], ht_ref[...], preferred_element_type=F32)

    @pl.when(e == pl.num_programs(1) - 1)
    def _():
        o_ref[...] = h_ref[...] + acc_ref[...].T


def _peer(hnt, u_bf, vt_bf, r2, e2, cc, e1, h2):
    d, t = hnt.shape
    n_exp = u_bf.shape[0]
    heads, nk, _ = r2.shape
    tm, te = PEER_TM, PEER_TE
    tab = pl.BlockSpec((heads, nk, tm), lambda i, e: (0, 0, i))
    return pl.pallas_call(
        _peer_kernel,
        grid=(t // tm, n_exp // te),
        in_specs=[pl.BlockSpec((d, tm), lambda i, e: (0, i)),
                  pl.BlockSpec((te, d), lambda i, e: (e, 0)),
                  pl.BlockSpec((1, d, te), lambda i, e: (e, 0, 0)),
                  tab, tab, tab, tab,
                  pl.BlockSpec((tm, d), lambda i, e: (i, 0))],
        out_specs=pl.BlockSpec((tm, d), lambda i, e: (i, 0)),
        out_shape=jax.ShapeDtypeStruct((t, d), F32),
        scratch_shapes=[pltpu.VMEM((d, tm), F32), pltpu.VMEM((te, tm), F32), pltpu.VMEM((te, tm), BF16),
                        pltpu.VMEM((heads, nk, tm), BF16), pltpu.VMEM((heads, nk, tm), BF16)],
        compiler_params=_params("parallel", "arbitrary"),
        name="peer_experts",
    )(hnt, u_bf, vt_bf, r2, e2, cc, e1, h2)


def _rmsnorm_kernel(x_ref, g_ref, o_ref):
    x = x_ref[...]
    o_ref[...] = x * lax.rsqrt(jnp.mean(x * x, axis=-1, keepdims=True) + EPS) * g_ref[...]


def _rmsnorm(x2, g):
    t, d = x2.shape
    tm = 512
    return pl.pallas_call(
        _rmsnorm_kernel,
        grid=(t // tm,),
        in_specs=[pl.BlockSpec((tm, d), lambda i: (i, 0)), _const_spec((1, d))],
        out_specs=pl.BlockSpec((tm, d), lambda i: (i, 0)),
        out_shape=jax.ShapeDtypeStruct((t, d), F32),
        compiler_params=_params("parallel"),
        name="final_norm",
    )(x2, g.reshape(1, d))


def _rotary_tables(seq):
    half = RET_DK // 2
    inv = 1.0 / (ROPE_BASE ** jnp.linspace(0.0, 1.0, half, dtype=F32))
    ang = jnp.arange(seq, dtype=F32)[:, None] * inv[None, :]
    cos, sin = jnp.cos(ang), jnp.sin(ang)
    cos_t = jnp.tile(jnp.concatenate([cos, cos], axis=-1), (1, RET_HEADS))
    sin_t = jnp.tile(jnp.concatenate([-sin, sin], axis=-1), (1, RET_HEADS))
    return cos_t, sin_t


def _pool_weights(pool_w, pool_scale):
    halves = POOL_WIDTH // 128
    per = 128 // POOL_GROUP
    blocks = []
    for hf in range(halves):
        blk = jnp.zeros((128, 128), F32)
        for g in range(per):
            sl = slice(g * POOL_GROUP, (g + 1) * POOL_GROUP)
            blk = blk.at[sl, sl].set(pool_w[hf * per + g])
        blocks.append(blk)
    return jnp.stack(blocks).astype(BF16), pool_scale.reshape(halves, 1, 128)


def kernel(x, norm_mix, w_in, pool_w, pool_scale, ret_decay, na_rpb, w_br_pool, w_br_ret, w_br_na,
           w_out, norm_ffn, peer_w_query, peer_sub_keys, peer_u, peer_v, norm_final):
    batch, seq, d = x.shape
    depth = w_in.shape[0]
    rows = seq // GRID_W
    kr = min(NA_ROWS_MAX, rows)
    cos_t, sin_t = _rotary_tables(seq)
    x2 = x.reshape(batch * seq, d)
    for l in range(depth):
        pool, r_q, r_k, r_v, r_g, n_q, n_k, n_v, gates = _inproj(x2, norm_mix[l], w_in[l].astype(BF16))
        w_halves, scale_halves = _pool_weights(pool_w[l], pool_scale[l])
        y_pool = _pool(pool, w_halves, scale_halves, batch, seq)
        lg = jax.nn.log_sigmoid(ret_decay[l].astype(F32))
        y_ret = _retention(lg, r_q, r_k, r_v, r_g, cos_t, sin_t, batch, seq)
        y_na = _na(n_q, n_k, n_v, _na_bias_table(na_rpb[l], rows, kr), batch, seq)
        h2, hnt, st = _merge(x2, y_pool, y_ret, y_na, gates,
                             w_br_pool[l].astype(BF16), w_br_ret[l].astype(BF16), w_br_na[l].astype(BF16),
                             w_out[l].astype(BF16), norm_ffn[l], peer_w_query[l].astype(BF16),
                             peer_sub_keys[l].astype(BF16))
        r2, e2, cc, e1 = _topk(st)
        vt = peer_v[l].reshape(-1, PEER_TE, d).transpose(0, 2, 1).astype(BF16)
        x2 = _peer(hnt, peer_u[l].astype(BF16), vt, r2, e2, cc, e1, h2)
    return _rmsnorm(x2, norm_final).reshape(batch, seq, d)
```

```python
import functools

import numpy as np
import jax
import jax.numpy as jnp
from jax import lax
from jax.experimental import pallas as pl
from jax.experimental.pallas import tpu as pltpu

F32 = jnp.float32
BF16 = jnp.bfloat16

EPS = 1e-6
GRID_W = 64
NEG = -1e30

POOL_WINDOWS = (2, 4, 8, 16)
POOL_GROUP = 64
POOL_WIDTH = POOL_GROUP * len(POOL_WINDOWS)

RET_HEADS = 4
RET_DK = 64
RET_DV = 128
RET_QK = RET_HEADS * RET_DK
RET_V = RET_HEADS * RET_DV
ROPE_BASE = 10000.0

NA_HEADS = 4
NA_DH = 64
NA_WIDTH = NA_HEADS * NA_DH
NA_ROWS_MAX = 8
NA_COLS = 16

N_BRANCH = 3

PEER_KEYS = 128
PEER_HEADS = 8
PEER_DK = 128
PEER_TOPK = 16
GELU_HALF = 0.5

V7X_VMEM_LIMIT_BYTES = 56 * 1024 * 1024

INPROJ_TM = 256
RET_CHUNK = 256
NA_ROWS_PER_STEP = 8
NA_ROWS_PER_ITER = 8
MERGE_TM = 256
TOPK_TM = 1024
TOPK_SUB = 128
PEER_TM = 512
PEER_TE = 2048
PEER_LANES = 128


def _params(*sem):
    return pltpu.CompilerParams(dimension_semantics=sem, vmem_limit_bytes=V7X_VMEM_LIMIT_BYTES)


def _const_spec(shape):
    nd = len(shape)
    return pl.BlockSpec(shape, lambda *_: (0,) * nd)


def _inproj_kernel(x_ref, g_ref, w_ref, pool_ref, rq_ref, rk_ref, rv_ref, rg_ref,
                   nq_ref, nk_ref, nv_ref, gate_ref):
    x = x_ref[...]
    xn = x * lax.rsqrt(jnp.mean(x * x, axis=-1, keepdims=True) + EPS) * g_ref[...]
    xb = xn.astype(BF16)
    off = 0
    for ref, scale in ((pool_ref, None), (rq_ref, None), (rk_ref, None), (rv_ref, None), (rg_ref, None),
                       (nq_ref, NA_DH ** -0.5), (nk_ref, None), (nv_ref, None)):
        width = ref.shape[-1]
        y = jnp.dot(xb, w_ref[:, off:off + width], preferred_element_type=F32)
        if scale is not None:
            y = y * scale
        ref[...] = y.astype(ref.dtype)
        off += width
    d = x.shape[-1]
    for b in range(N_BRANCH):
        gate_ref[:, b * d:(b + 1) * d] = jnp.dot(
            xb, w_ref[:, off + b * d:off + (b + 1) * d], preferred_element_type=F32)


def _inproj(x2, g, w_in_bf):
    t, d = x2.shape
    tm = INPROJ_TM
    widths = (POOL_WIDTH, RET_QK, RET_QK, RET_V, RET_V, NA_WIDTH, NA_WIDTH, NA_WIDTH, N_BRANCH * d)
    dtypes = (F32, F32, F32, BF16, F32, BF16, BF16, BF16, F32)
    tok = lambda w: pl.BlockSpec((tm, w), lambda i: (i, 0))
    return pl.pallas_call(
        _inproj_kernel,
        grid=(t // tm,),
        in_specs=[tok(d), _const_spec((1, d)), _const_spec(w_in_bf.shape)],
        out_specs=[tok(w) for w in widths],
        out_shape=[jax.ShapeDtypeStruct((t, w), dt) for w, dt in zip(widths, dtypes)],
        compiler_params=_params("parallel"),
        name="inproj",
    )(x2, g.reshape(1, d), w_in_bf)


def _pool_kernel(p_ref, w_ref, scale_ref, o_ref, *, halves):
    x = p_ref[...]
    s = x.shape[0]
    t = lax.broadcasted_iota(jnp.int32, x.shape, 0)
    lane = lax.broadcasted_iota(jnp.int32, x.shape, 1)
    half = pl.program_id(1)

    def down(v, k):
        return jnp.where(t >= k, pltpu.roll(v, k, 0), 0.0)

    def up(v, k):
        return jnp.where(t < s - k, pltpu.roll(v, s - k, 0), 0.0)

    trail = {1: x}
    lead = {1: x}
    for m in (2, 4, 8):
        trail[m] = trail[m // 2] + down(trail[m // 2], m // 2)
        lead[m] = lead[m // 2] + up(lead[m // 2], m // 2)

    def window(hw):
        return down(trail[hw], 1) + lead[hw]

    def pick(lo_first, hi_first, lo_second, hi_second):
        first = jnp.where(lane < POOL_GROUP, lo_first, hi_first)
        second = jnp.where(lane < POOL_GROUP, lo_second, hi_second)
        return jnp.where(half == 0, first, second)

    hw = [w // 2 for w in POOL_WINDOWS]
    wsum = pick(window(hw[0]), window(hw[1]), window(hw[2]), window(hw[3]))
    hwv = pick(hw[0], hw[1], hw[2], hw[3])
    cnt = (jnp.minimum(t + hwv, s) - jnp.maximum(t - hwv, 0)).astype(F32)
    dlt = wsum / cnt - x
    y = jnp.dot(dlt.astype(BF16), w_ref[0], preferred_element_type=F32)
    o_ref[...] = y * scale_ref[0]


def _pool(p_pool, w_halves, scale_halves, batch, seq):
    t = p_pool.shape[0]
    halves = POOL_WIDTH // 128
    return pl.pallas_call(
        functools.partial(_pool_kernel, halves=halves),
        grid=(batch, halves),
        in_specs=[pl.BlockSpec((seq, 128), lambda b, h: (b, h)),
                  pl.BlockSpec((1, 128, 128), lambda b, h: (h, 0, 0)),
                  pl.BlockSpec((1, 1, 128), lambda b, h: (h, 0, 0))],
        out_specs=pl.BlockSpec((seq, 128), lambda b, h: (b, h)),
        out_shape=jax.ShapeDtypeStruct((t, POOL_WIDTH), F32),
        compiler_params=_params("parallel", "parallel"),
        name="pool",
    )(p_pool, w_halves, scale_halves)


def _rotary(x, cos, sin_signed, first_half):
    width = x.shape[-1]
    half = RET_DK // 2
    swapped = jnp.where(first_half, pltpu.roll(x, width - half, 1), pltpu.roll(x, half, 1))
    return x * cos + swapped * sin_signed


def _retention_kernel(lg_ref, q_ref, k_ref, v_ref, cos_ref, sin_ref, *rest, reverse):
    if reverse:
        fwd_ref, g_ref, o_ref, state_ref = rest
    else:
        o_ref, state_ref = rest
    c = q_ref.shape[0]

    @pl.when(pl.program_id(1) == 0)
    def _():
        state_ref[...] = jnp.zeros_like(state_ref)

    lane = lax.broadcasted_iota(jnp.int32, (c, RET_QK), 1)
    first_half = (lane % RET_DK) < (RET_DK // 2)
    cos = cos_ref[...]
    sin = sin_ref[...]
    q = _rotary(q_ref[...], cos, sin, first_half)
    k = _rotary(k_ref[...], cos, sin, first_half) * (RET_DK ** -0.5)
    v = v_ref[...]

    row = lax.broadcasted_iota(jnp.int32, (c, c), 0)
    col = lax.broadcasted_iota(jnp.int32, (c, c), 1)
    pos = lax.broadcasted_iota(jnp.int32, (c, 1), 0).astype(F32)
    if reverse:
        diff = col - row
        mask = diff > 0
        q_pow = c - pos
        k_pow = pos
    else:
        diff = row - col
        mask = diff >= 0
        q_pow = pos + 1.0
        k_pow = (c - 1) - pos
    dist = jnp.where(mask, diff, 0).astype(F32)
    direction = 1 if reverse else 0

    for h in range(RET_HEADS):
        lg = lg_ref[direction, h]
        qh = q[:, h * RET_DK:(h + 1) * RET_DK]
        kh = k[:, h * RET_DK:(h + 1) * RET_DK]
        vh = v[:, h * RET_DV:(h + 1) * RET_DV]
        decay = jnp.where(mask, jnp.exp(lg * dist), 0.0)
        scores = lax.dot_general(qh.astype(BF16), kh.astype(BF16), (((1,), (1,)), ((), ())),
                                 preferred_element_type=F32) * decay
        intra = jnp.dot(scores.astype(BF16), vh, preferred_element_type=F32)
        q_dec = qh * jnp.exp(lg * q_pow)
        k_dec = kh * jnp.exp(lg * k_pow)
        state = state_ref[h]
        cross = jnp.dot(q_dec.astype(BF16), state.astype(BF16), preferred_element_type=F32)
        chunk_kv = lax.dot_general(k_dec.astype(BF16), vh, (((0,), (0,)), ((), ())),
                                   preferred_element_type=F32)
        chunk_decay = jnp.exp(lg * jnp.full((1, RET_DV), float(c), F32))
        state_ref[h] = state * chunk_decay + chunk_kv
        out = intra + cross
        sl = slice(h * RET_DV, (h + 1) * RET_DV)
        if reverse:
            y = fwd_ref[:, sl] + out
            y = y * lax.rsqrt(jnp.mean(y * y, axis=-1, keepdims=True) + EPS)
            g = g_ref[:, sl]
            o_ref[:, sl] = (g / (1.0 + jnp.exp(-g))) * y
        else:
            o_ref[:, sl] = out


def _retention(lg, r_q, r_k, r_v, r_g, cos_t, sin_t, batch, seq):
    t = r_q.shape[0]
    c = RET_CHUNK
    n = seq // c
    scratch = [pltpu.VMEM((RET_HEADS, RET_DK, RET_DV), F32)]
    smem = pl.BlockSpec(memory_space=pltpu.SMEM)

    def specs(chunk_of):
        tok = lambda w: pl.BlockSpec((c, w), lambda b, i: (b * n + chunk_of(i), 0))
        tab = pl.BlockSpec((c, RET_QK), lambda b, i: (chunk_of(i), 0))
        return tok, tab

    tok, tab = specs(lambda i: i)
    fwd = pl.pallas_call(
        functools.partial(_retention_kernel, reverse=False),
        grid=(batch, n),
        in_specs=[smem, tok(RET_QK), tok(RET_QK), tok(RET_V), tab, tab],
        out_specs=tok(RET_V),
        out_shape=jax.ShapeDtypeStruct((t, RET_V), F32),
        scratch_shapes=scratch,
        compiler_params=_params("parallel", "arbitrary"),
        name="retention_fwd",
    )(lg, r_q, r_k, r_v, cos_t, sin_t)
    tok, tab = specs(lambda i: n - 1 - i)
    return pl.pallas_call(
        functools.partial(_retention_kernel, reverse=True),
        grid=(batch, n),
        in_specs=[smem, tok(RET_QK), tok(RET_QK), tok(RET_V), tab, tab, tok(RET_V), tok(RET_V)],
        out_specs=tok(RET_V),
        out_shape=jax.ShapeDtypeStruct((t, RET_V), F32),
        scratch_shapes=scratch,
        compiler_params=_params("parallel", "arbitrary"),
        name="retention_bwd",
    )(lg, r_q, r_k, r_v, cos_t, sin_t, fwd, r_g)


def _na_kernel(q_ref, k_ref, v_ref, bias_ref, o_ref, *, rows, kr):
    step = pl.program_id(1)
    rps = q_ref.shape[0] // GRID_W
    nkeys = kr * GRID_W
    lo = kr // 2
    lane_head = lax.broadcasted_iota(jnp.int32, (GRID_W, NA_WIDTH), 1) // NA_DH
    head_lanes = [lane_head == h for h in range(NA_HEADS)]

    def body(it, carry):
        jobs = []
        for j in range(NA_ROWS_PER_ITER):
            i = it * NA_ROWS_PER_ITER + j
            r = step * rps + i
            r0 = jnp.clip(r - lo, 0, rows - kr)
            ver = jnp.where(r < lo, r, jnp.where(r > rows - kr + lo, r - (rows - kr), lo))
            key_rows = pl.ds(pl.multiple_of(r0 * GRID_W, GRID_W), nkeys)
            qq = q_ref[pl.ds(pl.multiple_of(i * GRID_W, GRID_W), GRID_W), :]
            kk = k_ref[key_rows, :]
            scores = [lax.dot_general(jnp.where(mask, qq, jnp.zeros((), qq.dtype)), kk, (((1,), (1,)), ((), ())),
                                      preferred_element_type=F32) + bias_ref[ver, h]
                      for h, mask in enumerate(head_lanes)]
            jobs.append((i, key_rows, scores))
        probs = []
        for i, key_rows, scores in jobs:
            ps = []
            for s in scores:
                p = jnp.exp(s - jnp.max(s, axis=-1, keepdims=True))
                ps.append((p / jnp.sum(p, axis=-1, keepdims=True)).astype(BF16))
            probs.append((i, key_rows, ps))
        for i, key_rows, ps in probs:
            vv = v_ref[key_rows, :]
            out = jnp.zeros((GRID_W, NA_WIDTH), F32)
            for p, mask in zip(ps, head_lanes):
                out = jnp.where(mask, jnp.dot(p, vv, preferred_element_type=F32), out)
            o_ref[pl.ds(pl.multiple_of(i * GRID_W, GRID_W), GRID_W), :] = out
        return carry

    lax.fori_loop(0, rps // NA_ROWS_PER_ITER, body, 0)


def _na_bias_table(rpb, rows, kr):
    lo = kr // 2
    reps = list(range(lo)) + [lo] + list(range(rows - kr + lo + 1, rows))
    c = np.arange(GRID_W)
    col_start = np.clip(c - NA_COLS // 2, 0, GRID_W - NA_COLS)
    col_in = (c[None, :] >= col_start[:, None]) & (c[None, :] < col_start[:, None] + NA_COLS)
    dj = np.clip(c[None, :] - c[:, None], -(NA_COLS - 1), NA_COLS - 1) + (NA_COLS - 1)
    pick = ((dj[None] == np.arange(2 * NA_COLS - 1)[:, None, None]) & col_in[None]).astype(np.float32)
    di0 = [int(np.clip(r - lo, 0, rows - kr)) - r + (NA_ROWS_MAX - 1) for r in reps]
    rpb_rows = jnp.stack([rpb[:, d0:d0 + kr, :] for d0 in di0]).astype(F32)
    bias = jnp.einsum('vhij,jck->vhcik', rpb_rows, jnp.asarray(pick), precision=lax.Precision.HIGHEST)
    bias = bias + jnp.asarray(np.where(col_in, 0.0, NEG).astype(np.float32))[None, None, :, None, :]
    return bias.reshape(len(reps), NA_HEADS, GRID_W, kr * GRID_W)


def _na(n_q, n_k, n_v, bias, batch, seq):
    t = n_q.shape[0]
    rows = seq // GRID_W
    kr = min(NA_ROWS_MAX, rows)
    rps = min(NA_ROWS_PER_STEP, rows)
    blk = rps * GRID_W
    steps = rows // rps
    return pl.pallas_call(
        functools.partial(_na_kernel, rows=rows, kr=kr),
        grid=(batch, steps),
        in_specs=[pl.BlockSpec((blk, NA_WIDTH), lambda b, i: (b * steps + i, 0)),
                  pl.BlockSpec((seq, NA_WIDTH), lambda b, i: (b, 0)),
                  pl.BlockSpec((seq, NA_WIDTH), lambda b, i: (b, 0)),
                  _const_spec(bias.shape)],
        out_specs=pl.BlockSpec((blk, NA_WIDTH), lambda b, i: (b * steps + i, 0)),
        out_shape=jax.ShapeDtypeStruct((t, NA_WIDTH), F32),
        compiler_params=_params("parallel", "arbitrary"),
        name="natten",
    )(n_q, n_k, n_v, bias)


def _merge_kernel(x_ref, yp_ref, yr_ref, yn_ref, gate_ref, wp_ref, wr_ref, wn_ref, wo_ref,
                  gf_ref, wq_ref, keys_ref, h_ref, hnt_ref, st_ref):
    d = x_ref.shape[-1]
    merged = None
    for b, (y_ref, w_ref) in enumerate(((yp_ref, wp_ref), (yr_ref, wr_ref), (yn_ref, wn_ref))):
        gate = 1.0 / (1.0 + jnp.exp(-gate_ref[:, b * d:(b + 1) * d]))
        term = gate * jnp.dot(y_ref[...].astype(BF16), w_ref[...], preferred_element_type=F32)
        merged = term if merged is None else merged + term
    h = x_ref[...] + jnp.dot(merged.astype(BF16), wo_ref[...], preferred_element_type=F32)
    h_ref[...] = h
    hn = h * lax.rsqrt(jnp.mean(h * h, axis=-1, keepdims=True) + EPS) * gf_ref[...]
    hnb = hn.astype(BF16)
    hnt_ref[...] = hn.T.astype(BF16)
    q = jnp.dot(hnb, wq_ref[...], preferred_element_type=F32).astype(BF16)
    for hh in range(PEER_HEADS):
        for p in range(2):
            col = (hh * 2 + p) * PEER_DK
            st_ref[hh, p] = lax.dot_general(keys_ref[p], q[:, col:col + PEER_DK],
                                            (((1,), (1,)), ((), ())), preferred_element_type=F32)


def _merge(x2, y_pool, y_ret, y_na, gates, wp, wr, wn, wo, g_ffn, wq, keys):
    t, d = x2.shape
    tm = MERGE_TM
    tok = lambda w: pl.BlockSpec((tm, w), lambda i: (i, 0))
    return pl.pallas_call(
        _merge_kernel,
        grid=(t // tm,),
        in_specs=[tok(d), tok(POOL_WIDTH), tok(RET_V), tok(NA_WIDTH), tok(N_BRANCH * d),
                  _const_spec(wp.shape), _const_spec(wr.shape), _const_spec(wn.shape), _const_spec(wo.shape),
                  _const_spec((1, d)), _const_spec(wq.shape), _const_spec(keys.shape)],
        out_specs=[tok(d),
                   pl.BlockSpec((d, tm), lambda i: (0, i)),
                   pl.BlockSpec((PEER_HEADS, 2, PEER_KEYS, tm), lambda i: (0, 0, 0, i))],
        out_shape=[jax.ShapeDtypeStruct((t, d), F32),
                   jax.ShapeDtypeStruct((d, t), BF16),
                   jax.ShapeDtypeStruct((PEER_HEADS, 2, PEER_KEYS, t), F32)],
        compiler_params=_params("parallel"),
        name="merge",
    )(x2, y_pool, y_ret, y_na, gates, wp, wr, wn, wo, g_ffn.reshape(1, d), wq, keys)


def _topk_exact_tile(s_ref, r2_ref, e2_ref, cc_ref, e1_ref, ds):
    nk = s_ref.shape[2]
    k_top = PEER_TOPK
    w = ds.stop - ds.start
    key = lax.broadcasted_iota(jnp.int32, (nk, w), 0)
    arow = lax.broadcasted_iota(jnp.int32, (k_top, w), 0)
    neg_inf = -jnp.inf

    def top_ranks(s):
        def body(a, carry):
            s, rank, vals = carry
            m = jnp.max(s, axis=0, keepdims=True)
            idx = jnp.min(jnp.where(s == m, key, nk), axis=0, keepdims=True)
            hit = key == idx
            return (jnp.where(hit, neg_inf, s), jnp.where(hit, a, rank), jnp.where(arow == a, m, vals))
        init = (s, jnp.full((nk, w), k_top, jnp.int32), jnp.zeros((k_top, w), F32))
        _, rank, vals = lax.fori_loop(0, k_top, body, init)
        return rank, vals

    s1 = s_ref[0, 0, :, ds]
    s2 = s_ref[0, 1, :, ds]
    r1, v1 = top_ranks(s1)
    r2, v2 = top_ranks(s2)
    top = v1[0:1] + v2[0:1]

    def merge(_, carry):
        cnt, front, z = carry
        m = jnp.max(front, axis=0, keepdims=True)
        a_star = jnp.min(jnp.where(front == m, arow, k_top), axis=0, keepdims=True)
        hit = arow == a_star
        cnt = jnp.where(hit, cnt + 1, cnt)
        nxt = jnp.sum(jnp.where(hit, cnt, 0), axis=0, keepdims=True)
        v2_next = jnp.sum(jnp.where(arow == nxt, v2, 0.0), axis=0, keepdims=True)
        cand = jnp.where(nxt < k_top, v1 + v2_next, neg_inf)
        return cnt, jnp.where(hit, cand, front), z + jnp.exp(m - top)
    init = (jnp.zeros((k_top, w), jnp.int32), v1 + v2[0:1], jnp.zeros((1, w), F32))
    cnt, _, z = lax.fori_loop(0, k_top, merge, init)

    cc = jnp.zeros((nk, w), jnp.int32)
    for a in range(k_top):
        cc = jnp.where(r1 == a, cnt[a:a + 1], cc)
    r2_ref[0, :, ds] = r2.astype(F32).astype(BF16)
    cc_ref[0, :, ds] = cc.astype(F32)
    e2_ref[0, :, ds] = jnp.exp(s2 - v2[0:1]).astype(BF16)
    e1_ref[0, :, ds] = jnp.exp(s1 - v1[0:1]) * (GELU_HALF / z)


def _topk_kernel(s_ref, r2_ref, e2_ref, cc_ref, e1_ref, v1_ref, v2_ref):
    nk = s_ref.shape[2]
    tm = s_ref.shape[3]
    k_top = PEER_TOPK
    w = TOPK_SUB
    subs = [slice(j * w, (j + 1) * w) for j in range(tm // w)]
    neg_inf = -jnp.inf
    no_tie_rank_sum = float(sum(range(k_top)) + k_top * (nk - k_top))
    mark_base, mark_step = 2.0 ** 100, 2.0 ** 96
    lowest_score = -(2.0 ** 99)

    redo = jnp.zeros((1, w), F32)
    for j, ds in enumerate(subs):
        for p, (rank_ref, vals_ref) in enumerate(((cc_ref, v1_ref), (r2_ref, v2_ref))):
            s = s_ref[0, p, :, ds]
            floor_ok = jnp.min(s, axis=0, keepdims=True) > lowest_score
            for a in range(k_top):
                m = jnp.max(s, axis=0, keepdims=True)
                s = jnp.where(s == m, -(mark_base + a * mark_step), s)
                vals_ref[a, j:j + 1, :] = m
            rank = jnp.where(s < lowest_score, s * (-1.0 / mark_step) - mark_base / mark_step, float(k_top))
            exact = floor_ok & (jnp.sum(rank, axis=0, keepdims=True) == no_tie_rank_sum)
            redo = redo + jnp.where(exact, 0.0, 1.0)
            rank_ref[0, :, ds] = rank.astype(rank_ref.dtype)

    v1 = v1_ref[...]
    v2 = v2_ref[...]
    arow = lax.broadcasted_iota(jnp.int32, v1.shape, 0).astype(F32)
    top = v1[0:1] + v2[0:1]
    cnt = jnp.zeros(v1.shape, F32)
    front = v1 + v2[0:1]
    z = jnp.zeros(top.shape, F32)
    for _ in range(k_top):
        m = jnp.max(front, axis=0, keepdims=True)
        a_star = jnp.min(jnp.where(front == m, arow, float(k_top)), axis=0, keepdims=True)
        hit = arow == a_star
        cnt = jnp.where(hit, cnt + 1.0, cnt)
        nxt = jnp.sum(jnp.where(hit, cnt, 0.0), axis=0, keepdims=True)
        v2_next = jnp.sum(jnp.where(arow == nxt, v2, 0.0), axis=0, keepdims=True)
        cand = jnp.where(nxt < float(k_top), v1 + v2_next, neg_inf)
        front = jnp.where(hit, cand, front)
        z = z + jnp.exp(m - top)
    inv_z = GELU_HALF / z

    for j, ds in enumerate(subs):
        r1 = cc_ref[0, :, ds]
        cc = jnp.zeros((nk, w), F32)
        for a in range(k_top):
            cc = jnp.where(r1 == float(a), cnt[a, j:j + 1, :], cc)
        cc_ref[0, :, ds] = cc
        e1_ref[0, :, ds] = jnp.exp(s_ref[0, 0, :, ds] - v1[0, j:j + 1, :]) * inv_z[0, j:j + 1, :]
        e2_ref[0, :, ds] = jnp.exp(s_ref[0, 1, :, ds] - v2[0, j:j + 1, :]).astype(BF16)

    @pl.when(jnp.sum(redo) > 0.0)
    def _():
        for ds in subs:
            _topk_exact_tile(s_ref, r2_ref, e2_ref, cc_ref, e1_ref, ds)


def _topk(st):
    heads, _, nk, t = st.shape
    tm = TOPK_TM
    out = pl.BlockSpec((1, nk, tm), lambda i, h: (h, 0, i))
    return pl.pallas_call(
        _topk_kernel,
        grid=(t // tm, heads),
        in_specs=[pl.BlockSpec((1, 2, nk, tm), lambda i, h: (h, 0, 0, i))],
        out_specs=[out] * 4,
        out_shape=[jax.ShapeDtypeStruct((heads, nk, t), dt) for dt in (BF16, BF16, F32, F32)],
        scratch_shapes=[pltpu.VMEM((PEER_TOPK, tm // TOPK_SUB, TOPK_SUB), F32)] * 2,
        compiler_params=_params("parallel", "parallel"),
        name="peer_topk",
    )(st)


def _peer_kernel(hnt_ref, u_ref, vt_ref, r2_in_ref, e2_in_ref, cc_ref, e1_ref, h_ref, gn_ref, o_ref,
                 acc_ref, a_ref, ht_ref, r2_ref, e2_ref, *, final_norm):
    e = pl.program_id(1)
    te = u_ref.shape[0]
    tm = hnt_ref.shape[1]
    nk = r2_ref.shape[1]
    rows_per_step = te // nk

    @pl.when(e == 0)
    def _():
        acc_ref[...] = jnp.zeros_like(acc_ref)
        r2_ref[...] = r2_in_ref[...]
        e2_ref[...] = e2_in_ref[...]

    a_ref[...] = jnp.dot(u_ref[...], hnt_ref[...], preferred_element_type=F32)
    for ii in range(rows_per_step):
        i = e * rows_per_step + ii
        rows = slice(ii * nk, (ii + 1) * nk)
        cc_rows = [cc_ref[hh, pl.ds(i, 1), :] for hh in range(PEER_HEADS)]
        e1_rows = [e1_ref[hh, pl.ds(i, 1), :] for hh in range(PEER_HEADS)]
        for lt in range(tm // PEER_LANES):
            lanes = slice(lt * PEER_LANES, (lt + 1) * PEER_LANES)
            gate = None
            for hh in range(PEER_HEADS):
                cc = jnp.broadcast_to(cc_rows[hh][:, lanes], (nk, PEER_LANES)).astype(BF16)
                e1 = jnp.broadcast_to(e1_rows[hh][:, lanes], (nk, PEER_LANES)).astype(BF16)
                term = jnp.where(r2_ref[hh, :, lanes] < cc, e2_ref[hh, :, lanes], jnp.zeros((), BF16)) * e1
                gate = term if gate is None else gate + term
            a = a_ref[rows, lanes]
            act = a * (1.0 + lax.erf(a * np.float32(1.0 / np.sqrt(2.0))))
            ht_ref[rows, lanes] = act.astype(BF16) * gate
    acc_ref[...] += jnp.dot(vt_ref[0], ht_ref[...], preferred_element_type=F32)

    @pl.when(e == pl.num_programs(1) - 1)
    def _():
        y = h_ref[...] + acc_ref[...].T
        if final_norm:
            y = y * lax.rsqrt(jnp.mean(y * y, axis=-1, keepdims=True) + EPS) * gn_ref[...]
        o_ref[...] = y


def _peer(hnt, u_bf, vt_bf, r2, e2, cc, e1, h2, g_norm, final_norm):
    d, t = hnt.shape
    n_exp = u_bf.shape[0]
    heads, nk, _ = r2.shape
    tm, te = PEER_TM, PEER_TE
    tab = pl.BlockSpec((heads, nk, tm), lambda i, e: (0, 0, i))
    return pl.pallas_call(
        functools.partial(_peer_kernel, final_norm=final_norm),
        grid=(t // tm, n_exp // te),
        in_specs=[pl.BlockSpec((d, tm), lambda i, e: (0, i)),
                  pl.BlockSpec((te, d), lambda i, e: (e, 0)),
                  pl.BlockSpec((1, d, te), lambda i, e: (e, 0, 0)),
                  tab, tab, tab, tab,
                  pl.BlockSpec((tm, d), lambda i, e: (i, 0)),
                  pl.BlockSpec((1, d), lambda i, e: (0, 0))],
        out_specs=pl.BlockSpec((tm, d), lambda i, e: (i, 0)),
        out_shape=jax.ShapeDtypeStruct((t, d), F32),
        scratch_shapes=[pltpu.VMEM((d, tm), F32), pltpu.VMEM((te, tm), F32), pltpu.VMEM((te, tm), BF16),
                        pltpu.VMEM((heads, nk, tm), BF16), pltpu.VMEM((heads, nk, tm), BF16)],
        compiler_params=_params("parallel", "arbitrary"),
        name="peer_experts",
    )(hnt, u_bf, vt_bf, r2, e2, cc, e1, h2, g_norm.reshape(1, d))


def _rotary_tables(seq):
    half = RET_DK // 2
    inv = 1.0 / (ROPE_BASE ** jnp.linspace(0.0, 1.0, half, dtype=F32))
    ang = jnp.arange(seq, dtype=F32)[:, None] * inv[None, :]
    cos, sin = jnp.cos(ang), jnp.sin(ang)
    cos_t = jnp.tile(jnp.concatenate([cos, cos], axis=-1), (1, RET_HEADS))
    sin_t = jnp.tile(jnp.concatenate([-sin, sin], axis=-1), (1, RET_HEADS))
    return cos_t, sin_t


def _pool_weights(pool_w, pool_scale):
    halves = POOL_WIDTH // 128
    per = 128 // POOL_GROUP
    blocks = []
    for hf in range(halves):
        blk = jnp.zeros((128, 128), F32)
        for g in range(per):
            sl = slice(g * POOL_GROUP, (g + 1) * POOL_GROUP)
            blk = blk.at[sl, sl].set(pool_w[hf * per + g])
        blocks.append(blk)
    return jnp.stack(blocks).astype(BF16), pool_scale.reshape(halves, 1, 128)


def kernel(x, norm_mix, w_in, pool_w, pool_scale, ret_decay, na_rpb, w_br_pool, w_br_ret, w_br_na,
           w_out, norm_ffn, peer_w_query, peer_sub_keys, peer_u, peer_v, norm_final):
    batch, seq, d = x.shape
    depth = w_in.shape[0]
    rows = seq // GRID_W
    kr = min(NA_ROWS_MAX, rows)
    cos_t, sin_t = _rotary_tables(seq)
    x2 = x.reshape(batch * seq, d)
    for l in range(depth):
        pool, r_q, r_k, r_v, r_g, n_q, n_k, n_v, gates = _inproj(x2, norm_mix[l], w_in[l].astype(BF16))
        w_halves, scale_halves = _pool_weights(pool_w[l], pool_scale[l])
        y_pool = _pool(pool, w_halves, scale_halves, batch, seq)
        lg = jax.nn.log_sigmoid(ret_decay[l].astype(F32))
        y_ret = _retention(lg, r_q, r_k, r_v, r_g, cos_t, sin_t, batch, seq)
        y_na = _na(n_q, n_k, n_v, _na_bias_table(na_rpb[l], rows, kr), batch, seq)
        h2, hnt, st = _merge(x2, y_pool, y_ret, y_na, gates,
                             w_br_pool[l].astype(BF16), w_br_ret[l].astype(BF16), w_br_na[l].astype(BF16),
                             w_out[l].astype(BF16), norm_ffn[l], peer_w_query[l].astype(BF16),
                             peer_sub_keys[l].astype(BF16))
        r2, e2, cc, e1 = _topk(st)
        vt = peer_v[l].reshape(-1, PEER_TE, d).transpose(0, 2, 1).astype(BF16)
        x2 = _peer(hnt, peer_u[l].astype(BF16), vt, r2, e2, cc, e1, h2, norm_final, final_norm=l == depth - 1)
    return x2.reshape(batch, seq, d)
```

```python
import functools

import numpy as np
import jax
import jax.numpy as jnp
from jax import lax
from jax.experimental import pallas as pl
from jax.experimental.pallas import tpu as pltpu

F32 = jnp.float32
BF16 = jnp.bfloat16

EPS = 1e-6
GRID_W = 64
NEG = -1e30

POOL_WINDOWS = (2, 4, 8, 16)
POOL_GROUP = 64
POOL_WIDTH = POOL_GROUP * len(POOL_WINDOWS)

RET_HEADS = 4
RET_DK = 64
RET_DV = 128
RET_QK = RET_HEADS * RET_DK
RET_V = RET_HEADS * RET_DV
ROPE_BASE = 10000.0

NA_HEADS = 4
NA_DH = 64
NA_WIDTH = NA_HEADS * NA_DH
NA_ROWS_MAX = 8
NA_COLS = 16

N_BRANCH = 3

PEER_KEYS = 128
PEER_HEADS = 8
PEER_DK = 128
PEER_TOPK = 16
GELU_HALF = 0.5

V7X_VMEM_LIMIT_BYTES = 56 * 1024 * 1024

INPROJ_TM = 256
RET_CHUNK = 256
NA_ROWS_PER_STEP = 8
NA_ROWS_PER_ITER = 8
MERGE_TM = 256
TOPK_TM = 1024
TOPK_SUB = 128
PEER_TM = 512
PEER_TE = 2048
PEER_LANES = 128


def _params(*sem):
    return pltpu.CompilerParams(dimension_semantics=sem, vmem_limit_bytes=V7X_VMEM_LIMIT_BYTES)


def _const_spec(shape):
    nd = len(shape)
    return pl.BlockSpec(shape, lambda *_: (0,) * nd)


def _inproj_kernel(x_ref, g_ref, w_ref, pool_ref, rq_ref, rk_ref, rv_ref, rg_ref,
                   nq_ref, nk_ref, nv_ref, gate_ref):
    x = x_ref[...]
    xn = x * lax.rsqrt(jnp.mean(x * x, axis=-1, keepdims=True) + EPS) * g_ref[...]
    xb = xn.astype(BF16)
    off = 0
    for ref, scale in ((pool_ref, None), (rq_ref, None), (rk_ref, None), (rv_ref, None), (rg_ref, None),
                       (nq_ref, NA_DH ** -0.5), (nk_ref, None), (nv_ref, None)):
        width = ref.shape[-1]
        y = jnp.dot(xb, w_ref[:, off:off + width], preferred_element_type=F32)
        if scale is not None:
            y = y * scale
        ref[...] = y.astype(ref.dtype)
        off += width
    d = x.shape[-1]
    for b in range(N_BRANCH):
        gate_ref[:, b * d:(b + 1) * d] = jnp.dot(
            xb, w_ref[:, off + b * d:off + (b + 1) * d], preferred_element_type=F32)


def _inproj(x2, g, w_in_bf):
    t, d = x2.shape
    tm = INPROJ_TM
    widths = (POOL_WIDTH, RET_QK, RET_QK, RET_V, RET_V, NA_WIDTH, NA_WIDTH, NA_WIDTH, N_BRANCH * d)
    dtypes = (F32, F32, F32, BF16, F32, BF16, BF16, BF16, F32)
    tok = lambda w: pl.BlockSpec((tm, w), lambda i: (i, 0))
    return pl.pallas_call(
        _inproj_kernel,
        grid=(t // tm,),
        in_specs=[tok(d), _const_spec((1, d)), _const_spec(w_in_bf.shape)],
        out_specs=[tok(w) for w in widths],
        out_shape=[jax.ShapeDtypeStruct((t, w), dt) for w, dt in zip(widths, dtypes)],
        compiler_params=_params("parallel"),
        name="inproj",
    )(x2, g.reshape(1, d), w_in_bf)


def _pool_kernel(p_ref, w_ref, scale_ref, o_ref, *, halves):
    x = p_ref[...]
    s = x.shape[0]
    t = lax.broadcasted_iota(jnp.int32, x.shape, 0)
    lane = lax.broadcasted_iota(jnp.int32, x.shape, 1)
    half = pl.program_id(1)

    def down(v, k):
        return jnp.where(t >= k, pltpu.roll(v, k, 0), 0.0)

    def up(v, k):
        return jnp.where(t < s - k, pltpu.roll(v, s - k, 0), 0.0)

    trail = {1: x}
    lead = {1: x}
    for m in (2, 4, 8):
        trail[m] = trail[m // 2] + down(trail[m // 2], m // 2)
        lead[m] = lead[m // 2] + up(lead[m // 2], m // 2)

    def window(hw):
        return down(trail[hw], 1) + lead[hw]

    def pick(lo_first, hi_first, lo_second, hi_second):
        first = jnp.where(lane < POOL_GROUP, lo_first, hi_first)
        second = jnp.where(lane < POOL_GROUP, lo_second, hi_second)
        return jnp.where(half == 0, first, second)

    hw = [w // 2 for w in POOL_WINDOWS]
    wsum = pick(window(hw[0]), window(hw[1]), window(hw[2]), window(hw[3]))
    hwv = pick(hw[0], hw[1], hw[2], hw[3])
    cnt = (jnp.minimum(t + hwv, s) - jnp.maximum(t - hwv, 0)).astype(F32)
    dlt = wsum / cnt - x
    y = jnp.dot(dlt.astype(BF16), w_ref[0], preferred_element_type=F32)
    o_ref[...] = y * scale_ref[0]


def _pool(p_pool, w_halves, scale_halves, batch, seq):
    t = p_pool.shape[0]
    halves = POOL_WIDTH // 128
    return pl.pallas_call(
        functools.partial(_pool_kernel, halves=halves),
        grid=(batch, halves),
        in_specs=[pl.BlockSpec((seq, 128), lambda b, h: (b, h)),
                  pl.BlockSpec((1, 128, 128), lambda b, h: (h, 0, 0)),
                  pl.BlockSpec((1, 1, 128), lambda b, h: (h, 0, 0))],
        out_specs=pl.BlockSpec((seq, 128), lambda b, h: (b, h)),
        out_shape=jax.ShapeDtypeStruct((t, POOL_WIDTH), F32),
        compiler_params=_params("parallel", "parallel"),
        name="pool",
    )(p_pool, w_halves, scale_halves)


def _rotary(x, cos, sin_signed, first_half):
    width = x.shape[-1]
    half = RET_DK // 2
    swapped = jnp.where(first_half, pltpu.roll(x, width - half, 1), pltpu.roll(x, half, 1))
    return x * cos + swapped * sin_signed


def _retention_kernel(lg_ref, q_ref, k_ref, v_ref, cos_ref, sin_ref, *rest, reverse):
    if reverse:
        fwd_ref, g_ref, o_ref, state_ref = rest
    else:
        o_ref, state_ref = rest
    c = q_ref.shape[0]

    @pl.when(pl.program_id(1) == 0)
    def _():
        state_ref[...] = jnp.zeros_like(state_ref)

    lane = lax.broadcasted_iota(jnp.int32, (c, RET_QK), 1)
    first_half = (lane % RET_DK) < (RET_DK // 2)
    cos = cos_ref[...]
    sin = sin_ref[...]
    q = _rotary(q_ref[...], cos, sin, first_half)
    k = _rotary(k_ref[...], cos, sin, first_half) * (RET_DK ** -0.5)
    v = v_ref[...]

    row = lax.broadcasted_iota(jnp.int32, (c, c), 0)
    col = lax.broadcasted_iota(jnp.int32, (c, c), 1)
    pos = lax.broadcasted_iota(jnp.int32, (c, 1), 0).astype(F32)
    if reverse:
        diff = col - row
        mask = diff > 0
        q_pow = c - pos
        k_pow = pos
    else:
        diff = row - col
        mask = diff >= 0
        q_pow = pos + 1.0
        k_pow = (c - 1) - pos
    dist = jnp.where(mask, diff, 0).astype(F32)
    direction = 1 if reverse else 0

    for h in range(RET_HEADS):
        lg = lg_ref[direction, h]
        qh = q[:, h * RET_DK:(h + 1) * RET_DK]
        kh = k[:, h * RET_DK:(h + 1) * RET_DK]
        vh = v[:, h * RET_DV:(h + 1) * RET_DV]
        decay = jnp.where(mask, jnp.exp(lg * dist), 0.0)
        scores = lax.dot_general(qh.astype(BF16), kh.astype(BF16), (((1,), (1,)), ((), ())),
                                 preferred_element_type=F32) * decay
        intra = jnp.dot(scores.astype(BF16), vh, preferred_element_type=F32)
        q_dec = qh * jnp.exp(lg * q_pow)
        k_dec = kh * jnp.exp(lg * k_pow)
        state = state_ref[h]
        cross = jnp.dot(q_dec.astype(BF16), state.astype(BF16), preferred_element_type=F32)
        chunk_kv = lax.dot_general(k_dec.astype(BF16), vh, (((0,), (0,)), ((), ())),
                                   preferred_element_type=F32)
        chunk_decay = jnp.exp(lg * jnp.full((1, RET_DV), float(c), F32))
        state_ref[h] = state * chunk_decay + chunk_kv
        out = intra + cross
        sl = slice(h * RET_DV, (h + 1) * RET_DV)
        if reverse:
            y = fwd_ref[:, sl] + out
            y = y * lax.rsqrt(jnp.mean(y * y, axis=-1, keepdims=True) + EPS)
            g = g_ref[:, sl]
            o_ref[:, sl] = (g / (1.0 + jnp.exp(-g))) * y
        else:
            o_ref[:, sl] = out


def _retention(lg, r_q, r_k, r_v, r_g, cos_t, sin_t, batch, seq):
    t = r_q.shape[0]
    c = RET_CHUNK
    n = seq // c
    scratch = [pltpu.VMEM((RET_HEADS, RET_DK, RET_DV), F32)]
    smem = pl.BlockSpec(memory_space=pltpu.SMEM)

    def specs(chunk_of):
        tok = lambda w: pl.BlockSpec((c, w), lambda b, i: (b * n + chunk_of(i), 0))
        tab = pl.BlockSpec((c, RET_QK), lambda b, i: (chunk_of(i), 0))
        return tok, tab

    tok, tab = specs(lambda i: i)
    fwd = pl.pallas_call(
        functools.partial(_retention_kernel, reverse=False),
        grid=(batch, n),
        in_specs=[smem, tok(RET_QK), tok(RET_QK), tok(RET_V), tab, tab],
        out_specs=tok(RET_V),
        out_shape=jax.ShapeDtypeStruct((t, RET_V), F32),
        scratch_shapes=scratch,
        compiler_params=_params("parallel", "arbitrary"),
        name="retention_fwd",
    )(lg, r_q, r_k, r_v, cos_t, sin_t)
    tok, tab = specs(lambda i: n - 1 - i)
    return pl.pallas_call(
        functools.partial(_retention_kernel, reverse=True),
        grid=(batch, n),
        in_specs=[smem, tok(RET_QK), tok(RET_QK), tok(RET_V), tab, tab, tok(RET_V), tok(RET_V)],
        out_specs=tok(RET_V),
        out_shape=jax.ShapeDtypeStruct((t, RET_V), F32),
        scratch_shapes=scratch,
        compiler_params=_params("parallel", "arbitrary"),
        name="retention_bwd",
    )(lg, r_q, r_k, r_v, cos_t, sin_t, fwd, r_g)


def _na_kernel(q_ref, k_ref, v_ref, bias_ref, o_ref, *, rows, kr):
    step = pl.program_id(1)
    rps = q_ref.shape[0] // GRID_W
    nkeys = kr * GRID_W
    lo = kr // 2
    lane_head = lax.broadcasted_iota(jnp.int32, (GRID_W, NA_WIDTH), 1) // NA_DH
    head_lanes = [lane_head == h for h in range(NA_HEADS)]

    def body(it, carry):
        jobs = []
        for j in range(NA_ROWS_PER_ITER):
            i = it * NA_ROWS_PER_ITER + j
            r = step * rps + i
            r0 = jnp.clip(r - lo, 0, rows - kr)
            ver = jnp.where(r < lo, r, jnp.where(r > rows - kr + lo, r - (rows - kr), lo))
            key_rows = pl.ds(pl.multiple_of(r0 * GRID_W, GRID_W), nkeys)
            qq = q_ref[pl.ds(pl.multiple_of(i * GRID_W, GRID_W), GRID_W), :]
            kk = k_ref[key_rows, :]
            scores = [lax.dot_general(jnp.where(mask, qq, jnp.zeros((), qq.dtype)), kk, (((1,), (1,)), ((), ())),
                                      preferred_element_type=F32) + bias_ref[ver, h]
                      for h, mask in enumerate(head_lanes)]
            jobs.append((i, key_rows, scores))
        probs = []
        for i, key_rows, scores in jobs:
            ps = []
            for s in scores:
                p = jnp.exp(s - jnp.max(s, axis=-1, keepdims=True))
                ps.append((p / jnp.sum(p, axis=-1, keepdims=True)).astype(BF16))
            probs.append((i, key_rows, ps))
        for i, key_rows, ps in probs:
            vv = v_ref[key_rows, :]
            out = jnp.zeros((GRID_W, NA_WIDTH), F32)
            for p, mask in zip(ps, head_lanes):
                out = jnp.where(mask, jnp.dot(p, vv, preferred_element_type=F32), out)
            o_ref[pl.ds(pl.multiple_of(i * GRID_W, GRID_W), GRID_W), :] = out
        return carry

    lax.fori_loop(0, rps // NA_ROWS_PER_ITER, body, 0)


def _na_bias_table(rpb, rows, kr):
    lo = kr // 2
    reps = list(range(lo)) + [lo] + list(range(rows - kr + lo + 1, rows))
    c = np.arange(GRID_W)
    col_start = np.clip(c - NA_COLS // 2, 0, GRID_W - NA_COLS)
    col_in = (c[None, :] >= col_start[:, None]) & (c[None, :] < col_start[:, None] + NA_COLS)
    dj = np.clip(c[None, :] - c[:, None], -(NA_COLS - 1), NA_COLS - 1) + (NA_COLS - 1)
    pick = ((dj[None] == np.arange(2 * NA_COLS - 1)[:, None, None]) & col_in[None]).astype(np.float32)
    di0 = [int(np.clip(r - lo, 0, rows - kr)) - r + (NA_ROWS_MAX - 1) for r in reps]
    rpb_rows = jnp.stack([rpb[:, d0:d0 + kr, :] for d0 in di0]).astype(F32)
    bias = jnp.einsum('vhij,jck->vhcik', rpb_rows, jnp.asarray(pick), precision=lax.Precision.HIGHEST)
    bias = bias + jnp.asarray(np.where(col_in, 0.0, NEG).astype(np.float32))[None, None, :, None, :]
    return bias.reshape(len(reps), NA_HEADS, GRID_W, kr * GRID_W)


def _na(n_q, n_k, n_v, bias, batch, seq):
    t = n_q.shape[0]
    rows = seq // GRID_W
    kr = min(NA_ROWS_MAX, rows)
    rps = min(NA_ROWS_PER_STEP, rows)
    blk = rps * GRID_W
    steps = rows // rps
    return pl.pallas_call(
        functools.partial(_na_kernel, rows=rows, kr=kr),
        grid=(batch, steps),
        in_specs=[pl.BlockSpec((blk, NA_WIDTH), lambda b, i: (b * steps + i, 0)),
                  pl.BlockSpec((seq, NA_WIDTH), lambda b, i: (b, 0)),
                  pl.BlockSpec((seq, NA_WIDTH), lambda b, i: (b, 0)),
                  _const_spec(bias.shape)],
        out_specs=pl.BlockSpec((blk, NA_WIDTH), lambda b, i: (b * steps + i, 0)),
        out_shape=jax.ShapeDtypeStruct((t, NA_WIDTH), F32),
        compiler_params=_params("parallel", "arbitrary"),
        name="natten",
    )(n_q, n_k, n_v, bias)


def _merge_kernel(x_ref, yp_ref, yr_ref, yn_ref, gate_ref, wp_ref, wr_ref, wn_ref, wo_ref,
                  gf_ref, wq_ref, keys_ref, h_ref, hnt_ref, st_ref):
    d = x_ref.shape[-1]
    merged = None
    for b, (y_ref, w_ref) in enumerate(((yp_ref, wp_ref), (yr_ref, wr_ref), (yn_ref, wn_ref))):
        gate = 1.0 / (1.0 + jnp.exp(-gate_ref[:, b * d:(b + 1) * d]))
        term = gate * jnp.dot(y_ref[...].astype(BF16), w_ref[...], preferred_element_type=F32)
        merged = term if merged is None else merged + term
    h = x_ref[...] + jnp.dot(merged.astype(BF16), wo_ref[...], preferred_element_type=F32)
    h_ref[...] = h
    hn = h * lax.rsqrt(jnp.mean(h * h, axis=-1, keepdims=True) + EPS) * gf_ref[...]
    hnb = hn.astype(BF16)
    hnt_ref[...] = hn.T.astype(BF16)
    q = jnp.dot(hnb, wq_ref[...], preferred_element_type=F32).astype(BF16)
    for hh in range(PEER_HEADS):
        for p in range(2):
            col = (hh * 2 + p) * PEER_DK
            st_ref[hh, p] = lax.dot_general(keys_ref[p], q[:, col:col + PEER_DK],
                                            (((1,), (1,)), ((), ())), preferred_element_type=F32)


def _merge(x2, y_pool, y_ret, y_na, gates, wp, wr, wn, wo, g_ffn, wq, keys):
    t, d = x2.shape
    tm = MERGE_TM
    tok = lambda w: pl.BlockSpec((tm, w), lambda i: (i, 0))
    return pl.pallas_call(
        _merge_kernel,
        grid=(t // tm,),
        in_specs=[tok(d), tok(POOL_WIDTH), tok(RET_V), tok(NA_WIDTH), tok(N_BRANCH * d),
                  _const_spec(wp.shape), _const_spec(wr.shape), _const_spec(wn.shape), _const_spec(wo.shape),
                  _const_spec((1, d)), _const_spec(wq.shape), _const_spec(keys.shape)],
        out_specs=[tok(d),
                   pl.BlockSpec((d, tm), lambda i: (0, i)),
                   pl.BlockSpec((PEER_HEADS, 2, PEER_KEYS, tm), lambda i: (0, 0, 0, i))],
        out_shape=[jax.ShapeDtypeStruct((t, d), F32),
                   jax.ShapeDtypeStruct((d, t), BF16),
                   jax.ShapeDtypeStruct((PEER_HEADS, 2, PEER_KEYS, t), F32)],
        compiler_params=_params("parallel"),
        name="merge",
    )(x2, y_pool, y_ret, y_na, gates, wp, wr, wn, wo, g_ffn.reshape(1, d), wq, keys)


def _topk_exact_tile(s_ref, r2_ref, e2_ref, cc_ref, e1_ref, ds):
    nk = s_ref.shape[2]
    k_top = PEER_TOPK
    w = ds.stop - ds.start
    key = lax.broadcasted_iota(jnp.int32, (nk, w), 0)
    arow = lax.broadcasted_iota(jnp.int32, (k_top, w), 0)
    neg_inf = -jnp.inf

    def top_ranks(s):
        def body(a, carry):
            s, rank, vals = carry
            m = jnp.max(s, axis=0, keepdims=True)
            idx = jnp.min(jnp.where(s == m, key, nk), axis=0, keepdims=True)
            hit = key == idx
            return (jnp.where(hit, neg_inf, s), jnp.where(hit, a, rank), jnp.where(arow == a, m, vals))
        init = (s, jnp.full((nk, w), k_top, jnp.int32), jnp.zeros((k_top, w), F32))
        _, rank, vals = lax.fori_loop(0, k_top, body, init)
        return rank, vals

    s1 = s_ref[0, 0, :, ds]
    s2 = s_ref[0, 1, :, ds]
    r1, v1 = top_ranks(s1)
    r2, v2 = top_ranks(s2)
    top = v1[0:1] + v2[0:1]

    def merge(_, carry):
        cnt, front, z = carry
        m = jnp.max(front, axis=0, keepdims=True)
        a_star = jnp.min(jnp.where(front == m, arow, k_top), axis=0, keepdims=True)
        hit = arow == a_star
        cnt = jnp.where(hit, cnt + 1, cnt)
        nxt = jnp.sum(jnp.where(hit, cnt, 0), axis=0, keepdims=True)
        v2_next = jnp.sum(jnp.where(arow == nxt, v2, 0.0), axis=0, keepdims=True)
        cand = jnp.where(nxt < k_top, v1 + v2_next, neg_inf)
        return cnt, jnp.where(hit, cand, front), z + jnp.exp(m - top)
    init = (jnp.zeros((k_top, w), jnp.int32), v1 + v2[0:1], jnp.zeros((1, w), F32))
    cnt, _, z = lax.fori_loop(0, k_top, merge, init)

    cc = jnp.zeros((nk, w), jnp.int32)
    for a in range(k_top):
        cc = jnp.where(r1 == a, cnt[a:a + 1], cc)
    r2_ref[0, :, ds] = r2.astype(F32).astype(BF16)
    cc_ref[0, :, ds] = cc.astype(F32)
    e2_ref[0, :, ds] = jnp.exp(s2 - v2[0:1]).astype(BF16)
    e1_ref[0, :, ds] = jnp.exp(s1 - v1[0:1]) * (GELU_HALF / z)


def _topk_kernel(s_ref, r2_ref, e2_ref, cc_ref, e1_ref, v1_ref, v2_ref):
    nk = s_ref.shape[2]
    tm = s_ref.shape[3]
    k_top = PEER_TOPK
    w = TOPK_SUB
    subs = [slice(j * w, (j + 1) * w) for j in range(tm // w)]
    neg_inf = -jnp.inf
    no_tie_rank_sum = float(sum(range(k_top)) + k_top * (nk - k_top))
    mark_base, mark_step = 2.0 ** 100, 2.0 ** 96
    lowest_score = -(2.0 ** 99)

    redo = jnp.zeros((1, w), F32)
    for j, ds in enumerate(subs):
        for p, (rank_ref, vals_ref) in enumerate(((cc_ref, v1_ref), (r2_ref, v2_ref))):
            s = s_ref[0, p, :, ds]
            floor_ok = jnp.min(s, axis=0, keepdims=True) > lowest_score
            for a in range(k_top):
                m = jnp.max(s, axis=0, keepdims=True)
                s = jnp.where(s == m, -(mark_base + a * mark_step), s)
                vals_ref[a, j:j + 1, :] = m
            rank = jnp.where(s < lowest_score, s * (-1.0 / mark_step) - mark_base / mark_step, float(k_top))
            exact = floor_ok & (jnp.sum(rank, axis=0, keepdims=True) == no_tie_rank_sum)
            redo = redo + jnp.where(exact, 0.0, 1.0)
            rank_ref[0, :, ds] = rank.astype(rank_ref.dtype)

    v1 = v1_ref[...]
    v2 = v2_ref[...]
    arow = lax.broadcasted_iota(jnp.int32, v1.shape, 0).astype(F32)
    top = v1[0:1] + v2[0:1]
    cnt = jnp.zeros(v1.shape, F32)
    front = v1 + v2[0:1]
    z = jnp.zeros(top.shape, F32)
    for _ in range(k_top):
        m = jnp.max(front, axis=0, keepdims=True)
        a_star = jnp.min(jnp.where(front == m, arow, float(k_top)), axis=0, keepdims=True)
        hit = arow == a_star
        cnt = jnp.where(hit, cnt + 1.0, cnt)
        nxt = jnp.sum(jnp.where(hit, cnt, 0.0), axis=0, keepdims=True)
        v2_next = jnp.sum(jnp.where(arow == nxt, v2, 0.0), axis=0, keepdims=True)
        cand = jnp.where(nxt < float(k_top), v1 + v2_next, neg_inf)
        front = jnp.where(hit, cand, front)
        z = z + jnp.exp(m - top)
    inv_z = GELU_HALF / z

    for j, ds in enumerate(subs):
        r1 = cc_ref[0, :, ds]
        cc = jnp.zeros((nk, w), F32)
        for a in range(k_top):
            cc = jnp.where(r1 == float(a), cnt[a, j:j + 1, :], cc)
        cc_ref[0, :, ds] = cc
        e1_ref[0, :, ds] = jnp.exp(s_ref[0, 0, :, ds] - v1[0, j:j + 1, :]) * inv_z[0, j:j + 1, :]
        e2_ref[0, :, ds] = jnp.exp(s_ref[0, 1, :, ds] - v2[0, j:j + 1, :]).astype(BF16)

    @pl.when(jnp.sum(redo) > 0.0)
    def _():
        for ds in subs:
            _topk_exact_tile(s_ref, r2_ref, e2_ref, cc_ref, e1_ref, ds)


def _topk(st):
    heads, _, nk, t = st.shape
    tm = TOPK_TM
    out = pl.BlockSpec((1, nk, tm), lambda i, h: (h, 0, i))
    return pl.pallas_call(
        _topk_kernel,
        grid=(t // tm, heads),
        in_specs=[pl.BlockSpec((1, 2, nk, tm), lambda i, h: (h, 0, 0, i))],
        out_specs=[out] * 4,
        out_shape=[jax.ShapeDtypeStruct((heads, nk, t), dt) for dt in (BF16, BF16, F32, F32)],
        scratch_shapes=[pltpu.VMEM((PEER_TOPK, tm // TOPK_SUB, TOPK_SUB), F32)] * 2,
        compiler_params=_params("parallel", "parallel"),
        name="peer_topk",
    )(st)


def _peer_kernel(hnt_ref, u_ref, vt_ref, r2_in_ref, e2_in_ref, cc_ref, e1_ref, h_ref, gn_ref, o_ref,
                 acc_ref, a_ref, ht_ref, r2_ref, e2_ref, *, final_norm):
    e = pl.program_id(1)
    te = u_ref.shape[0]
    tm = hnt_ref.shape[1]
    nk = r2_ref.shape[1]
    rows_per_step = te // nk

    @pl.when(e == 0)
    def _():
        acc_ref[...] = jnp.zeros_like(acc_ref)
        r2_ref[...] = r2_in_ref[...]
        e2_ref[...] = e2_in_ref[...]

    a_ref[...] = lax.dot_general(u_ref[...], hnt_ref[...], (((1,), (0,)), ((), ())),
                                 preferred_element_type=F32)
    for ii in range(rows_per_step):
        rows = slice(ii * nk, (ii + 1) * nk)
        cc_rows = [cc_ref[hh, ii:ii + 1, :] for hh in range(PEER_HEADS)]
        e1_rows = [e1_ref[hh, ii:ii + 1, :] for hh in range(PEER_HEADS)]
        for lt in range(tm // PEER_LANES):
            lanes = slice(lt * PEER_LANES, (lt + 1) * PEER_LANES)
            gate = None
            for hh in range(PEER_HEADS):
                cc = jnp.broadcast_to(cc_rows[hh][:, lanes], (nk, PEER_LANES)).astype(BF16)
                e1 = jnp.broadcast_to(e1_rows[hh][:, lanes], (nk, PEER_LANES)).astype(BF16)
                term = jnp.where(r2_ref[hh, :, lanes] < cc, e2_ref[hh, :, lanes], jnp.zeros((), BF16)) * e1
                gate = term if gate is None else gate + term
            a = a_ref[rows, lanes]
            act = a * (1.0 + lax.erf(a * np.float32(1.0 / np.sqrt(2.0))))
            ht_ref[rows, lanes] = act.astype(BF16) * gate
    acc_ref[...] += jnp.dot(vt_ref[0], ht_ref[...], preferred_element_type=F32)

    @pl.when(e == pl.num_programs(1) - 1)
    def _():
        y = h_ref[...] + acc_ref[...].T
        if final_norm:
            y = y * lax.rsqrt(jnp.mean(y * y, axis=-1, keepdims=True) + EPS) * gn_ref[...]
        o_ref[...] = y


def _peer(hnt, u_bf, vt_bf, r2, e2, cc, e1, h2, g_norm, final_norm):
    d, t = hnt.shape
    n_exp = u_bf.shape[0]
    heads, nk, _ = r2.shape
    tm, te = PEER_TM, PEER_TE
    tab = pl.BlockSpec((heads, nk, tm), lambda i, e: (0, 0, i))
    row_tab = pl.BlockSpec((heads, te // nk, tm), lambda i, e: (0, e, i))
    return pl.pallas_call(
        functools.partial(_peer_kernel, final_norm=final_norm),
        grid=(t // tm, n_exp // te),
        in_specs=[pl.BlockSpec((d, tm), lambda i, e: (0, i)),
                  pl.BlockSpec((te, d), lambda i, e: (e, 0)),
                  pl.BlockSpec((1, d, te), lambda i, e: (e, 0, 0)),
                  tab, tab, row_tab, row_tab,
                  pl.BlockSpec((tm, d), lambda i, e: (i, 0)),
                  pl.BlockSpec((1, d), lambda i, e: (0, 0))],
        out_specs=pl.BlockSpec((tm, d), lambda i, e: (i, 0)),
        out_shape=jax.ShapeDtypeStruct((t, d), F32),
        scratch_shapes=[pltpu.VMEM((d, tm), F32), pltpu.VMEM((te, tm), F32), pltpu.VMEM((te, tm), BF16),
                        pltpu.VMEM((heads, nk, tm), BF16), pltpu.VMEM((heads, nk, tm), BF16)],
        compiler_params=_params("parallel", "arbitrary"),
        name="peer_experts",
    )(hnt, u_bf, vt_bf, r2, e2, cc, e1, h2, g_norm.reshape(1, d))


def _rotary_tables(seq):
    half = RET_DK // 2
    inv = 1.0 / (ROPE_BASE ** jnp.linspace(0.0, 1.0, half, dtype=F32))
    ang = jnp.arange(seq, dtype=F32)[:, None] * inv[None, :]
    cos, sin = jnp.cos(ang), jnp.sin(ang)
    cos_t = jnp.tile(jnp.concatenate([cos, cos], axis=-1), (1, RET_HEADS))
    sin_t = jnp.tile(jnp.concatenate([-sin, sin], axis=-1), (1, RET_HEADS))
    return cos_t, sin_t


def _pool_weights(pool_w, pool_scale):
    halves = POOL_WIDTH // 128
    per = 128 // POOL_GROUP
    blocks = []
    for hf in range(halves):
        blk = jnp.zeros((128, 128), F32)
        for g in range(per):
            sl = slice(g * POOL_GROUP, (g + 1) * POOL_GROUP)
            blk = blk.at[sl, sl].set(pool_w[hf * per + g])
        blocks.append(blk)
    return jnp.stack(blocks).astype(BF16), pool_scale.reshape(halves, 1, 128)


def kernel(x, norm_mix, w_in, pool_w, pool_scale, ret_decay, na_rpb, w_br_pool, w_br_ret, w_br_na,
           w_out, norm_ffn, peer_w_query, peer_sub_keys, peer_u, peer_v, norm_final):
    batch, seq, d = x.shape
    depth = w_in.shape[0]
    rows = seq // GRID_W
    kr = min(NA_ROWS_MAX, rows)
    cos_t, sin_t = _rotary_tables(seq)
    x2 = x.reshape(batch * seq, d)
    for l in range(depth):
        pool, r_q, r_k, r_v, r_g, n_q, n_k, n_v, gates = _inproj(x2, norm_mix[l], w_in[l].astype(BF16))
        w_halves, scale_halves = _pool_weights(pool_w[l], pool_scale[l])
        y_pool = _pool(pool, w_halves, scale_halves, batch, seq)
        lg = jax.nn.log_sigmoid(ret_decay[l].astype(F32))
        y_ret = _retention(lg, r_q, r_k, r_v, r_g, cos_t, sin_t, batch, seq)
        y_na = _na(n_q, n_k, n_v, _na_bias_table(na_rpb[l], rows, kr), batch, seq)
        h2, hnt, st = _merge(x2, y_pool, y_ret, y_na, gates,
                             w_br_pool[l].astype(BF16), w_br_ret[l].astype(BF16), w_br_na[l].astype(BF16),
                             w_out[l].astype(BF16), norm_ffn[l], peer_w_query[l].astype(BF16),
                             peer_sub_keys[l].astype(BF16))
        r2, e2, cc, e1 = _topk(st)
        vt = peer_v[l].reshape(-1, PEER_TE, d).transpose(0, 2, 1).astype(BF16)
        x2 = _peer(hnt, peer_u[l], vt, r2, e2, cc, e1, h2, norm_final, final_norm=l == depth - 1)
    return x2.reshape(batch, seq, d)
```

```python
import functools

import numpy as np
import jax
import jax.numpy as jnp
from jax import lax
from jax.experimental import pallas as pl
from jax.experimental.pallas import tpu as pltpu

F32 = jnp.float32
BF16 = jnp.bfloat16

EPS = 1e-6
GRID_W = 64
NEG = -1e30

POOL_WINDOWS = (2, 4, 8, 16)
POOL_GROUP = 64
POOL_WIDTH = POOL_GROUP * len(POOL_WINDOWS)

RET_HEADS = 4
RET_DK = 64
RET_DV = 128
RET_QK = RET_HEADS * RET_DK
RET_V = RET_HEADS * RET_DV
ROPE_BASE = 10000.0

NA_HEADS = 4
NA_DH = 64
NA_WIDTH = NA_HEADS * NA_DH
NA_ROWS_MAX = 8
NA_COLS = 16

N_BRANCH = 3

PEER_KEYS = 128
PEER_HEADS = 8
PEER_DK = 128
PEER_TOPK = 16
GELU_HALF = 0.5

V7X_VMEM_LIMIT_BYTES = 56 * 1024 * 1024

INPROJ_TM = 256
RET_CHUNK = 256
NA_ROWS_PER_STEP = 8
NA_ROWS_PER_ITER = 8
MERGE_TM = 256
TOPK_TM = 1024
TOPK_SUB = 128
PEER_TM = 512
PEER_TE = 1024
PEER_LANES = 128


def _params(*sem):
    return pltpu.CompilerParams(dimension_semantics=sem, vmem_limit_bytes=V7X_VMEM_LIMIT_BYTES)


def _const_spec(shape):
    nd = len(shape)
    return pl.BlockSpec(shape, lambda *_: (0,) * nd)


def _inproj_kernel(x_ref, g_ref, w_ref, pool_ref, rq_ref, rk_ref, rv_ref, rg_ref,
                   nq_ref, nk_ref, nv_ref, gate_ref):
    x = x_ref[...]
    xn = x * lax.rsqrt(jnp.mean(x * x, axis=-1, keepdims=True) + EPS) * g_ref[...]
    xb = xn.astype(BF16)
    off = 0
    for ref, scale in ((pool_ref, None), (rq_ref, None), (rk_ref, None), (rv_ref, None), (rg_ref, None),
                       (nq_ref, NA_DH ** -0.5), (nk_ref, None), (nv_ref, None)):
        width = ref.shape[-1]
        y = jnp.dot(xb, w_ref[:, off:off + width], preferred_element_type=F32)
        if scale is not None:
            y = y * scale
        ref[...] = y.astype(ref.dtype)
        off += width
    d = x.shape[-1]
    for b in range(N_BRANCH):
        gate_ref[:, b * d:(b + 1) * d] = jnp.dot(
            xb, w_ref[:, off + b * d:off + (b + 1) * d], preferred_element_type=F32)


def _inproj(x2, g, w_in_bf):
    t, d = x2.shape
    tm = INPROJ_TM
    widths = (POOL_WIDTH, RET_QK, RET_QK, RET_V, RET_V, NA_WIDTH, NA_WIDTH, NA_WIDTH, N_BRANCH * d)
    dtypes = (F32, F32, F32, BF16, F32, BF16, BF16, BF16, F32)
    tok = lambda w: pl.BlockSpec((tm, w), lambda i: (i, 0))
    return pl.pallas_call(
        _inproj_kernel,
        grid=(t // tm,),
        in_specs=[tok(d), _const_spec((1, d)), _const_spec(w_in_bf.shape)],
        out_specs=[tok(w) for w in widths],
        out_shape=[jax.ShapeDtypeStruct((t, w), dt) for w, dt in zip(widths, dtypes)],
        compiler_params=_params("parallel"),
        name="inproj",
    )(x2, g.reshape(1, d), w_in_bf)


def _pool_kernel(p_ref, w_ref, scale_ref, o_ref, *, halves):
    x = p_ref[...]
    s = x.shape[0]
    t = lax.broadcasted_iota(jnp.int32, x.shape, 0)
    lane = lax.broadcasted_iota(jnp.int32, x.shape, 1)
    half = pl.program_id(1)

    def down(v, k):
        return jnp.where(t >= k, pltpu.roll(v, k, 0), 0.0)

    def up(v, k):
        return jnp.where(t < s - k, pltpu.roll(v, s - k, 0), 0.0)

    trail = {1: x}
    lead = {1: x}
    for m in (2, 4, 8):
        trail[m] = trail[m // 2] + down(trail[m // 2], m // 2)
        lead[m] = lead[m // 2] + up(lead[m // 2], m // 2)

    def window(hw):
        return down(trail[hw], 1) + lead[hw]

    def pick(lo_first, hi_first, lo_second, hi_second):
        first = jnp.where(lane < POOL_GROUP, lo_first, hi_first)
        second = jnp.where(lane < POOL_GROUP, lo_second, hi_second)
        return jnp.where(half == 0, first, second)

    hw = [w // 2 for w in POOL_WINDOWS]
    wsum = pick(window(hw[0]), window(hw[1]), window(hw[2]), window(hw[3]))
    hwv = pick(hw[0], hw[1], hw[2], hw[3])
    cnt = (jnp.minimum(t + hwv, s) - jnp.maximum(t - hwv, 0)).astype(F32)
    dlt = wsum / cnt - x
    y = jnp.dot(dlt.astype(BF16), w_ref[0], preferred_element_type=F32)
    o_ref[...] = y * scale_ref[0]


def _pool(p_pool, w_halves, scale_halves, batch, seq):
    t = p_pool.shape[0]
    halves = POOL_WIDTH // 128
    return pl.pallas_call(
        functools.partial(_pool_kernel, halves=halves),
        grid=(batch, halves),
        in_specs=[pl.BlockSpec((seq, 128), lambda b, h: (b, h)),
                  pl.BlockSpec((1, 128, 128), lambda b, h: (h, 0, 0)),
                  pl.BlockSpec((1, 1, 128), lambda b, h: (h, 0, 0))],
        out_specs=pl.BlockSpec((seq, 128), lambda b, h: (b, h)),
        out_shape=jax.ShapeDtypeStruct((t, POOL_WIDTH), F32),
        compiler_params=_params("parallel", "parallel"),
        name="pool",
    )(p_pool, w_halves, scale_halves)


def _rotary(x, cos, sin_signed, first_half):
    width = x.shape[-1]
    half = RET_DK // 2
    swapped = jnp.where(first_half, pltpu.roll(x, width - half, 1), pltpu.roll(x, half, 1))
    return x * cos + swapped * sin_signed


def _retention_kernel(lg_ref, q_ref, k_ref, v_ref, cos_ref, sin_ref, *rest, reverse):
    if reverse:
        fwd_ref, g_ref, o_ref, state_ref = rest
    else:
        o_ref, state_ref = rest
    c = q_ref.shape[0]

    @pl.when(pl.program_id(1) == 0)
    def _():
        state_ref[...] = jnp.zeros_like(state_ref)

    lane = lax.broadcasted_iota(jnp.int32, (c, RET_QK), 1)
    first_half = (lane % RET_DK) < (RET_DK // 2)
    cos = cos_ref[...]
    sin = sin_ref[...]
    q = _rotary(q_ref[...], cos, sin, first_half)
    k = _rotary(k_ref[...], cos, sin, first_half) * (RET_DK ** -0.5)
    v = v_ref[...]

    row = lax.broadcasted_iota(jnp.int32, (c, c), 0)
    col = lax.broadcasted_iota(jnp.int32, (c, c), 1)
    pos = lax.broadcasted_iota(jnp.int32, (c, 1), 0).astype(F32)
    if reverse:
        diff = col - row
        mask = diff > 0
        q_pow = c - pos
        k_pow = pos
    else:
        diff = row - col
        mask = diff >= 0
        q_pow = pos + 1.0
        k_pow = (c - 1) - pos
    dist = jnp.where(mask, diff, 0).astype(F32)
    direction = 1 if reverse else 0

    for h in range(RET_HEADS):
        lg = lg_ref[direction, h]
        qh = q[:, h * RET_DK:(h + 1) * RET_DK]
        kh = k[:, h * RET_DK:(h + 1) * RET_DK]
        vh = v[:, h * RET_DV:(h + 1) * RET_DV]
        decay = jnp.where(mask, jnp.exp(lg * dist), 0.0)
        scores = lax.dot_general(qh.astype(BF16), kh.astype(BF16), (((1,), (1,)), ((), ())),
                                 preferred_element_type=F32) * decay
        intra = jnp.dot(scores.astype(BF16), vh, preferred_element_type=F32)
        q_dec = qh * jnp.exp(lg * q_pow)
        k_dec = kh * jnp.exp(lg * k_pow)
        state = state_ref[h]
        cross = jnp.dot(q_dec.astype(BF16), state.astype(BF16), preferred_element_type=F32)
        chunk_kv = lax.dot_general(k_dec.astype(BF16), vh, (((0,), (0,)), ((), ())),
                                   preferred_element_type=F32)
        chunk_decay = jnp.exp(lg * jnp.full((1, RET_DV), float(c), F32))
        state_ref[h] = state * chunk_decay + chunk_kv
        out = intra + cross
        sl = slice(h * RET_DV, (h + 1) * RET_DV)
        if reverse:
            y = fwd_ref[:, sl] + out
            y = y * lax.rsqrt(jnp.mean(y * y, axis=-1, keepdims=True) + EPS)
            g = g_ref[:, sl]
            o_ref[:, sl] = (g / (1.0 + jnp.exp(-g))) * y
        else:
            o_ref[:, sl] = out


def _retention(lg, r_q, r_k, r_v, r_g, cos_t, sin_t, batch, seq):
    t = r_q.shape[0]
    c = RET_CHUNK
    n = seq // c
    scratch = [pltpu.VMEM((RET_HEADS, RET_DK, RET_DV), F32)]
    smem = pl.BlockSpec(memory_space=pltpu.SMEM)

    def specs(chunk_of):
        tok = lambda w: pl.BlockSpec((c, w), lambda b, i: (b * n + chunk_of(i), 0))
        tab = pl.BlockSpec((c, RET_QK), lambda b, i: (chunk_of(i), 0))
        return tok, tab

    tok, tab = specs(lambda i: i)
    fwd = pl.pallas_call(
        functools.partial(_retention_kernel, reverse=False),
        grid=(batch, n),
        in_specs=[smem, tok(RET_QK), tok(RET_QK), tok(RET_V), tab, tab],
        out_specs=tok(RET_V),
        out_shape=jax.ShapeDtypeStruct((t, RET_V), F32),
        scratch_shapes=scratch,
        compiler_params=_params("parallel", "arbitrary"),
        name="retention_fwd",
    )(lg, r_q, r_k, r_v, cos_t, sin_t)
    tok, tab = specs(lambda i: n - 1 - i)
    return pl.pallas_call(
        functools.partial(_retention_kernel, reverse=True),
        grid=(batch, n),
        in_specs=[smem, tok(RET_QK), tok(RET_QK), tok(RET_V), tab, tab, tok(RET_V), tok(RET_V)],
        out_specs=tok(RET_V),
        out_shape=jax.ShapeDtypeStruct((t, RET_V), F32),
        scratch_shapes=scratch,
        compiler_params=_params("parallel", "arbitrary"),
        name="retention_bwd",
    )(lg, r_q, r_k, r_v, cos_t, sin_t, fwd, r_g)


def _na_kernel(q_ref, k_ref, v_ref, bias_ref, o_ref, *, rows, kr):
    step = pl.program_id(1)
    rps = q_ref.shape[0] // GRID_W
    nkeys = kr * GRID_W
    lo = kr // 2
    lane_head = lax.broadcasted_iota(jnp.int32, (GRID_W, NA_WIDTH), 1) // NA_DH
    head_lanes = [lane_head == h for h in range(NA_HEADS)]

    def body(it, carry):
        jobs = []
        for j in range(NA_ROWS_PER_ITER):
            i = it * NA_ROWS_PER_ITER + j
            r = step * rps + i
            r0 = jnp.clip(r - lo, 0, rows - kr)
            ver = jnp.where(r < lo, r, jnp.where(r > rows - kr + lo, r - (rows - kr), lo))
            key_rows = pl.ds(pl.multiple_of(r0 * GRID_W, GRID_W), nkeys)
            qq = q_ref[pl.ds(pl.multiple_of(i * GRID_W, GRID_W), GRID_W), :]
            kk = k_ref[key_rows, :]
            scores = [lax.dot_general(jnp.where(mask, qq, jnp.zeros((), qq.dtype)), kk, (((1,), (1,)), ((), ())),
                                      preferred_element_type=F32) + bias_ref[ver, h]
                      for h, mask in enumerate(head_lanes)]
            jobs.append((i, key_rows, scores))
        probs = []
        for i, key_rows, scores in jobs:
            ps = []
            for s in scores:
                p = jnp.exp(s - jnp.max(s, axis=-1, keepdims=True))
                ps.append((p / jnp.sum(p, axis=-1, keepdims=True)).astype(BF16))
            probs.append((i, key_rows, ps))
        for i, key_rows, ps in probs:
            vv = v_ref[key_rows, :]
            out = jnp.zeros((GRID_W, NA_WIDTH), F32)
            for p, mask in zip(ps, head_lanes):
                out = jnp.where(mask, jnp.dot(p, vv, preferred_element_type=F32), out)
            o_ref[pl.ds(pl.multiple_of(i * GRID_W, GRID_W), GRID_W), :] = out
        return carry

    lax.fori_loop(0, rps // NA_ROWS_PER_ITER, body, 0)


def _na_bias_table(rpb, rows, kr):
    lo = kr // 2
    reps = list(range(lo)) + [lo] + list(range(rows - kr + lo + 1, rows))
    c = np.arange(GRID_W)
    col_start = np.clip(c - NA_COLS // 2, 0, GRID_W - NA_COLS)
    col_in = (c[None, :] >= col_start[:, None]) & (c[None, :] < col_start[:, None] + NA_COLS)
    dj = np.clip(c[None, :] - c[:, None], -(NA_COLS - 1), NA_COLS - 1) + (NA_COLS - 1)
    pick = ((dj[None] == np.arange(2 * NA_COLS - 1)[:, None, None]) & col_in[None]).astype(np.float32)
    di0 = [int(np.clip(r - lo, 0, rows - kr)) - r + (NA_ROWS_MAX - 1) for r in reps]
    rpb_rows = jnp.stack([rpb[:, d0:d0 + kr, :] for d0 in di0]).astype(F32)
    bias = jnp.einsum('vhij,jck->vhcik', rpb_rows, jnp.asarray(pick), precision=lax.Precision.HIGHEST)
    bias = bias + jnp.asarray(np.where(col_in, 0.0, NEG).astype(np.float32))[None, None, :, None, :]
    return bias.reshape(len(reps), NA_HEADS, GRID_W, kr * GRID_W)


def _na(n_q, n_k, n_v, bias, batch, seq):
    t = n_q.shape[0]
    rows = seq // GRID_W
    kr = min(NA_ROWS_MAX, rows)
    rps = min(NA_ROWS_PER_STEP, rows)
    blk = rps * GRID_W
    steps = rows // rps
    return pl.pallas_call(
        functools.partial(_na_kernel, rows=rows, kr=kr),
        grid=(batch, steps),
        in_specs=[pl.BlockSpec((blk, NA_WIDTH), lambda b, i: (b * steps + i, 0)),
                  pl.BlockSpec((seq, NA_WIDTH), lambda b, i: (b, 0)),
                  pl.BlockSpec((seq, NA_WIDTH), lambda b, i: (b, 0)),
                  _const_spec(bias.shape)],
        out_specs=pl.BlockSpec((blk, NA_WIDTH), lambda b, i: (b * steps + i, 0)),
        out_shape=jax.ShapeDtypeStruct((t, NA_WIDTH), F32),
        compiler_params=_params("parallel", "arbitrary"),
        name="natten",
    )(n_q, n_k, n_v, bias)


def _merge_kernel(x_ref, yp_ref, yr_ref, yn_ref, gate_ref, wp_ref, wr_ref, wn_ref, wo_ref,
                  gf_ref, wq_ref, keys_ref, h_ref, hnt_ref, st_ref):
    d = x_ref.shape[-1]
    merged = None
    for b, (y_ref, w_ref) in enumerate(((yp_ref, wp_ref), (yr_ref, wr_ref), (yn_ref, wn_ref))):
        gate = 1.0 / (1.0 + jnp.exp(-gate_ref[:, b * d:(b + 1) * d]))
        term = gate * jnp.dot(y_ref[...].astype(BF16), w_ref[...], preferred_element_type=F32)
        merged = term if merged is None else merged + term
    h = x_ref[...] + jnp.dot(merged.astype(BF16), wo_ref[...], preferred_element_type=F32)
    h_ref[...] = h
    hn = h * lax.rsqrt(jnp.mean(h * h, axis=-1, keepdims=True) + EPS) * gf_ref[...]
    hnb = hn.astype(BF16)
    hnt_ref[...] = hn.T.astype(BF16)
    q = jnp.dot(hnb, wq_ref[...], preferred_element_type=F32).astype(BF16)
    for hh in range(PEER_HEADS):
        for p in range(2):
            col = (hh * 2 + p) * PEER_DK
            st_ref[hh, p] = lax.dot_general(keys_ref[p], q[:, col:col + PEER_DK],
                                            (((1,), (1,)), ((), ())), preferred_element_type=F32)


def _merge(x2, y_pool, y_ret, y_na, gates, wp, wr, wn, wo, g_ffn, wq, keys):
    t, d = x2.shape
    tm = MERGE_TM
    tok = lambda w: pl.BlockSpec((tm, w), lambda i: (i, 0))
    return pl.pallas_call(
        _merge_kernel,
        grid=(t // tm,),
        in_specs=[tok(d), tok(POOL_WIDTH), tok(RET_V), tok(NA_WIDTH), tok(N_BRANCH * d),
                  _const_spec(wp.shape), _const_spec(wr.shape), _const_spec(wn.shape), _const_spec(wo.shape),
                  _const_spec((1, d)), _const_spec(wq.shape), _const_spec(keys.shape)],
        out_specs=[tok(d),
                   pl.BlockSpec((d, tm), lambda i: (0, i)),
                   pl.BlockSpec((PEER_HEADS, 2, PEER_KEYS, tm), lambda i: (0, 0, 0, i))],
        out_shape=[jax.ShapeDtypeStruct((t, d), F32),
                   jax.ShapeDtypeStruct((d, t), BF16),
                   jax.ShapeDtypeStruct((PEER_HEADS, 2, PEER_KEYS, t), F32)],
        compiler_params=_params("parallel"),
        name="merge",
    )(x2, y_pool, y_ret, y_na, gates, wp, wr, wn, wo, g_ffn.reshape(1, d), wq, keys)


def _topk_exact_tile(s_ref, r2_ref, e2_ref, cc_ref, e1_ref, ds):
    nk = s_ref.shape[2]
    k_top = PEER_TOPK
    w = ds.stop - ds.start
    key = lax.broadcasted_iota(jnp.int32, (nk, w), 0)
    arow = lax.broadcasted_iota(jnp.int32, (k_top, w), 0)
    neg_inf = -jnp.inf

    def top_ranks(s):
        def body(a, carry):
            s, rank, vals = carry
            m = jnp.max(s, axis=0, keepdims=True)
            idx = jnp.min(jnp.where(s == m, key, nk), axis=0, keepdims=True)
            hit = key == idx
            return (jnp.where(hit, neg_inf, s), jnp.where(hit, a, rank), jnp.where(arow == a, m, vals))
        init = (s, jnp.full((nk, w), k_top, jnp.int32), jnp.zeros((k_top, w), F32))
        _, rank, vals = lax.fori_loop(0, k_top, body, init)
        return rank, vals

    s1 = s_ref[0, 0, :, ds]
    s2 = s_ref[0, 1, :, ds]
    r1, v1 = top_ranks(s1)
    r2, v2 = top_ranks(s2)
    top = v1[0:1] + v2[0:1]

    def merge(_, carry):
        cnt, front, z = carry
        m = jnp.max(front, axis=0, keepdims=True)
        a_star = jnp.min(jnp.where(front == m, arow, k_top), axis=0, keepdims=True)
        hit = arow == a_star
        cnt = jnp.where(hit, cnt + 1, cnt)
        nxt = jnp.sum(jnp.where(hit, cnt, 0), axis=0, keepdims=True)
        v2_next = jnp.sum(jnp.where(arow == nxt, v2, 0.0), axis=0, keepdims=True)
        cand = jnp.where(nxt < k_top, v1 + v2_next, neg_inf)
        return cnt, jnp.where(hit, cand, front), z + jnp.exp(m - top)
    init = (jnp.zeros((k_top, w), jnp.int32), v1 + v2[0:1], jnp.zeros((1, w), F32))
    cnt, _, z = lax.fori_loop(0, k_top, merge, init)

    cc = jnp.zeros((nk, w), jnp.int32)
    for a in range(k_top):
        cc = jnp.where(r1 == a, cnt[a:a + 1], cc)
    r2_ref[0, :, ds] = r2.astype(F32).astype(BF16)
    cc_ref[0, :, ds] = cc.astype(F32)
    e2_ref[0, :, ds] = jnp.exp(s2 - v2[0:1]).astype(BF16)
    e1_ref[0, :, ds] = jnp.exp(s1 - v1[0:1]) * (GELU_HALF / z)


def _topk_kernel(s_ref, r2_ref, e2_ref, cc_ref, e1_ref, v1_ref, v2_ref):
    nk = s_ref.shape[2]
    tm = s_ref.shape[3]
    k_top = PEER_TOPK
    w = TOPK_SUB
    subs = [slice(j * w, (j + 1) * w) for j in range(tm // w)]
    neg_inf = -jnp.inf
    no_tie_rank_sum = float(sum(range(k_top)) + k_top * (nk - k_top))
    mark_base, mark_step = 2.0 ** 100, 2.0 ** 96
    lowest_score = -(2.0 ** 99)

    redo = jnp.zeros((1, w), F32)
    for j, ds in enumerate(subs):
        for p, (rank_ref, vals_ref) in enumerate(((cc_ref, v1_ref), (r2_ref, v2_ref))):
            s = s_ref[0, p, :, ds]
            floor_ok = jnp.min(s, axis=0, keepdims=True) > lowest_score
            for a in range(k_top):
                m = jnp.max(s, axis=0, keepdims=True)
                s = jnp.where(s == m, -(mark_base + a * mark_step), s)
                vals_ref[a, j:j + 1, :] = m
            rank = jnp.where(s < lowest_score, s * (-1.0 / mark_step) - mark_base / mark_step, float(k_top))
            exact = floor_ok & (jnp.sum(rank, axis=0, keepdims=True) == no_tie_rank_sum)
            redo = redo + jnp.where(exact, 0.0, 1.0)
            rank_ref[0, :, ds] = rank.astype(rank_ref.dtype)

    v1 = v1_ref[...]
    v2 = v2_ref[...]
    arow = lax.broadcasted_iota(jnp.int32, v1.shape, 0).astype(F32)
    top = v1[0:1] + v2[0:1]
    cnt = jnp.zeros(v1.shape, F32)
    front = v1 + v2[0:1]
    z = jnp.zeros(top.shape, F32)
    for _ in range(k_top):
        m = jnp.max(front, axis=0, keepdims=True)
        a_star = jnp.min(jnp.where(front == m, arow, float(k_top)), axis=0, keepdims=True)
        hit = arow == a_star
        cnt = jnp.where(hit, cnt + 1.0, cnt)
        nxt = jnp.sum(jnp.where(hit, cnt, 0.0), axis=0, keepdims=True)
        v2_next = jnp.sum(jnp.where(arow == nxt, v2, 0.0), axis=0, keepdims=True)
        cand = jnp.where(nxt < float(k_top), v1 + v2_next, neg_inf)
        front = jnp.where(hit, cand, front)
        z = z + jnp.exp(m - top)
    inv_z = GELU_HALF / z

    for j, ds in enumerate(subs):
        r1 = cc_ref[0, :, ds]
        cc = jnp.zeros((nk, w), F32)
        for a in range(k_top):
            cc = jnp.where(r1 == float(a), cnt[a, j:j + 1, :], cc)
        cc_ref[0, :, ds] = cc
        e1_ref[0, :, ds] = jnp.exp(s_ref[0, 0, :, ds] - v1[0, j:j + 1, :]) * inv_z[0, j:j + 1, :]
        e2_ref[0, :, ds] = jnp.exp(s_ref[0, 1, :, ds] - v2[0, j:j + 1, :]).astype(BF16)

    @pl.when(jnp.sum(redo) > 0.0)
    def _():
        for ds in subs:
            _topk_exact_tile(s_ref, r2_ref, e2_ref, cc_ref, e1_ref, ds)


def _topk(st):
    heads, _, nk, t = st.shape
    tm = TOPK_TM
    out = pl.BlockSpec((1, nk, tm), lambda i, h: (h, 0, i))
    return pl.pallas_call(
        _topk_kernel,
        grid=(t // tm, heads),
        in_specs=[pl.BlockSpec((1, 2, nk, tm), lambda i, h: (h, 0, 0, i))],
        out_specs=[out] * 4,
        out_shape=[jax.ShapeDtypeStruct((heads, nk, t), dt) for dt in (BF16, BF16, F32, F32)],
        scratch_shapes=[pltpu.VMEM((PEER_TOPK, tm // TOPK_SUB, TOPK_SUB), F32)] * 2,
        compiler_params=_params("parallel", "parallel"),
        name="peer_topk",
    )(st)


def _peer_kernel(hnt_ref, u_ref, v_ref, r2_in_ref, e2_in_ref, cc_ref, e1_ref, h_ref, gn_ref, o_ref,
                 acc_ref, a_ref, ht_ref, r2_ref, e2_ref, *, final_norm):
    e = pl.program_id(1)
    te = u_ref.shape[1]
    tm = hnt_ref.shape[1]
    nk = r2_ref.shape[1]
    rows_per_step = te // nk

    @pl.when(e == 0)
    def _():
        acc_ref[...] = jnp.zeros_like(acc_ref)
        r2_ref[...] = r2_in_ref[...]
        e2_ref[...] = e2_in_ref[...]

    a_ref[...] = lax.dot_general(u_ref[0], hnt_ref[...], (((1,), (0,)), ((), ())),
                                 preferred_element_type=F32)
    for ii in range(rows_per_step):
        rows = slice(ii * nk, (ii + 1) * nk)
        cc_rows = [cc_ref[hh, ii:ii + 1, :] for hh in range(PEER_HEADS)]
        e1_rows = [e1_ref[hh, ii:ii + 1, :] for hh in range(PEER_HEADS)]
        for lt in range(tm // PEER_LANES):
            lanes = slice(lt * PEER_LANES, (lt + 1) * PEER_LANES)
            gate = None
            for hh in range(PEER_HEADS):
                cc = jnp.broadcast_to(cc_rows[hh][:, lanes], (nk, PEER_LANES)).astype(BF16)
                e1 = jnp.broadcast_to(e1_rows[hh][:, lanes], (nk, PEER_LANES)).astype(BF16)
                term = jnp.where(r2_ref[hh, :, lanes] < cc, e2_ref[hh, :, lanes], jnp.zeros((), BF16)) * e1
                gate = term if gate is None else gate + term
            a = a_ref[rows, lanes]
            act = a * (1.0 + lax.erf(a * np.float32(1.0 / np.sqrt(2.0))))
            ht_ref[rows, lanes] = act.astype(BF16) * gate
    acc_ref[...] += lax.dot_general(ht_ref[...].T, v_ref[0], (((1,), (0,)), ((), ())),
                                    preferred_element_type=F32)

    @pl.when(e == pl.num_programs(1) - 1)
    def _():
        y = h_ref[...] + acc_ref[...]
        if final_norm:
            y = y * lax.rsqrt(jnp.mean(y * y, axis=-1, keepdims=True) + EPS) * gn_ref[...]
        o_ref[...] = y


def _peer(hnt, u_all, v_all, layer, r2, e2, cc, e1, h2, g_norm, final_norm):
    d, t = hnt.shape
    n_exp = u_all.shape[1]
    heads, nk, _ = r2.shape
    tm, te = PEER_TM, PEER_TE
    tab = pl.BlockSpec((heads, nk, tm), lambda i, e: (0, 0, i))
    row_tab = pl.BlockSpec((heads, te // nk, tm), lambda i, e: (0, e, i))
    return pl.pallas_call(
        functools.partial(_peer_kernel, final_norm=final_norm),
        grid=(t // tm, n_exp // te),
        in_specs=[pl.BlockSpec((d, tm), lambda i, e: (0, i)),
                  pl.BlockSpec((1, te, d), lambda i, e: (layer, e, 0)),
                  pl.BlockSpec((1, te, d), lambda i, e: (layer, e, 0)),
                  tab, tab, row_tab, row_tab,
                  pl.BlockSpec((tm, d), lambda i, e: (i, 0)),
                  pl.BlockSpec((1, d), lambda i, e: (0, 0))],
        out_specs=pl.BlockSpec((tm, d), lambda i, e: (i, 0)),
        out_shape=jax.ShapeDtypeStruct((t, d), F32),
        scratch_shapes=[pltpu.VMEM((tm, d), F32), pltpu.VMEM((te, tm), F32), pltpu.VMEM((te, tm), BF16),
                        pltpu.VMEM((heads, nk, tm), BF16), pltpu.VMEM((heads, nk, tm), BF16)],
        compiler_params=_params("parallel", "arbitrary"),
        name="peer_experts",
    )(hnt, u_all, v_all, r2, e2, cc, e1, h2, g_norm.reshape(1, d))


def _rotary_tables(seq):
    half = RET_DK // 2
    inv = 1.0 / (ROPE_BASE ** jnp.linspace(0.0, 1.0, half, dtype=F32))
    ang = jnp.arange(seq, dtype=F32)[:, None] * inv[None, :]
    cos, sin = jnp.cos(ang), jnp.sin(ang)
    cos_t = jnp.tile(jnp.concatenate([cos, cos], axis=-1), (1, RET_HEADS))
    sin_t = jnp.tile(jnp.concatenate([-sin, sin], axis=-1), (1, RET_HEADS))
    return cos_t, sin_t


def _pool_weights(pool_w, pool_scale):
    halves = POOL_WIDTH // 128
    per = 128 // POOL_GROUP
    blocks = []
    for hf in range(halves):
        blk = jnp.zeros((128, 128), F32)
        for g in range(per):
            sl = slice(g * POOL_GROUP, (g + 1) * POOL_GROUP)
            blk = blk.at[sl, sl].set(pool_w[hf * per + g])
        blocks.append(blk)
    return jnp.stack(blocks).astype(BF16), pool_scale.reshape(halves, 1, 128)


def kernel(x, norm_mix, w_in, pool_w, pool_scale, ret_decay, na_rpb, w_br_pool, w_br_ret, w_br_na,
           w_out, norm_ffn, peer_w_query, peer_sub_keys, peer_u, peer_v, norm_final):
    batch, seq, d = x.shape
    depth = w_in.shape[0]
    rows = seq // GRID_W
    kr = min(NA_ROWS_MAX, rows)
    cos_t, sin_t = _rotary_tables(seq)
    x2 = x.reshape(batch * seq, d)
    for l in range(depth):
        pool, r_q, r_k, r_v, r_g, n_q, n_k, n_v, gates = _inproj(x2, norm_mix[l], w_in[l].astype(BF16))
        w_halves, scale_halves = _pool_weights(pool_w[l], pool_scale[l])
        y_pool = _pool(pool, w_halves, scale_halves, batch, seq)
        lg = jax.nn.log_sigmoid(ret_decay[l].astype(F32))
        y_ret = _retention(lg, r_q, r_k, r_v, r_g, cos_t, sin_t, batch, seq)
        y_na = _na(n_q, n_k, n_v, _na_bias_table(na_rpb[l], rows, kr), batch, seq)
        h2, hnt, st = _merge(x2, y_pool, y_ret, y_na, gates,
                             w_br_pool[l].astype(BF16), w_br_ret[l].astype(BF16), w_br_na[l].astype(BF16),
                             w_out[l].astype(BF16), norm_ffn[l], peer_w_query[l].astype(BF16),
                             peer_sub_keys[l].astype(BF16))
        r2, e2, cc, e1 = _topk(st)
        x2 = _peer(hnt, peer_u, peer_v, l, r2, e2, cc, e1, h2, norm_final, final_norm=l == depth - 1)
    return x2.reshape(batch, seq, d)
```

```python
import functools

import numpy as np
import jax
import jax.numpy as jnp
from jax import lax
from jax.experimental import pallas as pl
from jax.experimental.pallas import tpu as pltpu

F32 = jnp.float32
BF16 = jnp.bfloat16

EPS = 1e-6
GRID_W = 64
NEG = -1e30

POOL_WINDOWS = (2, 4, 8, 16)
POOL_GROUP = 64
POOL_WIDTH = POOL_GROUP * len(POOL_WINDOWS)

RET_HEADS = 4
RET_DK = 64
RET_DV = 128
RET_QK = RET_HEADS * RET_DK
RET_V = RET_HEADS * RET_DV
ROPE_BASE = 10000.0

NA_HEADS = 4
NA_DH = 64
NA_WIDTH = NA_HEADS * NA_DH
NA_ROWS_MAX = 8
NA_COLS = 16

N_BRANCH = 3

PEER_KEYS = 128
PEER_HEADS = 8
PEER_DK = 128
PEER_TOPK = 16
GELU_HALF = 0.5

V7X_VMEM_LIMIT_BYTES = 56 * 1024 * 1024

INPROJ_TM = 256
RET_CHUNK = 256
NA_ROWS_PER_STEP = 8
NA_ROWS_PER_ITER = 8
MERGE_TM = 256
TOPK_TM = 1024
TOPK_SUB = 128
PEER_TM = 512
PEER_TE = 1024
PEER_LANES = 128


def _params(*sem):
    return pltpu.CompilerParams(dimension_semantics=sem, vmem_limit_bytes=V7X_VMEM_LIMIT_BYTES)


def _const_spec(shape):
    nd = len(shape)
    return pl.BlockSpec(shape, lambda *_: (0,) * nd)


def _inproj_kernel(x_ref, g_ref, w_ref, pool_ref, rq_ref, rk_ref, rv_ref, rg_ref,
                   nq_ref, nk_ref, nv_ref, gate_ref):
    x = x_ref[...]
    xn = x * lax.rsqrt(jnp.mean(x * x, axis=-1, keepdims=True) + EPS) * g_ref[...]
    xb = xn.astype(BF16)
    off = 0
    for ref, scale in ((pool_ref, None), (rq_ref, None), (rk_ref, None), (rv_ref, None), (rg_ref, None),
                       (nq_ref, NA_DH ** -0.5), (nk_ref, None), (nv_ref, None)):
        width = ref.shape[-1]
        y = jnp.dot(xb, w_ref[:, off:off + width], preferred_element_type=F32)
        if scale is not None:
            y = y * scale
        ref[...] = y.astype(ref.dtype)
        off += width
    d = x.shape[-1]
    for b in range(N_BRANCH):
        gate_ref[:, b * d:(b + 1) * d] = jnp.dot(
            xb, w_ref[:, off + b * d:off + (b + 1) * d], preferred_element_type=F32)


def _inproj(x2, g, w_in_bf):
    t, d = x2.shape
    tm = INPROJ_TM
    widths = (POOL_WIDTH, RET_QK, RET_QK, RET_V, RET_V, NA_WIDTH, NA_WIDTH, NA_WIDTH, N_BRANCH * d)
    dtypes = (F32, F32, F32, BF16, F32, BF16, BF16, BF16, F32)
    tok = lambda w: pl.BlockSpec((tm, w), lambda i: (i, 0))
    return pl.pallas_call(
        _inproj_kernel,
        grid=(t // tm,),
        in_specs=[tok(d), _const_spec((1, d)), _const_spec(w_in_bf.shape)],
        out_specs=[tok(w) for w in widths],
        out_shape=[jax.ShapeDtypeStruct((t, w), dt) for w, dt in zip(widths, dtypes)],
        compiler_params=_params("parallel"),
        name="inproj",
    )(x2, g.reshape(1, d), w_in_bf)


def _pool_kernel(p_ref, w_ref, scale_ref, o_ref, *, halves):
    x = p_ref[...]
    s = x.shape[0]
    t = lax.broadcasted_iota(jnp.int32, x.shape, 0)
    lane = lax.broadcasted_iota(jnp.int32, x.shape, 1)
    half = pl.program_id(1)

    def down(v, k):
        return jnp.where(t >= k, pltpu.roll(v, k, 0), 0.0)

    def up(v, k):
        return jnp.where(t < s - k, pltpu.roll(v, s - k, 0), 0.0)

    trail = {1: x}
    lead = {1: x}
    for m in (2, 4, 8):
        trail[m] = trail[m // 2] + down(trail[m // 2], m // 2)
        lead[m] = lead[m // 2] + up(lead[m // 2], m // 2)

    def window(hw):
        return down(trail[hw], 1) + lead[hw]

    def pick(lo_first, hi_first, lo_second, hi_second):
        first = jnp.where(lane < POOL_GROUP, lo_first, hi_first)
        second = jnp.where(lane < POOL_GROUP, lo_second, hi_second)
        return jnp.where(half == 0, first, second)

    hw = [w // 2 for w in POOL_WINDOWS]
    wsum = pick(window(hw[0]), window(hw[1]), window(hw[2]), window(hw[3]))
    hwv = pick(hw[0], hw[1], hw[2], hw[3])
    cnt = (jnp.minimum(t + hwv, s) - jnp.maximum(t - hwv, 0)).astype(F32)
    dlt = wsum / cnt - x
    y = jnp.dot(dlt.astype(BF16), w_ref[0], preferred_element_type=F32)
    o_ref[...] = y * scale_ref[0]


def _pool(p_pool, w_halves, scale_halves, batch, seq):
    t = p_pool.shape[0]
    halves = POOL_WIDTH // 128
    return pl.pallas_call(
        functools.partial(_pool_kernel, halves=halves),
        grid=(batch, halves),
        in_specs=[pl.BlockSpec((seq, 128), lambda b, h: (b, h)),
                  pl.BlockSpec((1, 128, 128), lambda b, h: (h, 0, 0)),
                  pl.BlockSpec((1, 1, 128), lambda b, h: (h, 0, 0))],
        out_specs=pl.BlockSpec((seq, 128), lambda b, h: (b, h)),
        out_shape=jax.ShapeDtypeStruct((t, POOL_WIDTH), F32),
        compiler_params=_params("parallel", "parallel"),
        name="pool",
    )(p_pool, w_halves, scale_halves)


def _rotary(x, cos, sin_signed, first_half):
    width = x.shape[-1]
    half = RET_DK // 2
    swapped = jnp.where(first_half, pltpu.roll(x, width - half, 1), pltpu.roll(x, half, 1))
    return x * cos + swapped * sin_signed


def _retention_kernel(lg_ref, q_ref, k_ref, v_ref, cos_ref, sin_ref, *rest, reverse):
    if reverse:
        fwd_ref, g_ref, o_ref, state_ref = rest
    else:
        o_ref, state_ref = rest
    c = q_ref.shape[0]

    @pl.when(pl.program_id(1) == 0)
    def _():
        state_ref[...] = jnp.zeros_like(state_ref)

    lane = lax.broadcasted_iota(jnp.int32, (c, RET_QK), 1)
    first_half = (lane % RET_DK) < (RET_DK // 2)
    cos = cos_ref[...]
    sin = sin_ref[...]
    q = _rotary(q_ref[...], cos, sin, first_half)
    k = _rotary(k_ref[...], cos, sin, first_half) * (RET_DK ** -0.5)
    v = v_ref[...]

    row = lax.broadcasted_iota(jnp.int32, (c, c), 0)
    col = lax.broadcasted_iota(jnp.int32, (c, c), 1)
    pos = lax.broadcasted_iota(jnp.int32, (c, 1), 0).astype(F32)
    if reverse:
        diff = col - row
        mask = diff > 0
        q_pow = c - pos
        k_pow = pos
    else:
        diff = row - col
        mask = diff >= 0
        q_pow = pos + 1.0
        k_pow = (c - 1) - pos
    dist = jnp.where(mask, diff, 0).astype(F32)
    direction = 1 if reverse else 0

    for h in range(RET_HEADS):
        lg = lg_ref[direction, h]
        qh = q[:, h * RET_DK:(h + 1) * RET_DK]
        kh = k[:, h * RET_DK:(h + 1) * RET_DK]
        vh = v[:, h * RET_DV:(h + 1) * RET_DV]
        decay = jnp.where(mask, jnp.exp(lg * dist), 0.0)
        scores = lax.dot_general(qh.astype(BF16), kh.astype(BF16), (((1,), (1,)), ((), ())),
                                 preferred_element_type=F32) * decay
        intra = jnp.dot(scores.astype(BF16), vh, preferred_element_type=F32)
        q_dec = qh * jnp.exp(lg * q_pow)
        k_dec = kh * jnp.exp(lg * k_pow)
        state = state_ref[h]
        cross = jnp.dot(q_dec.astype(BF16), state.astype(BF16), preferred_element_type=F32)
        chunk_kv = lax.dot_general(k_dec.astype(BF16), vh, (((0,), (0,)), ((), ())),
                                   preferred_element_type=F32)
        chunk_decay = jnp.exp(lg * jnp.full((1, RET_DV), float(c), F32))
        state_ref[h] = state * chunk_decay + chunk_kv
        out = intra + cross
        sl = slice(h * RET_DV, (h + 1) * RET_DV)
        if reverse:
            y = fwd_ref[:, sl] + out
            y = y * lax.rsqrt(jnp.mean(y * y, axis=-1, keepdims=True) + EPS)
            g = g_ref[:, sl]
            o_ref[:, sl] = (g / (1.0 + jnp.exp(-g))) * y
        else:
            o_ref[:, sl] = out


def _retention(lg, r_q, r_k, r_v, r_g, cos_t, sin_t, batch, seq):
    t = r_q.shape[0]
    c = RET_CHUNK
    n = seq // c
    scratch = [pltpu.VMEM((RET_HEADS, RET_DK, RET_DV), F32)]
    smem = pl.BlockSpec(memory_space=pltpu.SMEM)

    def specs(chunk_of):
        tok = lambda w: pl.BlockSpec((c, w), lambda b, i: (b * n + chunk_of(i), 0))
        tab = pl.BlockSpec((c, RET_QK), lambda b, i: (chunk_of(i), 0))
        return tok, tab

    tok, tab = specs(lambda i: i)
    fwd = pl.pallas_call(
        functools.partial(_retention_kernel, reverse=False),
        grid=(batch, n),
        in_specs=[smem, tok(RET_QK), tok(RET_QK), tok(RET_V), tab, tab],
        out_specs=tok(RET_V),
        out_shape=jax.ShapeDtypeStruct((t, RET_V), F32),
        scratch_shapes=scratch,
        compiler_params=_params("parallel", "arbitrary"),
        name="retention_fwd",
    )(lg, r_q, r_k, r_v, cos_t, sin_t)
    tok, tab = specs(lambda i: n - 1 - i)
    return pl.pallas_call(
        functools.partial(_retention_kernel, reverse=True),
        grid=(batch, n),
        in_specs=[smem, tok(RET_QK), tok(RET_QK), tok(RET_V), tab, tab, tok(RET_V), tok(RET_V)],
        out_specs=tok(RET_V),
        out_shape=jax.ShapeDtypeStruct((t, RET_V), F32),
        scratch_shapes=scratch,
        compiler_params=_params("parallel", "arbitrary"),
        name="retention_bwd",
    )(lg, r_q, r_k, r_v, cos_t, sin_t, fwd, r_g)


def _na_kernel(q_ref, k_ref, v_ref, bias_ref, o_ref, *, rows, kr):
    step = pl.program_id(1)
    rps = q_ref.shape[0] // GRID_W
    nkeys = kr * GRID_W
    lo = kr // 2
    lane_head = lax.broadcasted_iota(jnp.int32, (GRID_W, NA_WIDTH), 1) // NA_DH
    head_lanes = [lane_head == h for h in range(NA_HEADS)]

    def body(it, carry):
        jobs = []
        for j in range(NA_ROWS_PER_ITER):
            i = it * NA_ROWS_PER_ITER + j
            r = step * rps + i
            r0 = jnp.clip(r - lo, 0, rows - kr)
            ver = jnp.where(r < lo, r, jnp.where(r > rows - kr + lo, r - (rows - kr), lo))
            key_rows = pl.ds(pl.multiple_of(r0 * GRID_W, GRID_W), nkeys)
            qq = q_ref[pl.ds(pl.multiple_of(i * GRID_W, GRID_W), GRID_W), :]
            kk = k_ref[key_rows, :]
            scores = [lax.dot_general(jnp.where(mask, qq, jnp.zeros((), qq.dtype)), kk, (((1,), (1,)), ((), ())),
                                      preferred_element_type=F32) + bias_ref[ver, h]
                      for h, mask in enumerate(head_lanes)]
            jobs.append((i, key_rows, scores))
        probs = []
        for i, key_rows, scores in jobs:
            ps = []
            for s in scores:
                p = jnp.exp(s - jnp.max(s, axis=-1, keepdims=True))
                ps.append((p / jnp.sum(p, axis=-1, keepdims=True)).astype(BF16))
            probs.append((i, key_rows, ps))
        for i, key_rows, ps in probs:
            vv = v_ref[key_rows, :]
            out = jnp.zeros((GRID_W, NA_WIDTH), F32)
            for p, mask in zip(ps, head_lanes):
                out = jnp.where(mask, jnp.dot(p, vv, preferred_element_type=F32), out)
            o_ref[pl.ds(pl.multiple_of(i * GRID_W, GRID_W), GRID_W), :] = out
        return carry

    lax.fori_loop(0, rps // NA_ROWS_PER_ITER, body, 0)


def _na_bias_table(rpb, rows, kr):
    lo = kr // 2
    reps = list(range(lo)) + [lo] + list(range(rows - kr + lo + 1, rows))
    c = np.arange(GRID_W)
    col_start = np.clip(c - NA_COLS // 2, 0, GRID_W - NA_COLS)
    col_in = (c[None, :] >= col_start[:, None]) & (c[None, :] < col_start[:, None] + NA_COLS)
    dj = np.clip(c[None, :] - c[:, None], -(NA_COLS - 1), NA_COLS - 1) + (NA_COLS - 1)
    di0 = [int(np.clip(r - lo, 0, rows - kr)) - r + (NA_ROWS_MAX - 1) for r in reps]
    rpb_rows = jnp.stack([rpb[:, d0:d0 + kr, :] for d0 in di0]).astype(F32)
    bias = jnp.full((len(reps), NA_HEADS, GRID_W, kr, GRID_W), NEG, F32)
    for j in range(2 * NA_COLS - 1):
        here = jnp.asarray((dj == j) & col_in)[None, None, :, None, :]
        bias = jnp.where(here, rpb_rows[:, :, None, :, j, None], bias)
    return bias.reshape(len(reps), NA_HEADS, GRID_W, kr * GRID_W)


def _na(n_q, n_k, n_v, bias, batch, seq):
    t = n_q.shape[0]
    rows = seq // GRID_W
    kr = min(NA_ROWS_MAX, rows)
    rps = min(NA_ROWS_PER_STEP, rows)
    blk = rps * GRID_W
    steps = rows // rps
    return pl.pallas_call(
        functools.partial(_na_kernel, rows=rows, kr=kr),
        grid=(batch, steps),
        in_specs=[pl.BlockSpec((blk, NA_WIDTH), lambda b, i: (b * steps + i, 0)),
                  pl.BlockSpec((seq, NA_WIDTH), lambda b, i: (b, 0)),
                  pl.BlockSpec((seq, NA_WIDTH), lambda b, i: (b, 0)),
                  _const_spec(bias.shape)],
        out_specs=pl.BlockSpec((blk, NA_WIDTH), lambda b, i: (b * steps + i, 0)),
        out_shape=jax.ShapeDtypeStruct((t, NA_WIDTH), F32),
        compiler_params=_params("parallel", "arbitrary"),
        name="natten",
    )(n_q, n_k, n_v, bias)


def _merge_kernel(x_ref, yp_ref, yr_ref, yn_ref, gate_ref, wp_ref, wr_ref, wn_ref, wo_ref,
                  gf_ref, wq_ref, keys_ref, h_ref, hnt_ref, st_ref):
    d = x_ref.shape[-1]
    merged = None
    for b, (y_ref, w_ref) in enumerate(((yp_ref, wp_ref), (yr_ref, wr_ref), (yn_ref, wn_ref))):
        gate = 1.0 / (1.0 + jnp.exp(-gate_ref[:, b * d:(b + 1) * d]))
        term = gate * jnp.dot(y_ref[...].astype(BF16), w_ref[...], preferred_element_type=F32)
        merged = term if merged is None else merged + term
    h = x_ref[...] + jnp.dot(merged.astype(BF16), wo_ref[...], preferred_element_type=F32)
    h_ref[...] = h
    hn = h * lax.rsqrt(jnp.mean(h * h, axis=-1, keepdims=True) + EPS) * gf_ref[...]
    hnb = hn.astype(BF16)
    hnt_ref[...] = hn.T.astype(BF16)
    q = jnp.dot(hnb, wq_ref[...], preferred_element_type=F32).astype(BF16)
    for hh in range(PEER_HEADS):
        for p in range(2):
            col = (hh * 2 + p) * PEER_DK
            st_ref[hh, p] = lax.dot_general(keys_ref[p], q[:, col:col + PEER_DK],
                                            (((1,), (1,)), ((), ())), preferred_element_type=F32)


def _merge(x2, y_pool, y_ret, y_na, gates, wp, wr, wn, wo, g_ffn, wq, keys):
    t, d = x2.shape
    tm = MERGE_TM
    tok = lambda w: pl.BlockSpec((tm, w), lambda i: (i, 0))
    return pl.pallas_call(
        _merge_kernel,
        grid=(t // tm,),
        in_specs=[tok(d), tok(POOL_WIDTH), tok(RET_V), tok(NA_WIDTH), tok(N_BRANCH * d),
                  _const_spec(wp.shape), _const_spec(wr.shape), _const_spec(wn.shape), _const_spec(wo.shape),
                  _const_spec((1, d)), _const_spec(wq.shape), _const_spec(keys.shape)],
        out_specs=[tok(d),
                   pl.BlockSpec((d, tm), lambda i: (0, i)),
                   pl.BlockSpec((PEER_HEADS, 2, PEER_KEYS, tm), lambda i: (0, 0, 0, i))],
        out_shape=[jax.ShapeDtypeStruct((t, d), F32),
                   jax.ShapeDtypeStruct((d, t), BF16),
                   jax.ShapeDtypeStruct((PEER_HEADS, 2, PEER_KEYS, t), F32)],
        compiler_params=_params("parallel"),
        name="merge",
    )(x2, y_pool, y_ret, y_na, gates, wp, wr, wn, wo, g_ffn.reshape(1, d), wq, keys)


def _topk_exact_tile(s_ref, r2_ref, e2_ref, cc_ref, e1_ref, ds):
    nk = s_ref.shape[2]
    k_top = PEER_TOPK
    w = ds.stop - ds.start
    key = lax.broadcasted_iota(jnp.int32, (nk, w), 0)
    arow = lax.broadcasted_iota(jnp.int32, (k_top, w), 0)
    neg_inf = -jnp.inf

    def top_ranks(s):
        def body(a, carry):
            s, rank, vals = carry
            m = jnp.max(s, axis=0, keepdims=True)
            idx = jnp.min(jnp.where(s == m, key, nk), axis=0, keepdims=True)
            hit = key == idx
            return (jnp.where(hit, neg_inf, s), jnp.where(hit, a, rank), jnp.where(arow == a, m, vals))
        init = (s, jnp.full((nk, w), k_top, jnp.int32), jnp.zeros((k_top, w), F32))
        _, rank, vals = lax.fori_loop(0, k_top, body, init)
        return rank, vals

    s1 = s_ref[0, 0, :, ds]
    s2 = s_ref[0, 1, :, ds]
    r1, v1 = top_ranks(s1)
    r2, v2 = top_ranks(s2)
    top = v1[0:1] + v2[0:1]

    def merge(_, carry):
        cnt, front, z = carry
        m = jnp.max(front, axis=0, keepdims=True)
        a_star = jnp.min(jnp.where(front == m, arow, k_top), axis=0, keepdims=True)
        hit = arow == a_star
        cnt = jnp.where(hit, cnt + 1, cnt)
        nxt = jnp.sum(jnp.where(hit, cnt, 0), axis=0, keepdims=True)
        v2_next = jnp.sum(jnp.where(arow == nxt, v2, 0.0), axis=0, keepdims=True)
        cand = jnp.where(nxt < k_top, v1 + v2_next, neg_inf)
        return cnt, jnp.where(hit, cand, front), z + jnp.exp(m - top)
    init = (jnp.zeros((k_top, w), jnp.int32), v1 + v2[0:1], jnp.zeros((1, w), F32))
    cnt, _, z = lax.fori_loop(0, k_top, merge, init)

    cc = jnp.zeros((nk, w), jnp.int32)
    for a in range(k_top):
        cc = jnp.where(r1 == a, cnt[a:a + 1], cc)
    r2_ref[0, :, ds] = r2.astype(F32).astype(BF16)
    cc_ref[0, :, ds] = cc.astype(F32)
    e2_ref[0, :, ds] = jnp.exp(s2 - v2[0:1]).astype(BF16)
    e1_ref[0, :, ds] = jnp.exp(s1 - v1[0:1]) * (GELU_HALF / z)


def _topk_kernel(s_ref, r2_ref, e2_ref, cc_ref, e1_ref, v1_ref, v2_ref):
    nk = s_ref.shape[2]
    tm = s_ref.shape[3]
    k_top = PEER_TOPK
    w = TOPK_SUB
    subs = [slice(j * w, (j + 1) * w) for j in range(tm // w)]
    neg_inf = -jnp.inf
    no_tie_rank_sum = float(sum(range(k_top)) + k_top * (nk - k_top))
    mark_base, mark_step = 2.0 ** 100, 2.0 ** 96
    lowest_score = -(2.0 ** 99)

    redo = jnp.zeros((1, w), F32)
    for j, ds in enumerate(subs):
        for p, (rank_ref, vals_ref) in enumerate(((cc_ref, v1_ref), (r2_ref, v2_ref))):
            s = s_ref[0, p, :, ds]
            floor_ok = jnp.min(s, axis=0, keepdims=True) > lowest_score
            for a in range(k_top):
                m = jnp.max(s, axis=0, keepdims=True)
                s = jnp.where(s == m, -(mark_base + a * mark_step), s)
                vals_ref[a, j:j + 1, :] = m
            rank = jnp.where(s < lowest_score, s * (-1.0 / mark_step) - mark_base / mark_step, float(k_top))
            exact = floor_ok & (jnp.sum(rank, axis=0, keepdims=True) == no_tie_rank_sum)
            redo = redo + jnp.where(exact, 0.0, 1.0)
            rank_ref[0, :, ds] = rank.astype(rank_ref.dtype)

    v1 = v1_ref[...]
    v2 = v2_ref[...]
    arow = lax.broadcasted_iota(jnp.int32, v1.shape, 0).astype(F32)
    top = v1[0:1] + v2[0:1]
    cnt = jnp.zeros(v1.shape, F32)
    front = v1 + v2[0:1]
    z = jnp.zeros(top.shape, F32)
    for _ in range(k_top):
        m = jnp.max(front, axis=0, keepdims=True)
        a_star = jnp.min(jnp.where(front == m, arow, float(k_top)), axis=0, keepdims=True)
        hit = arow == a_star
        cnt = jnp.where(hit, cnt + 1.0, cnt)
        nxt = jnp.sum(jnp.where(hit, cnt, 0.0), axis=0, keepdims=True)
        v2_next = jnp.sum(jnp.where(arow == nxt, v2, 0.0), axis=0, keepdims=True)
        cand = jnp.where(nxt < float(k_top), v1 + v2_next, neg_inf)
        front = jnp.where(hit, cand, front)
        z = z + jnp.exp(m - top)
    inv_z = GELU_HALF / z

    for j, ds in enumerate(subs):
        r1 = cc_ref[0, :, ds]
        cc = jnp.zeros((nk, w), F32)
        for a in range(k_top):
            cc = jnp.where(r1 == float(a), cnt[a, j:j + 1, :], cc)
        cc_ref[0, :, ds] = cc
        e1_ref[0, :, ds] = jnp.exp(s_ref[0, 0, :, ds] - v1[0, j:j + 1, :]) * inv_z[0, j:j + 1, :]
        e2_ref[0, :, ds] = jnp.exp(s_ref[0, 1, :, ds] - v2[0, j:j + 1, :]).astype(BF16)

    @pl.when(jnp.sum(redo) > 0.0)
    def _():
        for ds in subs:
            _topk_exact_tile(s_ref, r2_ref, e2_ref, cc_ref, e1_ref, ds)


def _topk(st):
    heads, _, nk, t = st.shape
    tm = TOPK_TM
    out = pl.BlockSpec((1, nk, tm), lambda i, h: (h, 0, i))
    return pl.pallas_call(
        _topk_kernel,
        grid=(t // tm, heads),
        in_specs=[pl.BlockSpec((1, 2, nk, tm), lambda i, h: (h, 0, 0, i))],
        out_specs=[out] * 4,
        out_shape=[jax.ShapeDtypeStruct((heads, nk, t), dt) for dt in (BF16, BF16, F32, F32)],
        scratch_shapes=[pltpu.VMEM((PEER_TOPK, tm // TOPK_SUB, TOPK_SUB), F32)] * 2,
        compiler_params=_params("parallel", "parallel"),
        name="peer_topk",
    )(st)


def _peer_kernel(hnt_ref, u_ref, v_ref, r2_in_ref, e2_in_ref, cc_ref, e1_ref, h_ref, gn_ref, o_ref,
                 acc_ref, a_ref, ht_ref, r2_ref, e2_ref, *, final_norm):
    e = pl.program_id(1)
    te = u_ref.shape[1]
    tm = hnt_ref.shape[1]
    nk = r2_ref.shape[1]
    rows_per_step = te // nk

    @pl.when(e == 0)
    def _():
        acc_ref[...] = jnp.zeros_like(acc_ref)
        r2_ref[...] = r2_in_ref[...]
        e2_ref[...] = e2_in_ref[...]

    a_ref[...] = lax.dot_general(u_ref[0], hnt_ref[...], (((1,), (0,)), ((), ())),
                                 preferred_element_type=F32)
    for ii in range(rows_per_step):
        rows = slice(ii * nk, (ii + 1) * nk)
        cc_rows = [cc_ref[hh, ii:ii + 1, :] for hh in range(PEER_HEADS)]
        e1_rows = [e1_ref[hh, ii:ii + 1, :] for hh in range(PEER_HEADS)]
        for lt in range(tm // PEER_LANES):
            lanes = slice(lt * PEER_LANES, (lt + 1) * PEER_LANES)
            gate = None
            for hh in range(PEER_HEADS):
                cc = jnp.broadcast_to(cc_rows[hh][:, lanes], (nk, PEER_LANES)).astype(BF16)
                e1 = jnp.broadcast_to(e1_rows[hh][:, lanes], (nk, PEER_LANES)).astype(BF16)
                term = jnp.where(r2_ref[hh, :, lanes] < cc, e2_ref[hh, :, lanes], jnp.zeros((), BF16)) * e1
                gate = term if gate is None else gate + term
            a = a_ref[rows, lanes]
            act = a * (1.0 + lax.erf(a * np.float32(1.0 / np.sqrt(2.0))))
            ht_ref[rows, lanes] = act.astype(BF16) * gate
    acc_ref[...] += lax.dot_general(ht_ref[...].T, v_ref[0], (((1,), (0,)), ((), ())),
                                    preferred_element_type=F32)

    @pl.when(e == pl.num_programs(1) - 1)
    def _():
        y = h_ref[...] + acc_ref[...]
        if final_norm:
            y = y * lax.rsqrt(jnp.mean(y * y, axis=-1, keepdims=True) + EPS) * gn_ref[...]
        o_ref[...] = y


def _peer(hnt, u_all, v_all, layer, r2, e2, cc, e1, h2, g_norm, final_norm):
    d, t = hnt.shape
    n_exp = u_all.shape[1]
    heads, nk, _ = r2.shape
    tm, te = PEER_TM, PEER_TE
    tab = pl.BlockSpec((heads, nk, tm), lambda i, e: (0, 0, i))
    row_tab = pl.BlockSpec((heads, te // nk, tm), lambda i, e: (0, e, i))
    return pl.pallas_call(
        functools.partial(_peer_kernel, final_norm=final_norm),
        grid=(t // tm, n_exp // te),
        in_specs=[pl.BlockSpec((d, tm), lambda i, e: (0, i)),
                  pl.BlockSpec((1, te, d), lambda i, e: (layer, e, 0)),
                  pl.BlockSpec((1, te, d), lambda i, e: (layer, e, 0)),
                  tab, tab, row_tab, row_tab,
                  pl.BlockSpec((tm, d), lambda i, e: (i, 0)),
                  pl.BlockSpec((1, d), lambda i, e: (0, 0))],
        out_specs=pl.BlockSpec((tm, d), lambda i, e: (i, 0)),
        out_shape=jax.ShapeDtypeStruct((t, d), F32),
        scratch_shapes=[pltpu.VMEM((tm, d), F32), pltpu.VMEM((te, tm), F32), pltpu.VMEM((te, tm), BF16),
                        pltpu.VMEM((heads, nk, tm), BF16), pltpu.VMEM((heads, nk, tm), BF16)],
        compiler_params=_params("parallel", "arbitrary"),
        name="peer_experts",
    )(hnt, u_all, v_all, r2, e2, cc, e1, h2, g_norm.reshape(1, d))


def _rotary_tables(seq):
    half = RET_DK // 2
    inv = 1.0 / (ROPE_BASE ** jnp.linspace(0.0, 1.0, half, dtype=F32))
    ang = jnp.arange(seq, dtype=F32)[:, None] * inv[None, :]
    cos, sin = jnp.cos(ang), jnp.sin(ang)
    cos_t = jnp.tile(jnp.concatenate([cos, cos], axis=-1), (1, RET_HEADS))
    sin_t = jnp.tile(jnp.concatenate([-sin, sin], axis=-1), (1, RET_HEADS))
    return cos_t, sin_t


def _pool_weights(pool_w, pool_scale):
    halves = POOL_WIDTH // 128
    per = 128 // POOL_GROUP
    blocks = []
    for hf in range(halves):
        blk = jnp.zeros((128, 128), F32)
        for g in range(per):
            sl = slice(g * POOL_GROUP, (g + 1) * POOL_GROUP)
            blk = blk.at[sl, sl].set(pool_w[hf * per + g])
        blocks.append(blk)
    return jnp.stack(blocks).astype(BF16), pool_scale.reshape(halves, 1, 128)


def kernel(x, norm_mix, w_in, pool_w, pool_scale, ret_decay, na_rpb, w_br_pool, w_br_ret, w_br_na,
           w_out, norm_ffn, peer_w_query, peer_sub_keys, peer_u, peer_v, norm_final):
    batch, seq, d = x.shape
    depth = w_in.shape[0]
    rows = seq // GRID_W
    kr = min(NA_ROWS_MAX, rows)
    cos_t, sin_t = _rotary_tables(seq)
    x2 = x.reshape(batch * seq, d)
    for l in range(depth):
        pool, r_q, r_k, r_v, r_g, n_q, n_k, n_v, gates = _inproj(x2, norm_mix[l], w_in[l].astype(BF16))
        w_halves, scale_halves = _pool_weights(pool_w[l], pool_scale[l])
        y_pool = _pool(pool, w_halves, scale_halves, batch, seq)
        lg = jax.nn.log_sigmoid(ret_decay[l].astype(F32))
        y_ret = _retention(lg, r_q, r_k, r_v, r_g, cos_t, sin_t, batch, seq)
        y_na = _na(n_q, n_k, n_v, _na_bias_table(na_rpb[l], rows, kr), batch, seq)
        h2, hnt, st = _merge(x2, y_pool, y_ret, y_na, gates,
                             w_br_pool[l].astype(BF16), w_br_ret[l].astype(BF16), w_br_na[l].astype(BF16),
                             w_out[l].astype(BF16), norm_ffn[l], peer_w_query[l].astype(BF16),
                             peer_sub_keys[l].astype(BF16))
        r2, e2, cc, e1 = _topk(st)
        x2 = _peer(hnt, peer_u, peer_v, l, r2, e2, cc, e1, h2, norm_final, final_norm=l == depth - 1)
    return x2.reshape(batch, seq, d)
```

```python
import functools

import numpy as np
import jax
import jax.numpy as jnp
from jax import lax
from jax.experimental import pallas as pl
from jax.experimental.pallas import tpu as pltpu

F32 = jnp.float32
BF16 = jnp.bfloat16

EPS = 1e-6
GRID_W = 64
NEG = -1e30

POOL_WINDOWS = (2, 4, 8, 16)
POOL_GROUP = 64
POOL_WIDTH = POOL_GROUP * len(POOL_WINDOWS)

RET_HEADS = 4
RET_DK = 64
RET_DV = 128
RET_QK = RET_HEADS * RET_DK
RET_V = RET_HEADS * RET_DV
ROPE_BASE = 10000.0

NA_HEADS = 4
NA_DH = 64
NA_WIDTH = NA_HEADS * NA_DH
NA_ROWS_MAX = 8
NA_COLS = 16

N_BRANCH = 3

PEER_KEYS = 128
PEER_HEADS = 8
PEER_DK = 128
PEER_TOPK = 16
GELU_HALF = 0.5

V7X_VMEM_LIMIT_BYTES = 56 * 1024 * 1024

INPROJ_TM = 256
RET_CHUNK = 256
NA_ROWS_PER_STEP = 8
NA_ROWS_PER_ITER = 8
MERGE_TM = 256
TOPK_TM = 1024
TOPK_SUB = 128
PEER_TM = 512
PEER_TE = 1024
PEER_LANES = 128


def _params(*sem):
    return pltpu.CompilerParams(dimension_semantics=sem, vmem_limit_bytes=V7X_VMEM_LIMIT_BYTES)


def _const_spec(shape):
    nd = len(shape)
    return pl.BlockSpec(shape, lambda *_: (0,) * nd)


def _inproj_kernel(x_ref, g_ref, w_ref, pool_ref, rq_ref, rk_ref, rv_ref, rg_ref,
                   nq_ref, nk_ref, nv_ref, gate_ref):
    x = x_ref[...]
    xn = x * lax.rsqrt(jnp.mean(x * x, axis=-1, keepdims=True) + EPS) * g_ref[...]
    xb = xn.astype(BF16)
    off = 0
    for ref, scale in ((pool_ref, None), (rq_ref, None), (rk_ref, None), (rv_ref, None), (rg_ref, None),
                       (nq_ref, NA_DH ** -0.5), (nk_ref, None), (nv_ref, None)):
        width = ref.shape[-1]
        y = jnp.dot(xb, w_ref[:, off:off + width], preferred_element_type=F32)
        if scale is not None:
            y = y * scale
        ref[...] = y.astype(ref.dtype)
        off += width
    d = x.shape[-1]
    for b in range(N_BRANCH):
        gate_ref[:, b * d:(b + 1) * d] = jnp.dot(
            xb, w_ref[:, off + b * d:off + (b + 1) * d], preferred_element_type=F32)


def _inproj(x2, g, w_in_bf):
    t, d = x2.shape
    tm = INPROJ_TM
    widths = (POOL_WIDTH, RET_QK, RET_QK, RET_V, RET_V, NA_WIDTH, NA_WIDTH, NA_WIDTH, N_BRANCH * d)
    dtypes = (F32, F32, F32, BF16, F32, BF16, BF16, BF16, F32)
    tok = lambda w: pl.BlockSpec((tm, w), lambda i: (i, 0))
    return pl.pallas_call(
        _inproj_kernel,
        grid=(t // tm,),
        in_specs=[tok(d), _const_spec((1, d)), _const_spec(w_in_bf.shape)],
        out_specs=[tok(w) for w in widths],
        out_shape=[jax.ShapeDtypeStruct((t, w), dt) for w, dt in zip(widths, dtypes)],
        compiler_params=_params("parallel"),
        name="inproj",
    )(x2, g.reshape(1, d), w_in_bf)


def _pool_kernel(p_ref, w_ref, scale_ref, o_ref, *, halves):
    x = p_ref[...]
    s = x.shape[0]
    t = lax.broadcasted_iota(jnp.int32, x.shape, 0)
    lane = lax.broadcasted_iota(jnp.int32, x.shape, 1)
    half = pl.program_id(1)

    def down(v, k):
        return jnp.where(t >= k, pltpu.roll(v, k, 0), 0.0)

    def up(v, k):
        return jnp.where(t < s - k, pltpu.roll(v, s - k, 0), 0.0)

    trail = {1: x}
    lead = {1: x}
    for m in (2, 4, 8):
        trail[m] = trail[m // 2] + down(trail[m // 2], m // 2)
        lead[m] = lead[m // 2] + up(lead[m // 2], m // 2)

    def window(hw):
        return down(trail[hw], 1) + lead[hw]

    def pick(lo_first, hi_first, lo_second, hi_second):
        first = jnp.where(lane < POOL_GROUP, lo_first, hi_first)
        second = jnp.where(lane < POOL_GROUP, lo_second, hi_second)
        return jnp.where(half == 0, first, second)

    hw = [w // 2 for w in POOL_WINDOWS]
    wsum = pick(window(hw[0]), window(hw[1]), window(hw[2]), window(hw[3]))
    hwv = pick(hw[0], hw[1], hw[2], hw[3])
    cnt = (jnp.minimum(t + hwv, s) - jnp.maximum(t - hwv, 0)).astype(F32)
    dlt = wsum / cnt - x
    y = jnp.dot(dlt.astype(BF16), w_ref[0], preferred_element_type=F32)
    o_ref[...] = y * scale_ref[0]


def _pool(p_pool, w_halves, scale_halves, batch, seq):
    t = p_pool.shape[0]
    halves = POOL_WIDTH // 128
    return pl.pallas_call(
        functools.partial(_pool_kernel, halves=halves),
        grid=(batch, halves),
        in_specs=[pl.BlockSpec((seq, 128), lambda b, h: (b, h)),
                  pl.BlockSpec((1, 128, 128), lambda b, h: (h, 0, 0)),
                  pl.BlockSpec((1, 1, 128), lambda b, h: (h, 0, 0))],
        out_specs=pl.BlockSpec((seq, 128), lambda b, h: (b, h)),
        out_shape=jax.ShapeDtypeStruct((t, POOL_WIDTH), F32),
        compiler_params=_params("parallel", "parallel"),
        name="pool",
    )(p_pool, w_halves, scale_halves)


def _rotary(x, cos, sin_signed, first_half):
    width = x.shape[-1]
    half = RET_DK // 2
    swapped = jnp.where(first_half, pltpu.roll(x, width - half, 1), pltpu.roll(x, half, 1))
    return x * cos + swapped * sin_signed


def _retention_kernel(lg_ref, q_ref, k_ref, v_ref, cos_ref, sin_ref, *rest, reverse):
    if reverse:
        fwd_ref, g_ref, o_ref, state_ref = rest
    else:
        o_ref, state_ref = rest
    c = q_ref.shape[0]

    @pl.when(pl.program_id(1) == 0)
    def _():
        state_ref[...] = jnp.zeros_like(state_ref)

    lane = lax.broadcasted_iota(jnp.int32, (c, RET_QK), 1)
    first_half = (lane % RET_DK) < (RET_DK // 2)
    cos = cos_ref[...]
    sin = sin_ref[...]
    q = _rotary(q_ref[...], cos, sin, first_half)
    k = _rotary(k_ref[...], cos, sin, first_half) * (RET_DK ** -0.5)
    v = v_ref[...]

    row = lax.broadcasted_iota(jnp.int32, (c, c), 0)
    col = lax.broadcasted_iota(jnp.int32, (c, c), 1)
    pos = lax.broadcasted_iota(jnp.int32, (c, 1), 0).astype(F32)
    if reverse:
        diff = col - row
        mask = diff > 0
        q_pow = c - pos
        k_pow = pos
    else:
        diff = row - col
        mask = diff >= 0
        q_pow = pos + 1.0
        k_pow = (c - 1) - pos
    dist = jnp.where(mask, diff, 0).astype(F32)
    direction = 1 if reverse else 0

    for h in range(RET_HEADS):
        lg = lg_ref[direction, h]
        qh = q[:, h * RET_DK:(h + 1) * RET_DK]
        kh = k[:, h * RET_DK:(h + 1) * RET_DK]
        vh = v[:, h * RET_DV:(h + 1) * RET_DV]
        decay = jnp.where(mask, jnp.exp(lg * dist), 0.0)
        scores = lax.dot_general(qh.astype(BF16), kh.astype(BF16), (((1,), (1,)), ((), ())),
                                 preferred_element_type=F32) * decay
        intra = jnp.dot(scores.astype(BF16), vh, preferred_element_type=F32)
        q_dec = qh * jnp.exp(lg * q_pow)
        k_dec = kh * jnp.exp(lg * k_pow)
        state = state_ref[h]
        cross = jnp.dot(q_dec.astype(BF16), state.astype(BF16), preferred_element_type=F32)
        chunk_kv = lax.dot_general(k_dec.astype(BF16), vh, (((0,), (0,)), ((), ())),
                                   preferred_element_type=F32)
        chunk_decay = jnp.exp(lg * jnp.full((1, RET_DV), float(c), F32))
        state_ref[h] = state * chunk_decay + chunk_kv
        out = intra + cross
        sl = slice(h * RET_DV, (h + 1) * RET_DV)
        if reverse:
            y = fwd_ref[:, sl] + out
            y = y * lax.rsqrt(jnp.mean(y * y, axis=-1, keepdims=True) + EPS)
            g = g_ref[:, sl]
            o_ref[:, sl] = (g / (1.0 + jnp.exp(-g))) * y
        else:
            o_ref[:, sl] = out


def _retention(lg, r_q, r_k, r_v, r_g, cos_t, sin_t, batch, seq):
    t = r_q.shape[0]
    c = RET_CHUNK
    n = seq // c
    scratch = [pltpu.VMEM((RET_HEADS, RET_DK, RET_DV), F32)]
    smem = pl.BlockSpec(memory_space=pltpu.SMEM)

    def specs(chunk_of):
        tok = lambda w: pl.BlockSpec((c, w), lambda b, i: (b * n + chunk_of(i), 0))
        tab = pl.BlockSpec((c, RET_QK), lambda b, i: (chunk_of(i), 0))
        return tok, tab

    tok, tab = specs(lambda i: i)
    fwd = pl.pallas_call(
        functools.partial(_retention_kernel, reverse=False),
        grid=(batch, n),
        in_specs=[smem, tok(RET_QK), tok(RET_QK), tok(RET_V), tab, tab],
        out_specs=tok(RET_V),
        out_shape=jax.ShapeDtypeStruct((t, RET_V), F32),
        scratch_shapes=scratch,
        compiler_params=_params("parallel", "arbitrary"),
        name="retention_fwd",
    )(lg, r_q, r_k, r_v, cos_t, sin_t)
    tok, tab = specs(lambda i: n - 1 - i)
    return pl.pallas_call(
        functools.partial(_retention_kernel, reverse=True),
        grid=(batch, n),
        in_specs=[smem, tok(RET_QK), tok(RET_QK), tok(RET_V), tab, tab, tok(RET_V), tok(RET_V)],
        out_specs=tok(RET_V),
        out_shape=jax.ShapeDtypeStruct((t, RET_V), F32),
        scratch_shapes=scratch,
        compiler_params=_params("parallel", "arbitrary"),
        name="retention_bwd",
    )(lg, r_q, r_k, r_v, cos_t, sin_t, fwd, r_g)


def _na_kernel(q_ref, k_ref, v_ref, bias_ref, o_ref, *, rows, kr):
    step = pl.program_id(1)
    rps = q_ref.shape[0] // GRID_W
    nkeys = kr * GRID_W
    lo = kr // 2
    lane_head = lax.broadcasted_iota(jnp.int32, (GRID_W, NA_WIDTH), 1) // NA_DH
    head_lanes = [lane_head == h for h in range(NA_HEADS)]

    def body(it, carry):
        jobs = []
        for j in range(NA_ROWS_PER_ITER):
            i = it * NA_ROWS_PER_ITER + j
            r = step * rps + i
            r0 = jnp.clip(r - lo, 0, rows - kr)
            ver = jnp.where(r < lo, r, jnp.where(r > rows - kr + lo, r - (rows - kr), lo))
            key_rows = pl.ds(pl.multiple_of(r0 * GRID_W, GRID_W), nkeys)
            qq = q_ref[pl.ds(pl.multiple_of(i * GRID_W, GRID_W), GRID_W), :]
            kk = k_ref[key_rows, :]
            scores = [lax.dot_general(jnp.where(mask, qq, jnp.zeros((), qq.dtype)), kk, (((1,), (1,)), ((), ())),
                                      preferred_element_type=F32) + bias_ref[ver, h]
                      for h, mask in enumerate(head_lanes)]
            jobs.append((i, key_rows, scores))
        probs = []
        for i, key_rows, scores in jobs:
            ps = []
            for s in scores:
                p = jnp.exp(s - jnp.max(s, axis=-1, keepdims=True))
                ps.append((p / jnp.sum(p, axis=-1, keepdims=True)).astype(BF16))
            probs.append((i, key_rows, ps))
        for i, key_rows, ps in probs:
            vv = v_ref[key_rows, :]
            out = jnp.zeros((GRID_W, NA_WIDTH), F32)
            for p, mask in zip(ps, head_lanes):
                out = jnp.where(mask, jnp.dot(p, vv, preferred_element_type=F32), out)
            o_ref[pl.ds(pl.multiple_of(i * GRID_W, GRID_W), GRID_W), :] = out
        return carry

    lax.fori_loop(0, rps // NA_ROWS_PER_ITER, body, 0)


def _na_bias_table(rpb, rows, kr):
    lo = kr // 2
    reps = list(range(lo)) + [lo] + list(range(rows - kr + lo + 1, rows))
    c = np.arange(GRID_W)
    col_start = np.clip(c - NA_COLS // 2, 0, GRID_W - NA_COLS)
    col_in = (c[None, :] >= col_start[:, None]) & (c[None, :] < col_start[:, None] + NA_COLS)
    dj = np.clip(c[None, :] - c[:, None], -(NA_COLS - 1), NA_COLS - 1) + (NA_COLS - 1)
    di0 = [int(np.clip(r - lo, 0, rows - kr)) - r + (NA_ROWS_MAX - 1) for r in reps]
    rpb_rows = jnp.stack([rpb[:, d0:d0 + kr, :] for d0 in di0]).astype(F32)
    offsets = np.arange(2 * NA_COLS - 1)
    pick = (dj[None] == offsets[:, None, None]) & col_in[None]

    def by_matmul(vals):
        out = jnp.einsum('vhij,jck->vhcik', vals, jnp.asarray(pick.astype(np.float32)),
                         precision=lax.Precision.HIGHEST)
        return out + jnp.asarray(np.where(col_in, 0.0, NEG).astype(np.float32))[None, None, :, None, :]

    def by_select(vals):
        out = jnp.full((len(reps), NA_HEADS, GRID_W, kr, GRID_W), NEG, F32)
        for j in offsets:
            out = jnp.where(jnp.asarray(pick[j])[None, None, :, None, :], vals[:, :, None, :, j, None], out)
        return out

    bias = lax.cond(jnp.all(jnp.isfinite(rpb_rows)), by_matmul, by_select, rpb_rows)
    return bias.reshape(len(reps), NA_HEADS, GRID_W, kr * GRID_W)


def _na(n_q, n_k, n_v, bias, batch, seq):
    t = n_q.shape[0]
    rows = seq // GRID_W
    kr = min(NA_ROWS_MAX, rows)
    rps = min(NA_ROWS_PER_STEP, rows)
    blk = rps * GRID_W
    steps = rows // rps
    return pl.pallas_call(
        functools.partial(_na_kernel, rows=rows, kr=kr),
        grid=(batch, steps),
        in_specs=[pl.BlockSpec((blk, NA_WIDTH), lambda b, i: (b * steps + i, 0)),
                  pl.BlockSpec((seq, NA_WIDTH), lambda b, i: (b, 0)),
                  pl.BlockSpec((seq, NA_WIDTH), lambda b, i: (b, 0)),
                  _const_spec(bias.shape)],
        out_specs=pl.BlockSpec((blk, NA_WIDTH), lambda b, i: (b * steps + i, 0)),
        out_shape=jax.ShapeDtypeStruct((t, NA_WIDTH), F32),
        compiler_params=_params("parallel", "arbitrary"),
        name="natten",
    )(n_q, n_k, n_v, bias)


def _merge_kernel(x_ref, yp_ref, yr_ref, yn_ref, gate_ref, wp_ref, wr_ref, wn_ref, wo_ref,
                  gf_ref, wq_ref, keys_ref, h_ref, hnt_ref, st_ref):
    d = x_ref.shape[-1]
    merged = None
    for b, (y_ref, w_ref) in enumerate(((yp_ref, wp_ref), (yr_ref, wr_ref), (yn_ref, wn_ref))):
        gate = 1.0 / (1.0 + jnp.exp(-gate_ref[:, b * d:(b + 1) * d]))
        term = gate * jnp.dot(y_ref[...].astype(BF16), w_ref[...], preferred_element_type=F32)
        merged = term if merged is None else merged + term
    h = x_ref[...] + jnp.dot(merged.astype(BF16), wo_ref[...], preferred_element_type=F32)
    h_ref[...] = h
    hn = h * lax.rsqrt(jnp.mean(h * h, axis=-1, keepdims=True) + EPS) * gf_ref[...]
    hnb = hn.astype(BF16)
    hnt_ref[...] = hn.T.astype(BF16)
    q = jnp.dot(hnb, wq_ref[...], preferred_element_type=F32).astype(BF16)
    for hh in range(PEER_HEADS):
        for p in range(2):
            col = (hh * 2 + p) * PEER_DK
            st_ref[hh, p] = lax.dot_general(keys_ref[p], q[:, col:col + PEER_DK],
                                            (((1,), (1,)), ((), ())), preferred_element_type=F32)


def _merge(x2, y_pool, y_ret, y_na, gates, wp, wr, wn, wo, g_ffn, wq, keys):
    t, d = x2.shape
    tm = MERGE_TM
    tok = lambda w: pl.BlockSpec((tm, w), lambda i: (i, 0))
    return pl.pallas_call(
        _merge_kernel,
        grid=(t // tm,),
        in_specs=[tok(d), tok(POOL_WIDTH), tok(RET_V), tok(NA_WIDTH), tok(N_BRANCH * d),
                  _const_spec(wp.shape), _const_spec(wr.shape), _const_spec(wn.shape), _const_spec(wo.shape),
                  _const_spec((1, d)), _const_spec(wq.shape), _const_spec(keys.shape)],
        out_specs=[tok(d),
                   pl.BlockSpec((d, tm), lambda i: (0, i)),
                   pl.BlockSpec((PEER_HEADS, 2, PEER_KEYS, tm), lambda i: (0, 0, 0, i))],
        out_shape=[jax.ShapeDtypeStruct((t, d), F32),
                   jax.ShapeDtypeStruct((d, t), BF16),
                   jax.ShapeDtypeStruct((PEER_HEADS, 2, PEER_KEYS, t), F32)],
        compiler_params=_params("parallel"),
        name="merge",
    )(x2, y_pool, y_ret, y_na, gates, wp, wr, wn, wo, g_ffn.reshape(1, d), wq, keys)


def _topk_exact_tile(s_ref, r2_ref, e2_ref, cc_ref, e1_ref, ds):
    nk = s_ref.shape[2]
    k_top = PEER_TOPK
    w = ds.stop - ds.start
    key = lax.broadcasted_iota(jnp.int32, (nk, w), 0)
    arow = lax.broadcasted_iota(jnp.int32, (k_top, w), 0)
    neg_inf = -jnp.inf

    def top_ranks(s):
        def body(a, carry):
            s, rank, vals = carry
            m = jnp.max(s, axis=0, keepdims=True)
            idx = jnp.min(jnp.where(s == m, key, nk), axis=0, keepdims=True)
            hit = key == idx
            return (jnp.where(hit, neg_inf, s), jnp.where(hit, a, rank), jnp.where(arow == a, m, vals))
        init = (s, jnp.full((nk, w), k_top, jnp.int32), jnp.zeros((k_top, w), F32))
        _, rank, vals = lax.fori_loop(0, k_top, body, init)
        return rank, vals

    s1 = s_ref[0, 0, :, ds]
    s2 = s_ref[0, 1, :, ds]
    r1, v1 = top_ranks(s1)
    r2, v2 = top_ranks(s2)
    top = v1[0:1] + v2[0:1]

    def merge(_, carry):
        cnt, front, z = carry
        m = jnp.max(front, axis=0, keepdims=True)
        a_star = jnp.min(jnp.where(front == m, arow, k_top), axis=0, keepdims=True)
        hit = arow == a_star
        cnt = jnp.where(hit, cnt + 1, cnt)
        nxt = jnp.sum(jnp.where(hit, cnt, 0), axis=0, keepdims=True)
        v2_next = jnp.sum(jnp.where(arow == nxt, v2, 0.0), axis=0, keepdims=True)
        cand = jnp.where(nxt < k_top, v1 + v2_next, neg_inf)
        return cnt, jnp.where(hit, cand, front), z + jnp.exp(m - top)
    init = (jnp.zeros((k_top, w), jnp.int32), v1 + v2[0:1], jnp.zeros((1, w), F32))
    cnt, _, z = lax.fori_loop(0, k_top, merge, init)

    cc = jnp.zeros((nk, w), jnp.int32)
    for a in range(k_top):
        cc = jnp.where(r1 == a, cnt[a:a + 1], cc)
    r2_ref[0, :, ds] = r2.astype(F32).astype(BF16)
    cc_ref[0, :, ds] = cc.astype(F32)
    e2_ref[0, :, ds] = jnp.exp(s2 - v2[0:1]).astype(BF16)
    e1_ref[0, :, ds] = jnp.exp(s1 - v1[0:1]) * (GELU_HALF / z)


def _topk_kernel(s_ref, r2_ref, e2_ref, cc_ref, e1_ref, v1_ref, v2_ref):
    nk = s_ref.shape[2]
    tm = s_ref.shape[3]
    k_top = PEER_TOPK
    w = TOPK_SUB
    subs = [slice(j * w, (j + 1) * w) for j in range(tm // w)]
    neg_inf = -jnp.inf
    no_tie_rank_sum = float(sum(range(k_top)) + k_top * (nk - k_top))
    mark_base, mark_step = 2.0 ** 100, 2.0 ** 96
    lowest_score = -(2.0 ** 99)

    redo = jnp.zeros((1, w), F32)
    for j, ds in enumerate(subs):
        for p, (rank_ref, vals_ref) in enumerate(((cc_ref, v1_ref), (r2_ref, v2_ref))):
            s = s_ref[0, p, :, ds]
            floor_ok = jnp.min(s, axis=0, keepdims=True) > lowest_score
            for a in range(k_top):
                m = jnp.max(s, axis=0, keepdims=True)
                s = jnp.where(s == m, -(mark_base + a * mark_step), s)
                vals_ref[a, j:j + 1, :] = m
            decoded = jnp.floor(s * (-1.0 / mark_step) - (mark_base / mark_step - 0.5))
            rank = jnp.where(s < lowest_score, decoded, float(k_top))
            exact = floor_ok & (jnp.sum(rank, axis=0, keepdims=True) == no_tie_rank_sum)
            redo = redo + jnp.where(exact, 0.0, 1.0)
            rank_ref[0, :, ds] = rank.astype(rank_ref.dtype)

    v1 = v1_ref[...]
    v2 = v2_ref[...]
    arow = lax.broadcasted_iota(jnp.int32, v1.shape, 0).astype(F32)
    top = v1[0:1] + v2[0:1]
    cnt = jnp.zeros(v1.shape, F32)
    front = v1 + v2[0:1]
    z = jnp.zeros(top.shape, F32)
    for _ in range(k_top):
        m = jnp.max(front, axis=0, keepdims=True)
        a_star = jnp.min(jnp.where(front == m, arow, float(k_top)), axis=0, keepdims=True)
        hit = arow == a_star
        cnt = jnp.where(hit, cnt + 1.0, cnt)
        nxt = jnp.sum(jnp.where(hit, cnt, 0.0), axis=0, keepdims=True)
        v2_next = jnp.sum(jnp.where(arow == nxt, v2, 0.0), axis=0, keepdims=True)
        cand = jnp.where(nxt < float(k_top), v1 + v2_next, neg_inf)
        front = jnp.where(hit, cand, front)
        z = z + jnp.exp(m - top)
    inv_z = GELU_HALF / z

    for j, ds in enumerate(subs):
        r1 = cc_ref[0, :, ds]
        cc = jnp.zeros((nk, w), F32)
        for a in range(k_top):
            cc = jnp.where(r1 == float(a), cnt[a, j:j + 1, :], cc)
        cc_ref[0, :, ds] = cc
        e1_ref[0, :, ds] = jnp.exp(s_ref[0, 0, :, ds] - v1[0, j:j + 1, :]) * inv_z[0, j:j + 1, :]
        e2_ref[0, :, ds] = jnp.exp(s_ref[0, 1, :, ds] - v2[0, j:j + 1, :]).astype(BF16)

    @pl.when(jnp.sum(redo) > 0.0)
    def _():
        for ds in subs:
            _topk_exact_tile(s_ref, r2_ref, e2_ref, cc_ref, e1_ref, ds)


def _topk(st):
    heads, _, nk, t = st.shape
    tm = TOPK_TM
    out = pl.BlockSpec((1, nk, tm), lambda i, h: (h, 0, i))
    return pl.pallas_call(
        _topk_kernel,
        grid=(t // tm, heads),
        in_specs=[pl.BlockSpec((1, 2, nk, tm), lambda i, h: (h, 0, 0, i))],
        out_specs=[out] * 4,
        out_shape=[jax.ShapeDtypeStruct((heads, nk, t), dt) for dt in (BF16, BF16, F32, F32)],
        scratch_shapes=[pltpu.VMEM((PEER_TOPK, tm // TOPK_SUB, TOPK_SUB), F32)] * 2,
        compiler_params=_params("parallel", "parallel"),
        name="peer_topk",
    )(st)


def _peer_kernel(hnt_ref, u_ref, v_ref, r2_in_ref, e2_in_ref, cc_ref, e1_ref, h_ref, gn_ref, o_ref,
                 acc_ref, a_ref, ht_ref, r2_ref, e2_ref, *, final_norm):
    e = pl.program_id(1)
    te = u_ref.shape[1]
    tm = hnt_ref.shape[1]
    nk = r2_ref.shape[1]
    rows_per_step = te // nk

    @pl.when(e == 0)
    def _():
        acc_ref[...] = jnp.zeros_like(acc_ref)
        r2_ref[...] = r2_in_ref[...]
        e2_ref[...] = e2_in_ref[...]

    a_ref[...] = lax.dot_general(u_ref[0], hnt_ref[...], (((1,), (0,)), ((), ())),
                                 preferred_element_type=F32)
    for ii in range(rows_per_step):
        rows = slice(ii * nk, (ii + 1) * nk)
        cc_rows = [cc_ref[hh, ii:ii + 1, :] for hh in range(PEER_HEADS)]
        e1_rows = [e1_ref[hh, ii:ii + 1, :] for hh in range(PEER_HEADS)]
        for lt in range(tm // PEER_LANES):
            lanes = slice(lt * PEER_LANES, (lt + 1) * PEER_LANES)
            gate = None
            for hh in range(PEER_HEADS):
                cc = jnp.broadcast_to(cc_rows[hh][:, lanes], (nk, PEER_LANES)).astype(BF16)
                e1 = jnp.broadcast_to(e1_rows[hh][:, lanes], (nk, PEER_LANES)).astype(BF16)
                term = jnp.where(r2_ref[hh, :, lanes] < cc, e2_ref[hh, :, lanes], jnp.zeros((), BF16)) * e1
                gate = term if gate is None else gate + term
            a = a_ref[rows, lanes]
            act = a * (1.0 + lax.erf(a * np.float32(1.0 / np.sqrt(2.0))))
            ht_ref[rows, lanes] = act.astype(BF16) * gate
    acc_ref[...] += lax.dot_general(ht_ref[...].T, v_ref[0], (((1,), (0,)), ((), ())),
                                    preferred_element_type=F32)

    @pl.when(e == pl.num_programs(1) - 1)
    def _():
        y = h_ref[...] + acc_ref[...]
        if final_norm:
            y = y * lax.rsqrt(jnp.mean(y * y, axis=-1, keepdims=True) + EPS) * gn_ref[...]
        o_ref[...] = y


def _peer(hnt, u_all, v_all, layer, r2, e2, cc, e1, h2, g_norm, final_norm):
    d, t = hnt.shape
    n_exp = u_all.shape[1]
    heads, nk, _ = r2.shape
    tm, te = PEER_TM, PEER_TE
    tab = pl.BlockSpec((heads, nk, tm), lambda i, e: (0, 0, i))
    row_tab = pl.BlockSpec((heads, te // nk, tm), lambda i, e: (0, e, i))
    return pl.pallas_call(
        functools.partial(_peer_kernel, final_norm=final_norm),
        grid=(t // tm, n_exp // te),
        in_specs=[pl.BlockSpec((d, tm), lambda i, e: (0, i)),
                  pl.BlockSpec((1, te, d), lambda i, e: (layer, e, 0)),
                  pl.BlockSpec((1, te, d), lambda i, e: (layer, e, 0)),
                  tab, tab, row_tab, row_tab,
                  pl.BlockSpec((tm, d), lambda i, e: (i, 0)),
                  pl.BlockSpec((1, d), lambda i, e: (0, 0))],
        out_specs=pl.BlockSpec((tm, d), lambda i, e: (i, 0)),
        out_shape=jax.ShapeDtypeStruct((t, d), F32),
        scratch_shapes=[pltpu.VMEM((tm, d), F32), pltpu.VMEM((te, tm), F32), pltpu.VMEM((te, tm), BF16),
                        pltpu.VMEM((heads, nk, tm), BF16), pltpu.VMEM((heads, nk, tm), BF16)],
        compiler_params=_params("parallel", "arbitrary"),
        name="peer_experts",
    )(hnt, u_all, v_all, r2, e2, cc, e1, h2, g_norm.reshape(1, d))


def _rotary_tables(seq):
    half = RET_DK // 2
    inv = 1.0 / (ROPE_BASE ** jnp.linspace(0.0, 1.0, half, dtype=F32))
    ang = jnp.arange(seq, dtype=F32)[:, None] * inv[None, :]
    cos, sin = jnp.cos(ang), jnp.sin(ang)
    cos_t = jnp.tile(jnp.concatenate([cos, cos], axis=-1), (1, RET_HEADS))
    sin_t = jnp.tile(jnp.concatenate([-sin, sin], axis=-1), (1, RET_HEADS))
    return cos_t, sin_t


def _pool_weights(pool_w, pool_scale):
    halves = POOL_WIDTH // 128
    per = 128 // POOL_GROUP
    blocks = []
    for hf in range(halves):
        blk = jnp.zeros((128, 128), F32)
        for g in range(per):
            sl = slice(g * POOL_GROUP, (g + 1) * POOL_GROUP)
            blk = blk.at[sl, sl].set(pool_w[hf * per + g])
        blocks.append(blk)
    return jnp.stack(blocks).astype(BF16), pool_scale.reshape(halves, 1, 128)


def kernel(x, norm_mix, w_in, pool_w, pool_scale, ret_decay, na_rpb, w_br_pool, w_br_ret, w_br_na,
           w_out, norm_ffn, peer_w_query, peer_sub_keys, peer_u, peer_v, norm_final):
    batch, seq, d = x.shape
    depth = w_in.shape[0]
    rows = seq // GRID_W
    kr = min(NA_ROWS_MAX, rows)
    cos_t, sin_t = _rotary_tables(seq)
    x2 = x.reshape(batch * seq, d)
    for l in range(depth):
        pool, r_q, r_k, r_v, r_g, n_q, n_k, n_v, gates = _inproj(x2, norm_mix[l], w_in[l].astype(BF16))
        w_halves, scale_halves = _pool_weights(pool_w[l], pool_scale[l])
        y_pool = _pool(pool, w_halves, scale_halves, batch, seq)
        lg = jax.nn.log_sigmoid(ret_decay[l].astype(F32))
        y_ret = _retention(lg, r_q, r_k, r_v, r_g, cos_t, sin_t, batch, seq)
        y_na = _na(n_q, n_k, n_v, _na_bias_table(na_rpb[l], rows, kr), batch, seq)
        h2, hnt, st = _merge(x2, y_pool, y_ret, y_na, gates,
                             w_br_pool[l].astype(BF16), w_br_ret[l].astype(BF16), w_br_na[l].astype(BF16),
                             w_out[l].astype(BF16), norm_ffn[l], peer_w_query[l].astype(BF16),
                             peer_sub_keys[l].astype(BF16))
        r2, e2, cc, e1 = _topk(st)
        x2 = _peer(hnt, peer_u, peer_v, l, r2, e2, cc, e1, h2, norm_final, final_norm=l == depth - 1)
    return x2.reshape(batch, seq, d)
```
